```python
import math
import jax, jax.numpy as jnp
from jax import lax
import numpy as np

D_MODEL = 1024
BATCH = 16
SEQ = 2048
DEPTH = 1

CHUNK = 64
Q_BLOCK = 128
MLA_HEADS = 8
MLA_Q_RANK = 256
MLA_KV_RANK = 128
MLA_NOPE = 64
MLA_ROPE = 32
MLA_V = 64
ROPE_THETA = 10000.0
DSA_HEADS = 8
DSA_HEAD_DIM = 64
IDX_HEADS = 8
IDX_DIM = 32
IDX_TOPK_MAX = 256
REL_BUCKETS = 32
REL_MAX_DIST = 128
N_GROUPS = 4
EXPERTS_PER_GROUP = 8
N_EXPERTS = N_GROUPS * EXPERTS_PER_GROUP
TOP_K_IN_GROUP = 2
EXPERT_HIDDEN = 256
LN_EPS = 1e-5
RMS_EPS = 1e-6
DEEPNORM_ALPHA = (2.0 * DEPTH) ** 0.25
DEEPNORM_BETA = (8.0 * DEPTH) ** -0.25
NEG = -1e30
IN_SIZES = (MLA_Q_RANK, MLA_KV_RANK, MLA_ROPE, DSA_HEADS * DSA_HEAD_DIM, DSA_HEAD_DIM, DSA_HEAD_DIM, IDX_HEADS * IDX_DIM, IDX_DIM, IDX_HEADS)
IN_WIDTH = MLA_Q_RANK + MLA_KV_RANK + MLA_ROPE + DSA_HEADS * DSA_HEAD_DIM + 2 * DSA_HEAD_DIM + IDX_HEADS * IDX_DIM + IDX_DIM + IDX_HEADS

kernel_name = 'hybrid_mla_dsa_hmoe_block'


def layer_norm(x, g, b):
    xf = x.astype(jnp.float32)
    mu = jnp.mean(xf, axis=-1, keepdims=True)
    var = jnp.mean(jnp.square(xf - mu), axis=-1, keepdims=True)
    return ((xf - mu) * lax.rsqrt(var + LN_EPS) * g + b).astype(x.dtype)


def rms_norm(x, g):
    xf = x.astype(jnp.float32)
    return (xf * lax.rsqrt(jnp.mean(jnp.square(xf), axis=-1, keepdims=True) + RMS_EPS) * g).astype(x.dtype)


def apply_rope(x, pos):
    half = x.shape[-1] // 2
    inv = ROPE_THETA ** (-jnp.arange(half, dtype=jnp.float32) / half)
    ang = pos.astype(jnp.float32)[..., None] * inv
    cos = jnp.cos(ang)[:, :, None, :]
    sin = jnp.sin(ang)[:, :, None, :]
    x1 = x[..., :half].astype(jnp.float32)
    x2 = x[..., half:].astype(jnp.float32)
    return jnp.concatenate([x1 * cos - x2 * sin, x1 * sin + x2 * cos], axis=-1).astype(x.dtype)


def t5_bucket(rel):
    nb = REL_BUCKETS // 2
    max_exact = nb // 2
    ret = jnp.where(rel > 0, nb, 0)
    n = jnp.abs(rel)
    large = max_exact + (jnp.log(jnp.maximum(n, 1).astype(jnp.float32) / max_exact)
                         / math.log(REL_MAX_DIST / max_exact) * (nb - max_exact)).astype(jnp.int32)
    large = jnp.minimum(large, nb - 1)
    return ret + jnp.where(n < max_exact, n, large)


def to_blocks(a):
    B, T = a.shape[:2]
    a = a.reshape((B, T // Q_BLOCK, Q_BLOCK) + a.shape[2:])
    return jnp.swapaxes(a, 0, 1)


def from_blocks(a):
    a = jnp.swapaxes(a, 0, 1)
    return a.reshape((a.shape[0], a.shape[1] * a.shape[2]) + a.shape[3:])


def mla_branch(c_q, c_kv, k_rope_raw, pos, q_norm_g, w_uq, kv_norm_g, w_uk, w_uv):
    B, T = pos.shape
    q = (rms_norm(c_q, q_norm_g) @ w_uq).reshape(B, T, MLA_HEADS, MLA_NOPE + MLA_ROPE)
    q = jnp.concatenate([q[..., :MLA_NOPE], apply_rope(q[..., MLA_NOPE:], pos)], axis=-1)
    ckv = rms_norm(c_kv, kv_norm_g)
    k_nope = (ckv @ w_uk).reshape(B, T, MLA_HEADS, MLA_NOPE)
    v = (ckv @ w_uv).reshape(B, T, MLA_HEADS, MLA_V)
    k_rope = apply_rope(k_rope_raw[:, :, None, :], pos)
    k = jnp.concatenate([k_nope, jnp.broadcast_to(k_rope, (B, T, MLA_HEADS, MLA_ROPE))], axis=-1)
    scale = (MLA_NOPE + MLA_ROPE) ** -0.5
    k_chunk = pos // CHUNK

    def one_block(args):
        qb, pb = args
        s = jnp.einsum('bqhd,bshd->bhqs', qb, k).astype(jnp.float32) * scale
        allowed = k_chunk[:, None, :] <= (pb // CHUNK)[:, :, None]
        s = jnp.where(allowed[:, None], s, NEG)
        p = jax.nn.softmax(s, axis=-1).astype(v.dtype)
        return jnp.einsum('bhqs,bshd->bqhd', p, v)

    o = from_blocks(lax.map(one_block, (to_blocks(q), to_blocks(pos))))
    return o.reshape(B, T, MLA_HEADS * MLA_V)


def dsa_branch(q_b, k_b, v_b, q_idx, k_idx, w_idx, pos, rel_bias):
    B, T = pos.shape
    L = k_b.shape[1]
    topk = min(IDX_TOPK_MAX, L // 4)
    q = q_b.reshape(B, T, DSA_HEADS, DSA_HEAD_DIM)
    qi = q_idx.reshape(B, T, IDX_HEADS, IDX_DIM)
    w = w_idx * IDX_HEADS ** -0.5
    k_chunk = pos // CHUNK

    def one_block(args):
        qb, qib, wb, pb = args
        rel = jax.nn.relu(jnp.einsum('bqhd,bsd->bqhs', qib, k_idx).astype(jnp.float32) * IDX_DIM ** -0.5)
        score = jnp.einsum('bqh,bqhs->bqs', wb.astype(jnp.float32), rel)
        q_chunk = pb // CHUNK
        score = jnp.where(k_chunk[:, None, :] <= q_chunk[:, :, None], score, NEG)
        _, sel = lax.top_k(score, topk)
        k_sel = jax.vmap(lambda a, i: a[i])(k_b, sel)
        v_sel = jax.vmap(lambda a, i: a[i])(v_b, sel)
        pos_sel = jax.vmap(lambda a, i: a[i])(pos, sel)
        valid = (pos_sel // CHUNK) <= q_chunk[:, :, None]
        bias = jnp.transpose(rel_bias[t5_bucket(pos_sel - pb[:, :, None])], (0, 3, 1, 2))
        s = jnp.einsum('bqhd,bqkd->bhqk', qb, k_sel).astype(jnp.float32) * DSA_HEAD_DIM ** -0.5 + bias
        s = jnp.where(valid[:, None], s, NEG)
        p = jax.nn.softmax(s, axis=-1).astype(v_sel.dtype)
        return jnp.einsum('bhqk,bqkd->bqhd', p, v_sel)

    o = from_blocks(lax.map(one_block, (to_blocks(q), to_blocks(qi), to_blocks(w), to_blocks(pos))))
    return o.reshape(B, T, DSA_HEADS * DSA_HEAD_DIM)


def hierarchical_moe(h, w_grp, b_grp, w_rt, b_rt, w_eg, w_eu, w_ed):
    B, T, _ = h.shape
    g_logits = (h @ w_grp).astype(jnp.float32) + b_grp
    p_grp = jax.nn.softmax(g_logits, axis=-1)
    g_sel = jnp.argmax(g_logits, axis=-1)
    g_w = jnp.take_along_axis(p_grp, g_sel[..., None], axis=-1)
    e_logits = ((h @ w_rt).astype(jnp.float32) + b_rt).reshape(B, T, N_GROUPS, EXPERTS_PER_GROUP)
    e_logits = jnp.take_along_axis(e_logits, g_sel[..., None, None], axis=2)[:, :, 0, :]
    p_e = jax.nn.softmax(e_logits, axis=-1)
    top_v, top_i = lax.top_k(p_e, TOP_K_IN_GROUP)
    top_v = top_v / jnp.sum(top_v, axis=-1, keepdims=True)
    expert_id = g_sel[..., None] * EXPERTS_PER_GROUP + top_i
    comb = jnp.sum(jax.nn.one_hot(expert_id, N_EXPERTS, dtype=jnp.float32) * (g_w * top_v)[..., None], axis=-2)
    comb = comb.astype(h.dtype)

    def per_row(args):
        hr, cr = args
        a = jnp.einsum('td,edf->tef', hr, w_eg)
        u = jnp.einsum('td,edf->tef', hr, w_eu)
        hid = jax.nn.silu(a) * u * cr[..., None]
        return jnp.einsum('tef,efd->td', hid, w_ed)

    return lax.map(per_row, (h, comb))


def setup_inputs(seed: int = 0) -> dict:
    key = jax.random.key(seed)
    ks = jax.random.split(key, 32)
    f32 = jnp.float32
    D = D_MODEL
    nrm = lambda k, shape, s: jax.random.normal(k, shape, f32) * s
    gain = lambda k, shape: 1.0 + 0.02 * jax.random.normal(k, shape, f32)
    x = jax.random.normal(ks[0], (BATCH, SEQ, D), f32)
    start = jax.random.randint(ks[1], (BATCH, 1), 0, 64, dtype=jnp.int32) * CHUNK
    positions = (start + jnp.arange(SEQ, dtype=jnp.int32)[None, :]).astype(jnp.int32)
    offs = np.cumsum((0,) + IN_SIZES)
    v_off = int(offs[5])
    col_scale = jnp.concatenate([jnp.ones((v_off,), f32), jnp.full((DSA_HEAD_DIM,), DEEPNORM_BETA, f32),
                                 jnp.ones((IN_WIDTH - v_off - DSA_HEAD_DIM,), f32)])
    L = DEPTH
    beta = DEEPNORM_BETA
    return {
        'x': x,
        'positions': positions,
        'ln0_g': gain(ks[2], (D,)),
        'ln0_b': nrm(ks[3], (D,), 0.02),
        'w_in': nrm(ks[4], (L, D, IN_WIDTH), D ** -0.5) * col_scale,
        'q_norm_g': gain(ks[5], (L, MLA_Q_RANK)),
        'w_uq': nrm(ks[6], (L, MLA_Q_RANK, MLA_HEADS * (MLA_NOPE + MLA_ROPE)), MLA_Q_RANK ** -0.5),
        'kv_norm_g': gain(ks[7], (L, MLA_KV_RANK)),
        'w_uk': nrm(ks[8], (L, MLA_KV_RANK, MLA_HEADS * MLA_NOPE), MLA_KV_RANK ** -0.5),
        'w_uv': nrm(ks[9], (L, MLA_KV_RANK, MLA_HEADS * MLA_V), beta * MLA_KV_RANK ** -0.5),
        'rel_bias': nrm(ks[10], (REL_BUCKETS, DSA_HEADS), 0.2),
        'w_up_a': nrm(ks[11], (L, MLA_HEADS * MLA_V, D), beta * (MLA_HEADS * MLA_V) ** -0.5),
        'w_up_b': nrm(ks[12], (L, DSA_HEADS * DSA_HEAD_DIM, D), beta * (DSA_HEADS * DSA_HEAD_DIM) ** -0.5),
        'w_gate': nrm(ks[13], (L, D, 2 * D), D ** -0.5),
        'b_gate': nrm(ks[14], (L, 2 * D), 0.02),
        'w_o': nrm(ks[15], (L, D, D), beta * D ** -0.5),
        'ln1_g': gain(ks[16], (L, D)),
        'ln1_b': nrm(ks[17], (L, D), 0.02),
        'w_grp': nrm(ks[18], (L, D, N_GROUPS), D ** -0.5),
        'b_grp': nrm(ks[19], (L, N_GROUPS), 0.01),
        'w_rt': nrm(ks[20], (L, D, N_EXPERTS), D ** -0.5),
        'b_rt': nrm(ks[21], (L, N_EXPERTS), 0.01),
        'w_exp_gate': nrm(ks[22], (L, N_EXPERTS, D, EXPERT_HIDDEN), D ** -0.5),
        'w_exp_up': nrm(ks[23], (L, N_EXPERTS, D, EXPERT_HIDDEN), D ** -0.5),
        'w_exp_down': nrm(ks[24], (L, N_EXPERTS, EXPERT_HIDDEN, D), beta * EXPERT_HIDDEN ** -0.5),
        'ln2_g': gain(ks[25], (L, D)),
        'ln2_b': nrm(ks[26], (L, D), 0.02),
    }


def reference(x, positions, ln0_g, ln0_b, w_in, q_norm_g, w_uq, kv_norm_g, w_uk, w_uv, rel_bias,
              w_up_a, w_up_b, w_gate, b_gate, w_o, ln1_g, ln1_b, w_grp, b_grp, w_rt, b_rt,
              w_exp_gate, w_exp_up, w_exp_down, ln2_g, ln2_b):
    split_at = [int(o) for o in np.cumsum(IN_SIZES)[:-1]]
    x = layer_norm(x, ln0_g, ln0_b)
    for l in range(DEPTH):
        proj = x @ w_in[l]
        c_q, c_kv, k_rope, q_b, k_b, v_b, q_idx, k_idx, w_idx = jnp.split(proj, split_at, axis=-1)
        o_a = mla_branch(c_q, c_kv, k_rope, positions, q_norm_g[l], w_uq[l], kv_norm_g[l], w_uk[l], w_uv[l])
        o_b = dsa_branch(q_b, k_b, v_b, q_idx, k_idx, w_idx, positions, rel_bias)
        gates = jax.nn.sigmoid(x @ w_gate[l] + b_gate[l])
        g_a, g_b = jnp.split(gates, 2, axis=-1)
        mixed = (g_a * (o_a @ w_up_a[l]) + g_b * (o_b @ w_up_b[l])) @ w_o[l]
        h = layer_norm(DEEPNORM_ALPHA * x + mixed, ln1_g[l], ln1_b[l])
        ffn = hierarchical_moe(h, w_grp[l], b_grp[l], w_rt[l], b_rt[l], w_exp_gate[l], w_exp_up[l], w_exp_down[l])
        x = layer_norm(DEEPNORM_ALPHA * h + ffn, ln2_g[l], ln2_b[l])
    return x
```

```python
import functools
import math

import numpy as np
import jax
import jax.numpy as jnp
from jax import lax
from jax.experimental import pallas as pl
from jax.experimental.pallas import tpu as pltpu

F32 = jnp.float32
BF16 = jnp.bfloat16
I32 = jnp.int32

CHUNK = 64
MLA_HEADS = 8
MLA_Q_RANK = 256
MLA_KV_RANK = 128
MLA_NOPE = 64
MLA_ROPE = 32
MLA_V = 64
ROPE_THETA = 10000.0
DSA_HEADS = 8
DSA_HEAD_DIM = 64
IDX_HEADS = 8
IDX_DIM = 32
IDX_TOPK_MAX = 256
REL_BUCKETS = 32
REL_MAX_DIST = 128
N_GROUPS = 4
EXPERTS_PER_GROUP = 8
N_EXPERTS = N_GROUPS * EXPERTS_PER_GROUP
EXPERT_HIDDEN = 256
LN_EPS = 1e-5
RMS_EPS = 1e-6
NEG = -1e30

LANES = 128
VMEM_LIMIT = 48 * 1024 * 1024

_KEY_NEG = int(np.array(NEG, np.float32).view(np.int32)) ^ 0x7FFFFFFF
_INT_MIN = -(2 ** 31)
_CHUNK_SHIFT = CHUNK.bit_length() - 1
assert 1 << _CHUNK_SHIFT == CHUNK


def _ln(x, g, b):
    mu = jnp.mean(x, axis=-1, keepdims=True)
    xc = x - mu
    var = jnp.mean(xc * xc, axis=-1, keepdims=True)
    return xc * lax.rsqrt(var + LN_EPS) * g + b


def _rms(x, g):
    return x * lax.rsqrt(jnp.mean(x * x, axis=-1, keepdims=True) + RMS_EPS) * g


def _dot(a, b):
    return jnp.dot(a, b, preferred_element_type=F32)


def _dot_nt(a, b):
    return lax.dot_general(a, b, (((1,), (1,)), ((), ())), preferred_element_type=F32)


def _bias_kernel(rb_ref, o_ref):
    nb = REL_BUCKETS // 2
    max_exact = nb // 2
    t = lax.broadcasted_iota(I32, (LANES, LANES), 0)
    s = lax.broadcasted_iota(I32, (LANES, LANES), 1)
    o_ref[0] = jnp.zeros(o_ref.shape[1:], F32)
    o_ref[3] = jnp.zeros(o_ref.shape[1:], F32)
    for which, off in ((1, -LANES), (2, 0)):
        rel = s - t + off
        ret = jnp.where(rel > 0, nb, 0)
        n = jnp.abs(rel)
        large = max_exact + (jnp.log(jnp.maximum(n, 1).astype(F32) / max_exact)
                             / math.log(REL_MAX_DIST / max_exact) * (nb - max_exact)).astype(I32)
        large = jnp.minimum(large, nb - 1)
        bucket = ret + jnp.where(n < max_exact, n, large)
        for h in range(DSA_HEADS):
            acc = jnp.zeros((LANES, LANES), F32)
            for b in range(REL_BUCKETS):
                acc = jnp.where(bucket == b, rb_ref[b, h], acc)
            o_ref[which, h] = acc - rb_ref[nb - 1, h]


def _bias_tiles(rel_bias):
    return pl.pallas_call(
        _bias_kernel,
        out_shape=jax.ShapeDtypeStruct((4, DSA_HEADS, LANES, LANES), F32),
        in_specs=[pl.BlockSpec(memory_space=pltpu.SMEM)],
        out_specs=pl.BlockSpec(memory_space=pltpu.VMEM),
        name="bias_tiles",
    )(rel_bias)


_C_Q, _C_KV, _C_RA, _C_RB, _C_QB, _C_KK, _C_VV, _C_QI, _C_KI, _C_END = (
    0, 256, 384, 512, 640, 1152, 1280, 1408, 1664, 1792)


def _in_proj_kernel(x_ref, pos_ref, g0_ref, b0_ref, w1_ref, qg_ref, wq_ref, wqr_ref, kg_ref, wk_ref,
                    wv_ref, invf_ref, sgn_ref,
                    qm_ref, km_ref, vm_ref, qb_ref, kk_ref, vv_ref, qi_ref, ki_ref, wi_ref):
    xn = _ln(x_ref[...], g0_ref[...], b0_ref[...])
    proj = _dot(xn.astype(BF16), w1_ref[...])
    qb_ref[...] = proj[:, _C_QB:_C_KK].astype(BF16)
    kk_ref[...] = proj[:, _C_KK:_C_VV].astype(BF16)
    vv_ref[...] = proj[:, _C_VV:_C_QI].astype(BF16)
    qi_ref[...] = proj[:, _C_QI:_C_KI].astype(BF16)
    ki_ref[...] = proj[:, _C_KI:_C_END].astype(BF16)
    ga = proj[:, _C_RA:_C_RB]
    gb = proj[:, _C_RB:_C_QB]
    wi_ref[...] = ga[:, 0:IDX_HEADS] * (IDX_HEADS ** -0.5 * IDX_DIM ** -0.5)

    ang = pos_ref[...].astype(F32) * invf_ref[...]
    cosv = jnp.cos(ang)
    sinv = jnp.sin(ang) * sgn_ref[...]
    lane = lax.broadcasted_iota(I32, ang.shape, 1)
    rope_lane = (lane >= MLA_NOPE) & (lane < MLA_NOPE + MLA_ROPE)
    kr = jnp.where(rope_lane, ga * cosv + gb * sinv, 0.0)

    scale = (MLA_NOPE + MLA_ROPE) ** -0.5
    cos8 = jnp.concatenate([cosv * scale] * MLA_HEADS, axis=1)
    sin8 = jnp.concatenate([sinv * scale] * MLA_HEADS, axis=1)
    cqn = _rms(proj[:, _C_Q:_C_KV], qg_ref[...]).astype(BF16)
    q = _dot(cqn, wq_ref[...]) * cos8 + _dot(cqn, wqr_ref[...]) * sin8
    qm_ref[...] = q.astype(BF16)

    ckn = _rms(proj[:, _C_KV:_C_RA], kg_ref[...]).astype(BF16)
    k = _dot(ckn, wk_ref[...]) + jnp.concatenate([kr] * MLA_HEADS, axis=1)
    km_ref[...] = k.astype(BF16)
    vm_ref[...] = _dot(ckn, wv_ref[...]).astype(BF16)


def _in_proj(x2, pos2, g0, b0, w1, qg, wq, wqr, kg, wk, wv, invf, sgn, tm):
    n, d = x2.shape
    hm = MLA_HEADS * LANES
    row = lambda w: pl.BlockSpec((tm, w), lambda i: (i, 0))
    full = lambda a: pl.BlockSpec(a.shape, lambda i: (0,) * a.ndim)
    outs = [(hm, BF16), (hm, BF16), (MLA_HEADS * MLA_V, BF16), (DSA_HEADS * DSA_HEAD_DIM, BF16),
            (LANES, BF16), (LANES, BF16), (IDX_HEADS * IDX_DIM, BF16), (LANES, BF16), (IDX_HEADS, F32)]
    return pl.pallas_call(
        _in_proj_kernel,
        grid=(n // tm,),
        in_specs=[row(d), row(1)] + [full(a) for a in (g0, b0, w1, qg, wq, wqr, kg, wk, wv, invf, sgn)],
        out_specs=[row(w) for w, _ in outs],
        out_shape=[jax.ShapeDtypeStruct((n, w), dt) for w, dt in outs],
        compiler_params=pltpu.CompilerParams(dimension_semantics=("parallel",),
                                             vmem_limit_bytes=VMEM_LIMIT),
        name="in_proj",
    )(x2, pos2, g0, b0, w1, qg, wq, wqr, kg, wk, wv, invf, sgn)


def _admissible(tq, ck, q0, k0):
    t = q0 + lax.broadcasted_iota(I32, (tq, ck), 0)
    s = k0 + lax.broadcasted_iota(I32, (tq, ck), 1)
    return (s >> _CHUNK_SHIFT) <= (t >> _CHUNK_SHIFT)


def _mla_kernel(q_ref, k_ref, v_ref, o_ref, m_ref, l_ref, acc_ref, *, tq, ck):
    j = pl.program_id(1)
    q0 = j * tq
    m_ref[...] = jnp.full(m_ref.shape, NEG, F32)
    l_ref[...] = jnp.zeros(l_ref.shape, F32)
    acc_ref[...] = jnp.zeros(acc_ref.shape, F32)
    nfull = (q0 + CHUNK) // ck

    def chunk(c, masked):
        k0 = pl.multiple_of(c * ck, ck)
        adm = _admissible(tq, ck, q0, k0) if masked else None
        for h in range(MLA_HEADS):
            qh = q_ref[:, h * LANES:(h + 1) * LANES]
            kc = k_ref[pl.ds(k0, ck), h * LANES:(h + 1) * LANES]
            s = _dot_nt(qh, kc)
            if masked:
                s = jnp.where(adm, s, NEG)
            m_prev = m_ref[h]
            m_new = jnp.maximum(m_prev, jnp.max(s, axis=1, keepdims=True))
            alpha = jnp.exp(m_prev - m_new)
            p = jnp.exp(s - m_new)
            l_ref[h] = alpha * l_ref[h] + jnp.sum(p, axis=1, keepdims=True)
            vc = v_ref[pl.ds(k0, ck), (h // 2) * LANES:(h // 2 + 1) * LANES]
            acc_ref[h] = alpha * acc_ref[h] + _dot(p.astype(BF16), vc)
            m_ref[h] = m_new

    def full_body(c, carry):
        chunk(c, False)
        return carry

    lax.fori_loop(0, nfull, full_body, 0)
    chunk(nfull, True)

    lane = lax.broadcasted_iota(I32, (tq, LANES), 1)
    for hp in range(MLA_HEADS // 2):
        o = jnp.where(lane < MLA_V, acc_ref[2 * hp] / l_ref[2 * hp], acc_ref[2 * hp + 1] / l_ref[2 * hp + 1])
        o_ref[:, hp * LANES:(hp + 1) * LANES] = o.astype(BF16)


def _mla_attn(qm, km, vm, tq, ck):
    b, t, hm = qm.shape
    assert ck % tq == 0 and tq > CHUNK and tq % CHUNK == 0 and t % ck == 0
    return pl.pallas_call(
        functools.partial(_mla_kernel, tq=tq, ck=ck),
        grid=(b, t // tq),
        in_specs=[pl.BlockSpec((None, tq, hm), lambda i, j: (i, j, 0)),
                  pl.BlockSpec((None, t, hm), lambda i, j: (i, 0, 0)),
                  pl.BlockSpec((None, t, vm.shape[2]), lambda i, j: (i, 0, 0))],
        out_specs=pl.BlockSpec((None, tq, vm.shape[2]), lambda i, j: (i, j, 0)),
        out_shape=jax.ShapeDtypeStruct((b, t, vm.shape[2]), BF16),
        scratch_shapes=[pltpu.VMEM((MLA_HEADS, tq, 1), F32), pltpu.VMEM((MLA_HEADS, tq, 1), F32),
                        pltpu.VMEM((MLA_HEADS, tq, LANES), F32)],
        compiler_params=pltpu.CompilerParams(dimension_semantics=("parallel", "arbitrary"),
                                             vmem_limit_bytes=VMEM_LIMIT),
        name="mla_attn",
    )(qm, km, vm)


def _dsa_kernel(qb_ref, qi_ref, wi_ref, kk_ref, vv_ref, ki_ref, bt_ref, o_ref,
                qs_ref, qim_ref, keys_ref, thr_ref, cut_ref, m_ref, l_ref, acc_ref, *, tq, ck, topk):
    nh = DSA_HEADS
    seq = kk_ref.shape[0]
    j = pl.program_id(1)
    q0 = j * tq
    nck = (q0 + tq + ck - 1) // ck
    lane = lax.broadcasted_iota(I32, (tq, LANES), 1)

    for h in range(nh):
        pair = qb_ref[:, (h // 2) * LANES:(h // 2 + 1) * LANES].astype(F32)
        in_head = (lane >= (h % 2) * DSA_HEAD_DIM) & (lane < (h % 2 + 1) * DSA_HEAD_DIM)
        qs_ref[h * tq:(h + 1) * tq, :] = jnp.where(in_head, pair, 0.0).astype(BF16)
        grp = qi_ref[:, (h // 4) * LANES:(h // 4 + 1) * LANES].astype(F32)
        in_head = (lane >= (h % 4) * IDX_DIM) & (lane < (h % 4 + 1) * IDX_DIM)
        qim_ref[h] = jnp.where(in_head, grp, 0.0).astype(BF16)

    w = wi_ref[...]

    def idx_chunk(c, masked):
        k0 = pl.multiple_of(c * ck, ck)
        kc = ki_ref[pl.ds(k0, ck), :]
        score = jnp.zeros((tq, ck), F32)
        for h in range(nh):
            score = score + w[:, h:h + 1] * jnp.maximum(_dot_nt(qim_ref[h], kc), 0.0)
        if masked:
            score = jnp.where(_admissible(tq, ck, q0, k0), score, NEG)
        score = jnp.where(score == 0.0, 0.0, score)
        bits = lax.bitcast_convert_type(score, I32)
        keys_ref[c] = bits ^ ((bits >> 31) & 0x7FFFFFFF)

    def idx_body(c, carry):
        idx_chunk(c, False)
        return carry

    lax.fori_loop(0, nck - 1, idx_body, 0)
    idx_chunk(nck - 1, True)

    thr_ref[...] = jnp.full(thr_ref.shape, _KEY_NEG + 1, I32)
    cut_ref[...] = jnp.full(cut_ref.shape, seq, I32)

    def count_rows(pred):
        def body(c, acc):
            hit = pred(c)
            for sub in range(ck // LANES):
                acc = acc + jnp.where(hit[:, sub * LANES:(sub + 1) * LANES], 1.0, 0.0)
            return acc
        acc = lax.fori_loop(0, nck, body, jnp.zeros((tq, LANES), F32))
        return jnp.sum(acc, axis=1, keepdims=True)

    @pl.when(q0 + tq > topk)
    def _search():
        def step(i, carry):
            thr, cge = carry
            cand = thr + lax.shift_left(jnp.int32(1), 31 - i)
            cnt = count_rows(lambda c: keys_ref[c] >= cand)
            ok = cnt >= topk
            return jnp.where(ok, cand, thr), jnp.where(ok, cnt, cge)

        thr0 = jnp.full((tq, 1), _INT_MIN, I32)
        cge0 = jnp.full((tq, 1), 1.0, F32) * (nck * ck).astype(F32)
        thr, cge = lax.fori_loop(0, 32, step, (thr0, cge0))
        thr_ref[...] = thr
        excess = cge - float(topk)

        @pl.when(jnp.max(excess) > 0.0)
        def _ties():
            nbits = max(1, int(seq - 1).bit_length())

            def tstep(i, cut):
                cand = cut + lax.shift_left(jnp.int32(1), nbits - 1 - i)

                def pred(c):
                    idx = c * ck + lax.broadcasted_iota(I32, (tq, ck), 1)
                    return (keys_ref[c] == thr) & (idx >= cand)

                return jnp.where(count_rows(pred) >= excess, cand, cut)

            cut = lax.fori_loop(0, nbits, tstep, jnp.zeros((tq, 1), I32))
            cut_ref[...] = jnp.where(excess > 0.0, cut, seq)

    m_ref[...] = jnp.full(m_ref.shape, NEG, F32)
    l_ref[...] = jnp.zeros(l_ref.shape, F32)
    acc_ref[...] = jnp.zeros(acc_ref.shape, F32)
    thr = thr_ref[...]
    cut = cut_ref[...]

    def att_chunk(c, near):
        k0 = pl.multiple_of(c * ck, ck)
        key = keys_ref[c]
        idx = k0 + lax.broadcasted_iota(I32, (tq, ck), 1)
        sel = (key > thr) | ((key == thr) & (idx < cut))
        s = _dot_nt(qs_ref[...], kk_ref[pl.ds(k0, ck), :]).reshape(nh, tq, ck)
        if near:
            sel = sel & _admissible(tq, ck, q0, k0)
            d0 = c * (ck // LANES) - j
            s = s + jnp.concatenate([bt_ref[d0 + 2 + i] for i in range(ck // LANES)], axis=2)
        s = jnp.where(sel[None], s, NEG)
        m_prev = m_ref[...]
        m_new = jnp.maximum(m_prev, jnp.max(s, axis=2, keepdims=True))
        alpha = jnp.exp(m_prev - m_new)
        p = jnp.exp(s - m_new)
        l_ref[...] = alpha * l_ref[...] + jnp.sum(p, axis=2, keepdims=True)
        pv = _dot(p.reshape(nh * tq, ck).astype(BF16), vv_ref[pl.ds(k0, ck), :])
        acc_ref[...] = alpha * acc_ref[...] + pv.reshape(nh, tq, LANES)
        m_ref[...] = m_new

    near_start = jnp.maximum((j - 1) // (ck // LANES), 0)

    def far_body(c, carry):
        att_chunk(c, False)
        return carry

    def near_body(c, carry):
        att_chunk(c, True)
        return carry

    lax.fori_loop(0, near_start, far_body, 0)
    lax.fori_loop(near_start, nck, near_body, 0)

    for hp in range(nh // 2):
        o = jnp.where(lane < DSA_HEAD_DIM, acc_ref[2 * hp] / l_ref[2 * hp],
                      acc_ref[2 * hp + 1] / l_ref[2 * hp + 1])
        o_ref[:, hp * LANES:(hp + 1) * LANES] = o.astype(BF16)


def _dsa_attn(qb, qi, wi, kk, vv, ki, bt, tq, ck, topk):
    b, t, _ = qb.shape
    assert tq == LANES and ck == 2 * LANES and t % ck == 0
    nh = DSA_HEADS
    blk = lambda a: pl.BlockSpec((None, tq, a.shape[2]), lambda i, j: (i, j, 0))
    seqb = lambda a: pl.BlockSpec((None, t, a.shape[2]), lambda i, j: (i, 0, 0))
    return pl.pallas_call(
        functools.partial(_dsa_kernel, tq=tq, ck=ck, topk=topk),
        grid=(b, t // tq),
        in_specs=[blk(qb), blk(qi), blk(wi), seqb(kk), seqb(vv), seqb(ki),
                  pl.BlockSpec(bt.shape, lambda i, j: (0, 0, 0, 0))],
        out_specs=blk(qb),
        out_shape=jax.ShapeDtypeStruct(qb.shape, BF16),
        scratch_shapes=[pltpu.VMEM((nh * tq, LANES), BF16), pltpu.VMEM((nh, tq, LANES), BF16),
                        pltpu.VMEM((t // ck, tq, ck), I32), pltpu.VMEM((tq, 1), I32),
                        pltpu.VMEM((tq, 1), I32), pltpu.VMEM((nh, tq, 1), F32),
                        pltpu.VMEM((nh, tq, 1), F32), pltpu.VMEM((nh, tq, LANES), F32)],
        compiler_params=pltpu.CompilerParams(dimension_semantics=("parallel", "arbitrary"),
                                             vmem_limit_bytes=VMEM_LIMIT),
        name="dsa_attn",
    )(qb, qi, wi, kk, vv, ki, bt)


_R_GRP = 0
_R_EXP = N_GROUPS


def _post_kernel(x_ref, oa_ref, ob_ref, g0_ref, b0_ref, wg_ref, bg_ref, wua_ref, wub_ref, wo_ref,
                 g1_ref, b1_ref, wr_ref, br_ref, h_ref, comb_ref, *, alpha):
    d = x_ref.shape[1]
    xn = _ln(x_ref[...], g0_ref[...], b0_ref[...])
    z = _dot(xn.astype(BF16), wg_ref[...]) + bg_ref[...]
    gates = 1.0 / (1.0 + jnp.exp(-z))
    mix = gates[:, :d] * _dot(oa_ref[...], wua_ref[...]) + gates[:, d:] * _dot(ob_ref[...], wub_ref[...])
    mixed = _dot(mix.astype(BF16), wo_ref[...])
    h = _ln(alpha * xn + mixed, g1_ref[...], b1_ref[...])
    h_ref[...] = h

    logits = _dot(h.astype(BF16), wr_ref[...]) + br_ref[...]
    lane = lax.broadcasted_iota(I32, logits.shape, 1).astype(F32)
    gmask = lane < N_GROUPS
    gl = jnp.where(gmask, logits, -jnp.inf)
    gmax = jnp.max(gl, axis=1, keepdims=True)
    g_sel = jnp.min(jnp.where(gl == gmax, lane, float(LANES)), axis=1, keepdims=True)
    g_w = 1.0 / jnp.sum(jnp.where(gmask, jnp.exp(gl - gmax), 0.0), axis=1, keepdims=True)
    e_lo = _R_EXP + g_sel * EXPERTS_PER_GROUP
    el = jnp.where((lane >= e_lo) & (lane < e_lo + EXPERTS_PER_GROUP), logits, -jnp.inf)
    m1 = jnp.max(el, axis=1, keepdims=True)
    i1 = jnp.min(jnp.where(el == m1, lane, float(LANES)), axis=1, keepdims=True)
    el2 = jnp.where(lane == i1, -jnp.inf, el)
    m2 = jnp.max(el2, axis=1, keepdims=True)
    i2 = jnp.min(jnp.where(el2 == m2, lane, float(LANES)), axis=1, keepdims=True)
    e21 = jnp.exp(m2 - m1)
    w1 = g_w / (1.0 + e21)
    w2 = g_w * e21 / (1.0 + e21)
    comb_ref[...] = jnp.where(lane == i1, w1, 0.0) + jnp.where(lane == i2, w2, 0.0)


def _post_attn(x2, oa, ob, g0, b0, wg, bg, wua, wub, wo, g1, b1, wr, br, alpha, tm):
    n, d = x2.shape
    row = lambda w: pl.BlockSpec((tm, w), lambda i: (i, 0))
    full = lambda a: pl.BlockSpec(a.shape, lambda i: (0,) * a.ndim)
    return pl.pallas_call(
        functools.partial(_post_kernel, alpha=alpha),
        grid=(n // tm,),
        in_specs=[row(d), row(oa.shape[1]), row(ob.shape[1])]
                 + [full(a) for a in (g0, b0, wg, bg, wua, wub, wo, g1, b1, wr, br)],
        out_specs=[row(d), row(LANES)],
        out_shape=[jax.ShapeDtypeStruct((n, d), F32), jax.ShapeDtypeStruct((n, LANES), F32)],
        compiler_params=pltpu.CompilerParams(dimension_semantics=("parallel",),
                                             vmem_limit_bytes=VMEM_LIMIT),
        name="post_attn",
    )(x2, oa, ob, g0, b0, wg, bg, wua, wub, wo, g1, b1, wr, br)


def _moe_kernel(h_ref, comb_ref, weg_ref, weu_ref, wed_ref, g2_ref, b2_ref, o_ref, acc_ref, *, alpha):
    e = pl.program_id(1)

    @pl.when(e == 0)
    def _init():
        acc_ref[...] = jnp.zeros(acc_ref.shape, F32)

    hb = h_ref[...].astype(BF16)
    a = _dot(hb, weg_ref[...])
    u = _dot(hb, weu_ref[...])
    comb = comb_ref[...]
    lane = lax.broadcasted_iota(I32, comb.shape, 1)
    cw = jnp.sum(jnp.where(lane == _R_EXP + e, comb, 0.0), axis=1, keepdims=True)
    hid = (a / (1.0 + jnp.exp(-a))) * u * cw
    acc_ref[...] += _dot(hid.astype(BF16), wed_ref[...])

    @pl.when(e == pl.num_programs(1) - 1)
    def _fin():
        o_ref[...] = _ln(alpha * h_ref[...] + acc_ref[...], g2_ref[...], b2_ref[...])


def _moe(h, comb, weg, weu, wed, g2, b2, alpha, tm):
    n, d = h.shape
    ne, _, f = weg.shape
    row = lambda w: pl.BlockSpec((tm, w), lambda i, e: (i, 0))
    return pl.pallas_call(
        functools.partial(_moe_kernel, alpha=alpha),
        grid=(n // tm, ne),
        in_specs=[row(d), row(LANES),
                  pl.BlockSpec((None, d, f), lambda i, e: (e, 0, 0)),
                  pl.BlockSpec((None, d, f), lambda i, e: (e, 0, 0)),
                  pl.BlockSpec((None, f, d), lambda i, e: (e, 0, 0)),
                  pl.BlockSpec(g2.shape, lambda i, e: (0, 0)),
                  pl.BlockSpec(b2.shape, lambda i, e: (0, 0))],
        out_specs=row(d),
        out_shape=jax.ShapeDtypeStruct((n, d), F32),
        scratch_shapes=[pltpu.VMEM((tm, d), F32)],
        compiler_params=pltpu.CompilerParams(dimension_semantics=("parallel", "arbitrary"),
                                             vmem_limit_bytes=VMEM_LIMIT),
        name="moe",
    )(h, comb, weg, weu, wed, g2, b2)


def _layout_weights(w_in, w_uq, w_uk, w_uv):
    d = w_in.shape[0]
    sizes = (MLA_Q_RANK, MLA_KV_RANK, MLA_ROPE, DSA_HEADS * DSA_HEAD_DIM, DSA_HEAD_DIM, DSA_HEAD_DIM,
             IDX_HEADS * IDX_DIM, IDX_DIM, IDX_HEADS)
    offs = np.cumsum((0,) + sizes)
    c_q, c_kv, k_rope, q_b, k_b, v_b, q_idx, k_idx, w_idx = (
        w_in[:, int(offs[i]):int(offs[i + 1])] for i in range(len(sizes)))
    half = MLA_ROPE // 2
    z = lambda w: jnp.zeros((d, w), w_in.dtype)
    pad = LANES - MLA_NOPE - MLA_ROPE
    grp_a = jnp.concatenate([w_idx, z(MLA_NOPE - IDX_HEADS), k_rope, z(pad)], axis=1)
    grp_b = jnp.concatenate([z(MLA_NOPE), k_rope[:, half:], k_rope[:, :half], z(pad)], axis=1)
    w1 = jnp.concatenate([c_q, c_kv, grp_a, grp_b, q_b * DSA_HEAD_DIM ** -0.5, k_b, k_b, v_b, v_b, q_idx,
                          k_idx, k_idx, k_idx, k_idx], axis=1)
    assert w1.shape[1] == _C_END

    dk = MLA_NOPE + MLA_ROPE
    rq = w_uq.shape[0]
    zq = lambda w: jnp.zeros((rq, w), w_uq.dtype)
    wq_parts, wqr_parts, wk_parts = [], [], []
    for h in range(MLA_HEADS):
        nope = w_uq[:, h * dk:h * dk + MLA_NOPE]
        rope = w_uq[:, h * dk + MLA_NOPE:(h + 1) * dk]
        wq_parts += [nope, rope, zq(pad)]
        wqr_parts += [zq(MLA_NOPE), rope[:, half:], rope[:, :half], zq(pad)]
        wk_parts += [w_uk[:, h * MLA_NOPE:(h + 1) * MLA_NOPE],
                     jnp.zeros((w_uk.shape[0], LANES - MLA_NOPE), w_uk.dtype)]
    wq = jnp.concatenate(wq_parts, axis=1)
    wqr = jnp.concatenate(wqr_parts, axis=1)
    wk = jnp.concatenate(wk_parts, axis=1)

    inv = ROPE_THETA ** (-np.arange(half, dtype=np.float32) / half)
    invf = np.zeros((1, LANES), np.float32)
    invf[0, MLA_NOPE:MLA_NOPE + half] = inv
    invf[0, MLA_NOPE + half:MLA_NOPE + MLA_ROPE] = inv
    sgn = np.zeros((1, LANES), np.float32)
    sgn[0, MLA_NOPE:MLA_NOPE + half] = -1.0
    sgn[0, MLA_NOPE + half:MLA_NOPE + MLA_ROPE] = 1.0
    return (w1.astype(BF16), wq.astype(BF16), wqr.astype(BF16), wk.astype(BF16), w_uv.astype(BF16),
            jnp.asarray(invf), jnp.asarray(sgn))


def kernel(x, positions, ln0_g, ln0_b, w_in, q_norm_g, w_uq, kv_norm_g, w_uk, w_uv, rel_bias, w_up_a, w_up_b,
           w_gate, b_gate, w_o, ln1_g, ln1_b, w_grp, b_grp, w_rt, b_rt, w_exp_gate, w_exp_up, w_exp_down,
           ln2_g, ln2_b):
    bsz, seq, d = x.shape
    depth = w_in.shape[0]
    assert depth == 1
    alpha = (2.0 * depth) ** 0.25
    n = bsz * seq
    topk = min(IDX_TOPK_MAX, seq // 4)
    r2 = lambda v: v.reshape(1, -1)

    x2 = x.reshape(n, d)
    pos2 = positions.reshape(n, 1)
    w1, wq, wqr, wk, wv, invf, sgn = _layout_weights(w_in[0], w_uq[0], w_uk[0], w_uv[0])

    bt = _bias_tiles(rel_bias)
    qm, km, vm, qb, kk, vv, qi, ki, wi = _in_proj(
        x2, pos2, r2(ln0_g), r2(ln0_b), w1, r2(q_norm_g[0]), wq, wqr, r2(kv_norm_g[0]), wk, wv, invf, sgn,
        tm=512)
    b3 = lambda a: a.reshape(bsz, seq, a.shape[1])
    o_a = _mla_attn(b3(qm), b3(km), b3(vm), tq=128, ck=256)
    o_b = _dsa_attn(b3(qb), b3(qi), b3(wi), b3(kk), b3(vv), b3(ki), bt, tq=128, ck=256, topk=topk)

    wr = jnp.concatenate([w_grp[0], w_rt[0], jnp.zeros((d, LANES - N_GROUPS - N_EXPERTS), F32)], axis=1)
    br = jnp.concatenate([b_grp[0], b_rt[0], jnp.zeros((LANES - N_GROUPS - N_EXPERTS,), F32)])
    h, comb = _post_attn(
        x2, o_a.reshape(n, -1), o_b.reshape(n, -1), r2(ln0_g), r2(ln0_b), w_gate[0].astype(BF16), r2(b_gate[0]),
        w_up_a[0].astype(BF16), w_up_b[0].astype(BF16), w_o[0].astype(BF16), r2(ln1_g[0]), r2(ln1_b[0]),
        wr.astype(BF16), r2(br), alpha, tm=256)
    out = _moe(h, comb, w_exp_gate[0].astype(BF16), w_exp_up[0].astype(BF16), w_exp_down[0].astype(BF16),
               r2(ln2_g[0]), r2(ln2_b[0]), alpha, tm=1024)
    return out.reshape(bsz, seq, d)
```

```python
import functools
import math

import numpy as np
import jax
import jax.numpy as jnp
from jax import lax
from jax.experimental import pallas as pl
from jax.experimental.pallas import tpu as pltpu

F32 = jnp.float32
BF16 = jnp.bfloat16
I32 = jnp.int32

CHUNK = 64
MLA_HEADS = 8
MLA_Q_RANK = 256
MLA_KV_RANK = 128
MLA_NOPE = 64
MLA_ROPE = 32
MLA_V = 64
ROPE_THETA = 10000.0
DSA_HEADS = 8
DSA_HEAD_DIM = 64
IDX_HEADS = 8
IDX_DIM = 32
IDX_TOPK_MAX = 256
REL_BUCKETS = 32
REL_MAX_DIST = 128
N_GROUPS = 4
EXPERTS_PER_GROUP = 8
N_EXPERTS = N_GROUPS * EXPERTS_PER_GROUP
EXPERT_HIDDEN = 256
LN_EPS = 1e-5
RMS_EPS = 1e-6
NEG = -1e30

LANES = 128
VMEM_LIMIT = 48 * 1024 * 1024

_KEY_NEG = int(np.array(NEG, np.float32).view(np.int32)) ^ 0x7FFFFFFF
_INT_MIN = -(2 ** 31)
_CHUNK_SHIFT = CHUNK.bit_length() - 1
assert 1 << _CHUNK_SHIFT == CHUNK


def _ln(x, g, b):
    mu = jnp.mean(x, axis=-1, keepdims=True)
    xc = x - mu
    var = jnp.mean(xc * xc, axis=-1, keepdims=True)
    return xc * lax.rsqrt(var + LN_EPS) * g + b


def _rms(x, g):
    return x * lax.rsqrt(jnp.mean(x * x, axis=-1, keepdims=True) + RMS_EPS) * g


def _dot(a, b):
    return jnp.dot(a, b, preferred_element_type=F32)


def _dot_nt(a, b):
    return lax.dot_general(a, b, (((1,), (1,)), ((), ())), preferred_element_type=F32)


def _bias_kernel(rb_ref, o_ref):
    nb = REL_BUCKETS // 2
    max_exact = nb // 2
    blk = o_ref.shape[2]
    s = lax.broadcasted_iota(I32, (blk, blk), 0)
    t = lax.broadcasted_iota(I32, (blk, blk), 1)
    for which, off in ((0, -blk), (1, 0)):
        rel = s - t + off
        ret = jnp.where(rel > 0, nb, 0)
        n = jnp.abs(rel)
        large = max_exact + (jnp.log(jnp.maximum(n, 1).astype(F32) / max_exact)
                             / math.log(REL_MAX_DIST / max_exact) * (nb - max_exact)).astype(I32)
        large = jnp.minimum(large, nb - 1)
        bucket = ret + jnp.where(n < max_exact, n, large)
        for h in range(DSA_HEADS):
            acc = jnp.zeros((blk, blk), F32)
            for b in range(REL_BUCKETS):
                acc = jnp.where(bucket == b, rb_ref[b, h], acc)
            o_ref[which, h] = acc - rb_ref[nb - 1, h]


def _bias_tiles(rel_bias, blk):
    assert blk >= REL_MAX_DIST
    return pl.pallas_call(
        _bias_kernel,
        out_shape=jax.ShapeDtypeStruct((2, DSA_HEADS, blk, blk), F32),
        in_specs=[pl.BlockSpec(memory_space=pltpu.SMEM)],
        out_specs=pl.BlockSpec(memory_space=pltpu.VMEM),
        name="bias_tiles",
    )(rel_bias)


_C_Q, _C_KV, _C_RA, _C_RB, _C_QB, _C_KK, _C_VV, _C_QI, _C_KI, _C_END = (
    0, 256, 384, 512, 640, 1152, 1280, 1408, 1664, 1792)


def _in_proj_kernel(x_ref, pos_ref, g0_ref, b0_ref, w1_ref, qg_ref, wq_ref, wqr_ref, kg_ref, wk_ref,
                    wv_ref, invf_ref, sgn_ref,
                    qm_ref, km_ref, vm_ref, qb_ref, kk_ref, vv_ref, qi_ref, ki_ref, wi_ref):
    xn = _ln(x_ref[...], g0_ref[...], b0_ref[...])
    proj = _dot(xn.astype(BF16), w1_ref[...])
    qb_ref[...] = proj[:, _C_QB:_C_KK].astype(BF16)
    kk_ref[...] = proj[:, _C_KK:_C_VV].astype(BF16)
    vv_ref[...] = proj[:, _C_VV:_C_QI].astype(BF16)
    qi_ref[...] = proj[:, _C_QI:_C_KI].astype(BF16)
    ki_ref[...] = proj[:, _C_KI:_C_END].astype(BF16)
    ga = proj[:, _C_RA:_C_RB]
    gb = proj[:, _C_RB:_C_QB]
    wi_ref[...] = ga[:, 0:IDX_HEADS] * (IDX_HEADS ** -0.5 * IDX_DIM ** -0.5)

    ang = pos_ref[...].astype(F32) * invf_ref[...]
    cosv = jnp.cos(ang)
    sinv = jnp.sin(ang) * sgn_ref[...]
    lane = lax.broadcasted_iota(I32, ang.shape, 1)
    rope_lane = (lane >= MLA_NOPE) & (lane < MLA_NOPE + MLA_ROPE)
    kr = jnp.where(rope_lane, ga * cosv + gb * sinv, 0.0)

    scale = (MLA_NOPE + MLA_ROPE) ** -0.5
    cos8 = jnp.concatenate([cosv * scale] * MLA_HEADS, axis=1)
    sin8 = jnp.concatenate([sinv * scale] * MLA_HEADS, axis=1)
    cqn = _rms(proj[:, _C_Q:_C_KV], qg_ref[...]).astype(BF16)
    q = _dot(cqn, wq_ref[...]) * cos8 + _dot(cqn, wqr_ref[...]) * sin8
    qm_ref[...] = q.astype(BF16)

    ckn = _rms(proj[:, _C_KV:_C_RA], kg_ref[...]).astype(BF16)
    k = _dot(ckn, wk_ref[...]) + jnp.concatenate([kr] * MLA_HEADS, axis=1)
    km_ref[...] = k.astype(BF16)
    vm_ref[...] = _dot(ckn, wv_ref[...]).astype(BF16)


def _in_proj(x2, pos2, g0, b0, w1, qg, wq, wqr, kg, wk, wv, invf, sgn, tm):
    n, d = x2.shape
    hm = MLA_HEADS * LANES
    row = lambda w: pl.BlockSpec((tm, w), lambda i: (i, 0))
    full = lambda a: pl.BlockSpec(a.shape, lambda i: (0,) * a.ndim)
    outs = [(hm, BF16), (hm, BF16), (MLA_HEADS * MLA_V, BF16), (DSA_HEADS * DSA_HEAD_DIM, BF16),
            (LANES, BF16), (LANES, BF16), (IDX_HEADS * IDX_DIM, BF16), (LANES, BF16), (IDX_HEADS, F32)]
    return pl.pallas_call(
        _in_proj_kernel,
        grid=(n // tm,),
        in_specs=[row(d), row(1)] + [full(a) for a in (g0, b0, w1, qg, wq, wqr, kg, wk, wv, invf, sgn)],
        out_specs=[row(w) for w, _ in outs],
        out_shape=[jax.ShapeDtypeStruct((n, w), dt) for w, dt in outs],
        compiler_params=pltpu.CompilerParams(dimension_semantics=("parallel",),
                                             vmem_limit_bytes=VMEM_LIMIT),
        name="in_proj",
    )(x2, pos2, g0, b0, w1, qg, wq, wqr, kg, wk, wv, invf, sgn)


def _admissible(ck, tq, k0, q0):
    s = k0 + lax.broadcasted_iota(I32, (ck, tq), 0)
    t = q0 + lax.broadcasted_iota(I32, (ck, tq), 1)
    return (s >> _CHUNK_SHIFT) <= (t >> _CHUNK_SHIFT)


def _softmax_step(s, vt, m_ref, l_ref, acc_ref, h):
    m_prev = m_ref[h]
    m_new = jnp.maximum(m_prev, jnp.max(s, axis=0, keepdims=True))
    alpha = jnp.exp(m_prev - m_new)
    p = jnp.exp(s - m_new)
    l_ref[h] = alpha * l_ref[h] + jnp.sum(p, axis=0, keepdims=True)
    acc_ref[h] = alpha * acc_ref[h] + _dot(vt, p.astype(BF16))
    m_ref[h] = m_new


def _finish(o_ref, l_ref, acc_ref, nh):
    ot = jnp.concatenate([acc_ref[h] / l_ref[h] for h in range(nh)], axis=0)
    o_ref[...] = ot.T.astype(BF16)


def _mla_kernel(q_ref, k_ref, vt_ref, o_ref, s_ref, m_ref, l_ref, acc_ref, *, tq, ck):
    j = pl.program_id(1)
    q0 = j * tq
    m_ref[...] = jnp.full(m_ref.shape, NEG, F32)
    l_ref[...] = jnp.zeros(l_ref.shape, F32)
    acc_ref[...] = jnp.zeros(acc_ref.shape, F32)
    nfull = (q0 + CHUNK) // ck

    def chunk(c, masked):
        k0 = pl.multiple_of(c * ck, ck)
        for h in range(MLA_HEADS):
            s_ref[h] = _dot_nt(k_ref[pl.ds(k0, ck), h * LANES:(h + 1) * LANES],
                               q_ref[:, h * LANES:(h + 1) * LANES])
        adm = _admissible(ck, tq, k0, q0) if masked else None
        for h in range(MLA_HEADS):
            s = s_ref[h]
            if masked:
                s = jnp.where(adm, s, NEG)
            _softmax_step(s, vt_ref[c, h * MLA_V:(h + 1) * MLA_V, :], m_ref, l_ref, acc_ref, h)

    def full_body(c, carry):
        chunk(c, False)
        return carry

    lax.fori_loop(0, nfull, full_body, 0)
    chunk(nfull, True)
    _finish(o_ref, l_ref, acc_ref, MLA_HEADS)


def _mla_attn(qm, km, vmt, tq, ck):
    b, t, hm = qm.shape
    dv = vmt.shape[2]
    assert ck % tq == 0 and tq > CHUNK and tq % CHUNK == 0 and t % ck == 0
    return pl.pallas_call(
        functools.partial(_mla_kernel, tq=tq, ck=ck),
        grid=(b, t // tq),
        in_specs=[pl.BlockSpec((None, tq, hm), lambda i, j: (i, j, 0)),
                  pl.BlockSpec((None, t, hm), lambda i, j: (i, 0, 0)),
                  pl.BlockSpec((None, t // ck, dv, ck), lambda i, j: (i, 0, 0, 0))],
        out_specs=pl.BlockSpec((None, tq, dv), lambda i, j: (i, j, 0)),
        out_shape=jax.ShapeDtypeStruct((b, t, dv), BF16),
        scratch_shapes=[pltpu.VMEM((MLA_HEADS, ck, tq), F32),
                        pltpu.VMEM((MLA_HEADS, 1, tq), F32), pltpu.VMEM((MLA_HEADS, 1, tq), F32),
                        pltpu.VMEM((MLA_HEADS, MLA_V, tq), F32)],
        compiler_params=pltpu.CompilerParams(dimension_semantics=("parallel", "arbitrary"),
                                             vmem_limit_bytes=VMEM_LIMIT),
        name="mla_attn",
    )(qm, km, vmt)


def _dsa_kernel(qb_ref, qi_ref, wt_ref, kk_ref, vt_ref, ki_ref, bt_ref, o_ref,
                qs_ref, qim_ref, keys_ref, s_ref, thr_ref, cut_ref, m_ref, l_ref, acc_ref, *, tq, topk):
    nh = DSA_HEADS
    ck = tq
    seq = kk_ref.shape[0]
    j = pl.program_id(1)
    q0 = j * tq
    nck = j + 1
    lane = lax.broadcasted_iota(I32, (tq, LANES), 1)

    for h in range(nh):
        pair = qb_ref[:, (h // 2) * LANES:(h // 2 + 1) * LANES].astype(F32)
        in_head = (lane >= (h % 2) * DSA_HEAD_DIM) & (lane < (h % 2 + 1) * DSA_HEAD_DIM)
        qs_ref[h * tq:(h + 1) * tq, :] = jnp.where(in_head, pair, 0.0).astype(BF16)
        grp = qi_ref[:, (h // 4) * LANES:(h // 4 + 1) * LANES].astype(F32)
        in_head = (lane >= (h % 4) * IDX_DIM) & (lane < (h % 4 + 1) * IDX_DIM)
        qim_ref[h * tq:(h + 1) * tq, :] = jnp.where(in_head, grp, 0.0).astype(BF16)

    w = wt_ref[...]

    def idx_chunk(c, masked):
        k0 = pl.multiple_of(c * ck, ck)
        kc = ki_ref[pl.ds(k0, ck), :]
        for h in range(nh):
            s_ref[h] = _dot_nt(kc, qim_ref[h * tq:(h + 1) * tq, :])
        score = jnp.zeros((ck, tq), F32)
        for h in range(nh):
            score = score + w[h:h + 1, :] * jnp.maximum(s_ref[h], 0.0)
        if masked:
            score = jnp.where(_admissible(ck, tq, k0, q0), score, NEG)
        score = jnp.where(score == 0.0, 0.0, score)
        bits = lax.bitcast_convert_type(score, I32)
        keys_ref[c] = bits ^ ((bits >> 31) & 0x7FFFFFFF)

    def idx_body(c, carry):
        idx_chunk(c, False)
        return carry

    lax.fori_loop(0, nck - 1, idx_body, 0)
    idx_chunk(nck - 1, True)

    thr_ref[...] = jnp.full(thr_ref.shape, _KEY_NEG + 1, I32)
    cut_ref[...] = jnp.full(cut_ref.shape, seq, I32)

    def count_keys(pred):
        def body(c, acc):
            part = jnp.where(pred(c), 1.0, 0.0).reshape(ck // 8, 8, tq)
            while part.shape[0] > 1:
                half = part.shape[0] // 2
                part = part[:half] + part[half:]
            return acc + part[0]
        acc = lax.fori_loop(0, nck, body, jnp.zeros((8, tq), F32))
        return jnp.sum(acc, axis=0, keepdims=True)

    @pl.when(q0 + tq > topk)
    def _search():
        def step(i, carry):
            thr, cge = carry
            cand = thr + lax.shift_left(jnp.int32(1), 31 - i)
            cnt = count_keys(lambda c: keys_ref[c] >= cand)
            ok = cnt >= topk
            return jnp.where(ok, cand, thr), jnp.where(ok, cnt, cge)

        thr0 = jnp.full((1, tq), _INT_MIN, I32)
        cge0 = jnp.full((1, tq), 1.0, F32) * (nck * ck).astype(F32)
        thr, cge = lax.fori_loop(0, 32, step, (thr0, cge0))
        thr_ref[...] = thr
        excess = cge - float(topk)

        @pl.when(jnp.max(excess) > 0.0)
        def _ties():
            nbits = max(1, int(seq - 1).bit_length())

            def tstep(i, cut):
                cand = cut + lax.shift_left(jnp.int32(1), nbits - 1 - i)

                def pred(c):
                    idx = c * ck + lax.broadcasted_iota(I32, (ck, tq), 0)
                    return (keys_ref[c] == thr) & (idx >= cand)

                return jnp.where(count_keys(pred) >= excess, cand, cut)

            cut = lax.fori_loop(0, nbits, tstep, jnp.zeros((1, tq), I32))
            cut_ref[...] = jnp.where(excess > 0.0, cut, seq)

    m_ref[...] = jnp.full(m_ref.shape, NEG, F32)
    l_ref[...] = jnp.zeros(l_ref.shape, F32)
    acc_ref[...] = jnp.zeros(acc_ref.shape, F32)
    thr = thr_ref[...]
    cut = cut_ref[...]

    def att_chunk(c, bias_slab, own):
        k0 = pl.multiple_of(c * ck, ck)
        kc = kk_ref[pl.ds(k0, ck), :]
        for h in range(nh):
            s_ref[h] = _dot_nt(kc, qs_ref[h * tq:(h + 1) * tq, :])
        key = keys_ref[c]
        idx = k0 + lax.broadcasted_iota(I32, (ck, tq), 0)
        sel = (key > thr) | ((key == thr) & (idx < cut))
        if own:
            sel = sel & _admissible(ck, tq, k0, q0)
        vt = vt_ref[c]
        for h in range(nh):
            s = s_ref[h]
            if bias_slab is not None:
                s = s + bt_ref[bias_slab, h]
            _softmax_step(jnp.where(sel, s, NEG), vt, m_ref, l_ref, acc_ref, h)

    def far_body(c, carry):
        att_chunk(c, None, False)
        return carry

    lax.fori_loop(0, j - 1, far_body, 0)

    @pl.when(j > 0)
    def _prev():
        att_chunk(j - 1, 0, False)

    att_chunk(j, 1, True)
    _finish(o_ref, l_ref, acc_ref, nh)


def _dsa_attn(qb, qi, wt, kk, vt, ki, bt, tq, topk):
    b, t, _ = qb.shape
    ck = tq
    assert tq == bt.shape[2] and tq % LANES == 0 and t % tq == 0
    nh = DSA_HEADS
    blk = lambda a: pl.BlockSpec((None, tq, a.shape[2]), lambda i, j: (i, j, 0))
    seqb = lambda a: pl.BlockSpec((None, t, a.shape[2]), lambda i, j: (i, 0, 0))
    return pl.pallas_call(
        functools.partial(_dsa_kernel, tq=tq, topk=topk),
        grid=(b, t // tq),
        in_specs=[blk(qb), blk(qi), pl.BlockSpec((None, nh, tq), lambda i, j: (i, 0, j)), seqb(kk),
                  pl.BlockSpec((None, t // ck, DSA_HEAD_DIM, ck), lambda i, j: (i, 0, 0, 0)), seqb(ki),
                  pl.BlockSpec(bt.shape, lambda i, j: (0, 0, 0, 0))],
        out_specs=blk(qb),
        out_shape=jax.ShapeDtypeStruct(qb.shape, BF16),
        scratch_shapes=[pltpu.VMEM((nh * tq, LANES), BF16), pltpu.VMEM((nh * tq, LANES), BF16),
                        pltpu.VMEM((t // ck, ck, tq), I32), pltpu.VMEM((nh, ck, tq), F32),
                        pltpu.VMEM((1, tq), I32),
                        pltpu.VMEM((1, tq), I32), pltpu.VMEM((nh, 1, tq), F32),
                        pltpu.VMEM((nh, 1, tq), F32), pltpu.VMEM((nh, DSA_HEAD_DIM, tq), F32)],
        compiler_params=pltpu.CompilerParams(dimension_semantics=("parallel", "arbitrary"),
                                             vmem_limit_bytes=VMEM_LIMIT),
        name="dsa_attn",
    )(qb, qi, wt, kk, vt, ki, bt)


_R_GRP = 0
_R_EXP = N_GROUPS


def _post_kernel(x_ref, oa_ref, ob_ref, g0_ref, b0_ref, wg_ref, bg_ref, wua_ref, wub_ref, wo_ref,
                 g1_ref, b1_ref, wr_ref, br_ref, h_ref, comb_ref, *, alpha):
    d = x_ref.shape[1]
    xn = _ln(x_ref[...], g0_ref[...], b0_ref[...])
    z = _dot(xn.astype(BF16), wg_ref[...]) + bg_ref[...]
    gates = 1.0 / (1.0 + jnp.exp(-z))
    mix = gates[:, :d] * _dot(oa_ref[...], wua_ref[...]) + gates[:, d:] * _dot(ob_ref[...], wub_ref[...])
    mixed = _dot(mix.astype(BF16), wo_ref[...])
    h = _ln(alpha * xn + mixed, g1_ref[...], b1_ref[...])
    h_ref[...] = h

    logits = _dot(h.astype(BF16), wr_ref[...]) + br_ref[...]
    lane = lax.broadcasted_iota(I32, logits.shape, 1).astype(F32)
    gmask = lane < N_GROUPS
    gl = jnp.where(gmask, logits, -jnp.inf)
    gmax = jnp.max(gl, axis=1, keepdims=True)
    g_sel = jnp.min(jnp.where(gl == gmax, lane, float(LANES)), axis=1, keepdims=True)
    g_w = 1.0 / jnp.sum(jnp.where(gmask, jnp.exp(gl - gmax), 0.0), axis=1, keepdims=True)
    e_lo = _R_EXP + g_sel * EXPERTS_PER_GROUP
    el = jnp.where((lane >= e_lo) & (lane < e_lo + EXPERTS_PER_GROUP), logits, -jnp.inf)
    m1 = jnp.max(el, axis=1, keepdims=True)
    i1 = jnp.min(jnp.where(el == m1, lane, float(LANES)), axis=1, keepdims=True)
    el2 = jnp.where(lane == i1, -jnp.inf, el)
    m2 = jnp.max(el2, axis=1, keepdims=True)
    i2 = jnp.min(jnp.where(el2 == m2, lane, float(LANES)), axis=1, keepdims=True)
    e21 = jnp.exp(m2 - m1)
    w1 = g_w / (1.0 + e21)
    w2 = g_w * e21 / (1.0 + e21)
    comb_ref[...] = jnp.where(lane == i1, w1, 0.0) + jnp.where(lane == i2, w2, 0.0)


def _post_attn(x2, oa, ob, g0, b0, wg, bg, wua, wub, wo, g1, b1, wr, br, alpha, tm):
    n, d = x2.shape
    row = lambda w: pl.BlockSpec((tm, w), lambda i: (i, 0))
    full = lambda a: pl.BlockSpec(a.shape, lambda i: (0,) * a.ndim)
    return pl.pallas_call(
        functools.partial(_post_kernel, alpha=alpha),
        grid=(n // tm,),
        in_specs=[row(d), row(oa.shape[1]), row(ob.shape[1])]
                 + [full(a) for a in (g0, b0, wg, bg, wua, wub, wo, g1, b1, wr, br)],
        out_specs=[row(d), row(LANES)],
        out_shape=[jax.ShapeDtypeStruct((n, d), F32), jax.ShapeDtypeStruct((n, LANES), F32)],
        compiler_params=pltpu.CompilerParams(dimension_semantics=("parallel",),
                                             vmem_limit_bytes=VMEM_LIMIT),
        name="post_attn",
    )(x2, oa, ob, g0, b0, wg, bg, wua, wub, wo, g1, b1, wr, br)


def _moe_kernel(h_ref, comb_ref, weg_ref, weu_ref, wed_ref, g2_ref, b2_ref, o_ref, acc_ref, *, alpha):
    e = pl.program_id(1)

    @pl.when(e == 0)
    def _init():
        acc_ref[...] = jnp.zeros(acc_ref.shape, F32)

    hb = h_ref[...].astype(BF16)
    a = _dot(hb, weg_ref[...])
    u = _dot(hb, weu_ref[...])
    comb = comb_ref[...]
    lane = lax.broadcasted_iota(I32, comb.shape, 1)
    cw = jnp.sum(jnp.where(lane == _R_EXP + e, comb, 0.0), axis=1, keepdims=True)
    hid = (a / (1.0 + jnp.exp(-a))) * u * cw
    acc_ref[...] += _dot(hid.astype(BF16), wed_ref[...])

    @pl.when(e == pl.num_programs(1) - 1)
    def _fin():
        o_ref[...] = _ln(alpha * h_ref[...] + acc_ref[...], g2_ref[...], b2_ref[...])


def _moe(h, comb, weg, weu, wed, g2, b2, alpha, tm):
    n, d = h.shape
    ne, _, f = weg.shape
    row = lambda w: pl.BlockSpec((tm, w), lambda i, e: (i, 0))
    return pl.pallas_call(
        functools.partial(_moe_kernel, alpha=alpha),
        grid=(n // tm, ne),
        in_specs=[row(d), row(LANES),
                  pl.BlockSpec((None, d, f), lambda i, e: (e, 0, 0)),
                  pl.BlockSpec((None, d, f), lambda i, e: (e, 0, 0)),
                  pl.BlockSpec((None, f, d), lambda i, e: (e, 0, 0)),
                  pl.BlockSpec(g2.shape, lambda i, e: (0, 0)),
                  pl.BlockSpec(b2.shape, lambda i, e: (0, 0))],
        out_specs=row(d),
        out_shape=jax.ShapeDtypeStruct((n, d), F32),
        scratch_shapes=[pltpu.VMEM((tm, d), F32)],
        compiler_params=pltpu.CompilerParams(dimension_semantics=("parallel", "arbitrary"),
                                             vmem_limit_bytes=VMEM_LIMIT),
        name="moe",
    )(h, comb, weg, weu, wed, g2, b2)


def _layout_weights(w_in, w_uq, w_uk, w_uv):
    d = w_in.shape[0]
    sizes = (MLA_Q_RANK, MLA_KV_RANK, MLA_ROPE, DSA_HEADS * DSA_HEAD_DIM, DSA_HEAD_DIM, DSA_HEAD_DIM,
             IDX_HEADS * IDX_DIM, IDX_DIM, IDX_HEADS)
    offs = np.cumsum((0,) + sizes)
    c_q, c_kv, k_rope, q_b, k_b, v_b, q_idx, k_idx, w_idx = (
        w_in[:, int(offs[i]):int(offs[i + 1])] for i in range(len(sizes)))
    half = MLA_ROPE // 2
    z = lambda w: jnp.zeros((d, w), w_in.dtype)
    pad = LANES - MLA_NOPE - MLA_ROPE
    grp_a = jnp.concatenate([w_idx, z(MLA_NOPE - IDX_HEADS), k_rope, z(pad)], axis=1)
    grp_b = jnp.concatenate([z(MLA_NOPE), k_rope[:, half:], k_rope[:, :half], z(pad)], axis=1)
    w1 = jnp.concatenate([c_q, c_kv, grp_a, grp_b, q_b * DSA_HEAD_DIM ** -0.5, k_b, k_b, v_b, v_b, q_idx,
                          k_idx, k_idx, k_idx, k_idx], axis=1)
    assert w1.shape[1] == _C_END

    dk = MLA_NOPE + MLA_ROPE
    rq = w_uq.shape[0]
    zq = lambda w: jnp.zeros((rq, w), w_uq.dtype)
    wq_parts, wqr_parts, wk_parts = [], [], []
    for h in range(MLA_HEADS):
        nope = w_uq[:, h * dk:h * dk + MLA_NOPE]
        rope = w_uq[:, h * dk + MLA_NOPE:(h + 1) * dk]
        wq_parts += [nope, rope, zq(pad)]
        wqr_parts += [zq(MLA_NOPE), rope[:, half:], rope[:, :half], zq(pad)]
        wk_parts += [w_uk[:, h * MLA_NOPE:(h + 1) * MLA_NOPE],
                     jnp.zeros((w_uk.shape[0], LANES - MLA_NOPE), w_uk.dtype)]
    wq = jnp.concatenate(wq_parts, axis=1)
    wqr = jnp.concatenate(wqr_parts, axis=1)
    wk = jnp.concatenate(wk_parts, axis=1)

    inv = ROPE_THETA ** (-np.arange(half, dtype=np.float32) / half)
    invf = np.zeros((1, LANES), np.float32)
    invf[0, MLA_NOPE:MLA_NOPE + half] = inv
    invf[0, MLA_NOPE + half:MLA_NOPE + MLA_ROPE] = inv
    sgn = np.zeros((1, LANES), np.float32)
    sgn[0, MLA_NOPE:MLA_NOPE + half] = -1.0
    sgn[0, MLA_NOPE + half:MLA_NOPE + MLA_ROPE] = 1.0
    return (w1.astype(BF16), wq.astype(BF16), wqr.astype(BF16), wk.astype(BF16), w_uv.astype(BF16),
            jnp.asarray(invf), jnp.asarray(sgn))


def kernel(x, positions, ln0_g, ln0_b, w_in, q_norm_g, w_uq, kv_norm_g, w_uk, w_uv, rel_bias, w_up_a, w_up_b,
           w_gate, b_gate, w_o, ln1_g, ln1_b, w_grp, b_grp, w_rt, b_rt, w_exp_gate, w_exp_up, w_exp_down,
           ln2_g, ln2_b):
    bsz, seq, d = x.shape
    depth = w_in.shape[0]
    assert depth == 1
    alpha = (2.0 * depth) ** 0.25
    n = bsz * seq
    topk = min(IDX_TOPK_MAX, seq // 4)
    r2 = lambda v: v.reshape(1, -1)

    x2 = x.reshape(n, d)
    pos2 = positions.reshape(n, 1)
    w1, wq, wqr, wk, wv, invf, sgn = _layout_weights(w_in[0], w_uq[0], w_uk[0], w_uv[0])

    blk = 256
    bt = _bias_tiles(rel_bias, blk)
    qm, km, vm, qb, kk, vv, qi, ki, wi = _in_proj(
        x2, pos2, r2(ln0_g), r2(ln0_b), w1, r2(q_norm_g[0]), wq, wqr, r2(kv_norm_g[0]), wk, wv, invf, sgn,
        tm=512)
    b3 = lambda a: a.reshape(bsz, seq, a.shape[1])
    vchunks = lambda a, dv: a[:, :dv].reshape(bsz, seq // blk, blk, dv).transpose(0, 1, 3, 2)
    o_a = _mla_attn(b3(qm), b3(km), vchunks(vm, MLA_HEADS * MLA_V), tq=blk, ck=blk)
    o_b = _dsa_attn(b3(qb), b3(qi), b3(wi).transpose(0, 2, 1), b3(kk), vchunks(vv, DSA_HEAD_DIM), b3(ki), bt,
                    tq=blk, topk=topk)

    wr = jnp.concatenate([w_grp[0], w_rt[0], jnp.zeros((d, LANES - N_GROUPS - N_EXPERTS), F32)], axis=1)
    br = jnp.concatenate([b_grp[0], b_rt[0], jnp.zeros((LANES - N_GROUPS - N_EXPERTS,), F32)])
    h, comb = _post_attn(
        x2, o_a.reshape(n, -1), o_b.reshape(n, -1), r2(ln0_g), r2(ln0_b), w_gate[0].astype(BF16), r2(b_gate[0]),
        w_up_a[0].astype(BF16), w_up_b[0].astype(BF16), w_o[0].astype(BF16), r2(ln1_g[0]), r2(ln1_b[0]),
        wr.astype(BF16), r2(br), alpha, tm=256)
    out = _moe(h, comb, w_exp_gate[0].astype(BF16), w_exp_up[0].astype(BF16), w_exp_down[0].astype(BF16),
               r2(ln2_g[0]), r2(ln2_b[0]), alpha, tm=1024)
    return out.reshape(bsz, seq, d)
```

```python
import functools
import math

import numpy as np
import jax
import jax.numpy as jnp
from jax import lax
from jax.experimental import pallas as pl
from jax.experimental.pallas import tpu as pltpu

F32 = jnp.float32
BF16 = jnp.bfloat16
I32 = jnp.int32

CHUNK = 64
MLA_HEADS = 8
MLA_Q_RANK = 256
MLA_KV_RANK = 128
MLA_NOPE = 64
MLA_ROPE = 32
MLA_V = 64
ROPE_THETA = 10000.0
DSA_HEADS = 8
DSA_HEAD_DIM = 64
IDX_HEADS = 8
IDX_DIM = 32
IDX_TOPK_MAX = 256
REL_BUCKETS = 32
REL_MAX_DIST = 128
N_GROUPS = 4
EXPERTS_PER_GROUP = 8
N_EXPERTS = N_GROUPS * EXPERTS_PER_GROUP
EXPERT_HIDDEN = 256
LN_EPS = 1e-5
RMS_EPS = 1e-6
NEG = -1e30

LANES = 128
VMEM_LIMIT = 48 * 1024 * 1024

_KEY_NEG = int(np.array(NEG, np.float32).view(np.int32)) ^ 0x7FFFFFFF
_INT_MIN = -(2 ** 31)
_CHUNK_SHIFT = CHUNK.bit_length() - 1
assert 1 << _CHUNK_SHIFT == CHUNK


def _ln(x, g, b):
    mu = jnp.mean(x, axis=-1, keepdims=True)
    xc = x - mu
    var = jnp.mean(xc * xc, axis=-1, keepdims=True)
    return xc * lax.rsqrt(var + LN_EPS) * g + b


def _rms(x, g):
    return x * lax.rsqrt(jnp.mean(x * x, axis=-1, keepdims=True) + RMS_EPS) * g


def _dot(a, b):
    return jnp.dot(a, b, preferred_element_type=F32)


def _dot_nt(a, b):
    return lax.dot_general(a, b, (((1,), (1,)), ((), ())), preferred_element_type=F32)


def _bias_kernel(rb_ref, o_ref):
    nb = REL_BUCKETS // 2
    max_exact = nb // 2
    blk = o_ref.shape[2]
    s = lax.broadcasted_iota(I32, (blk, blk), 0)
    t = lax.broadcasted_iota(I32, (blk, blk), 1)
    for which, off in ((0, -blk), (1, 0)):
        rel = s - t + off
        ret = jnp.where(rel > 0, nb, 0)
        n = jnp.abs(rel)
        large = max_exact + (jnp.log(jnp.maximum(n, 1).astype(F32) / max_exact)
                             / math.log(REL_MAX_DIST / max_exact) * (nb - max_exact)).astype(I32)
        large = jnp.minimum(large, nb - 1)
        bucket = ret + jnp.where(n < max_exact, n, large)
        for h in range(DSA_HEADS):
            acc = jnp.zeros((blk, blk), F32)
            for b in range(REL_BUCKETS):
                acc = jnp.where(bucket == b, rb_ref[b, h], acc)
            o_ref[which, h] = acc - rb_ref[nb - 1, h]


def _bias_tiles(rel_bias, blk):
    assert blk >= REL_MAX_DIST
    return pl.pallas_call(
        _bias_kernel,
        out_shape=jax.ShapeDtypeStruct((2, DSA_HEADS, blk, blk), F32),
        in_specs=[pl.BlockSpec(memory_space=pltpu.SMEM)],
        out_specs=pl.BlockSpec(memory_space=pltpu.VMEM),
        name="bias_tiles",
    )(rel_bias)


_C_Q, _C_KV, _C_RA, _C_RB, _C_QB, _C_KK, _C_VV, _C_QI, _C_KI, _C_END = (
    0, 256, 384, 512, 640, 1152, 1280, 1408, 1664, 1792)


def _in_proj_kernel(x_ref, pos_ref, g0_ref, b0_ref, w1_ref, qg_ref, wq_ref, wqr_ref, kg_ref, wk_ref,
                    wv_ref, invf_ref, sgn_ref,
                    qm_ref, km_ref, vm_ref, qb_ref, kk_ref, vv_ref, qi_ref, ki_ref, wi_ref):
    xn = _ln(x_ref[...], g0_ref[...], b0_ref[...])
    proj = _dot(xn.astype(BF16), w1_ref[...])
    qb_ref[...] = proj[:, _C_QB:_C_KK].astype(BF16)
    kk_ref[...] = proj[:, _C_KK:_C_VV].astype(BF16)
    vv_ref[...] = proj[:, _C_VV:_C_QI].astype(BF16)
    qi_ref[...] = proj[:, _C_QI:_C_KI].astype(BF16)
    ki_ref[...] = proj[:, _C_KI:_C_END].astype(BF16)
    ga = proj[:, _C_RA:_C_RB]
    gb = proj[:, _C_RB:_C_QB]
    wi_ref[...] = ga[:, 0:IDX_HEADS] * (IDX_HEADS ** -0.5 * IDX_DIM ** -0.5)

    ang = pos_ref[...].astype(F32) * invf_ref[...]
    cosv = jnp.cos(ang)
    sinv = jnp.sin(ang) * sgn_ref[...]
    lane = lax.broadcasted_iota(I32, ang.shape, 1)
    rope_lane = (lane >= MLA_NOPE) & (lane < MLA_NOPE + MLA_ROPE)
    kr = jnp.where(rope_lane, ga * cosv + gb * sinv, 0.0)

    scale = (MLA_NOPE + MLA_ROPE) ** -0.5
    cos8 = jnp.concatenate([cosv * scale] * MLA_HEADS, axis=1)
    sin8 = jnp.concatenate([sinv * scale] * MLA_HEADS, axis=1)
    cqn = _rms(proj[:, _C_Q:_C_KV], qg_ref[...]).astype(BF16)
    q = _dot(cqn, wq_ref[...]) * cos8 + _dot(cqn, wqr_ref[...]) * sin8
    qm_ref[...] = q.astype(BF16)

    ckn = _rms(proj[:, _C_KV:_C_RA], kg_ref[...]).astype(BF16)
    k = _dot(ckn, wk_ref[...]) + jnp.concatenate([kr] * MLA_HEADS, axis=1)
    km_ref[...] = k.astype(BF16)
    vm_ref[...] = _dot(ckn, wv_ref[...]).astype(BF16)


def _in_proj(x2, pos2, g0, b0, w1, qg, wq, wqr, kg, wk, wv, invf, sgn, tm):
    n, d = x2.shape
    hm = MLA_HEADS * LANES
    row = lambda w: pl.BlockSpec((tm, w), lambda i: (i, 0))
    full = lambda a: pl.BlockSpec(a.shape, lambda i: (0,) * a.ndim)
    outs = [(hm, BF16), (hm, BF16), (MLA_HEADS * MLA_V, BF16), (DSA_HEADS * DSA_HEAD_DIM, BF16),
            (LANES, BF16), (LANES, BF16), (IDX_HEADS * IDX_DIM, BF16), (LANES, BF16), (IDX_HEADS, F32)]
    return pl.pallas_call(
        _in_proj_kernel,
        grid=(n // tm,),
        in_specs=[row(d), row(1)] + [full(a) for a in (g0, b0, w1, qg, wq, wqr, kg, wk, wv, invf, sgn)],
        out_specs=[row(w) for w, _ in outs],
        out_shape=[jax.ShapeDtypeStruct((n, w), dt) for w, dt in outs],
        compiler_params=pltpu.CompilerParams(dimension_semantics=("parallel",),
                                             vmem_limit_bytes=VMEM_LIMIT),
        name="in_proj",
    )(x2, pos2, g0, b0, w1, qg, wq, wqr, kg, wk, wv, invf, sgn)


def _admissible(ck, tq, k0, q0):
    s = k0 + lax.broadcasted_iota(I32, (ck, tq), 0)
    t = q0 + lax.broadcasted_iota(I32, (ck, tq), 1)
    return (s >> _CHUNK_SHIFT) <= (t >> _CHUNK_SHIFT)


def _softmax_step(s, vt, m_ref, l_ref, acc_ref, h):
    m_prev = m_ref[h]
    m_new = jnp.maximum(m_prev, jnp.max(s, axis=0, keepdims=True))
    alpha = jnp.exp(m_prev - m_new)
    p = jnp.exp(s - m_new)
    l_ref[h] = alpha * l_ref[h] + jnp.sum(p, axis=0, keepdims=True)
    acc_ref[h] = alpha * acc_ref[h] + _dot(vt, p.astype(BF16))
    m_ref[h] = m_new


def _finish(o_ref, l_ref, acc_ref, nh):
    ot = jnp.concatenate([acc_ref[h] / l_ref[h] for h in range(nh)], axis=0)
    o_ref[...] = ot.T.astype(BF16)


def _mla_kernel(q_ref, k_ref, vt_ref, o_ref, s_ref, m_ref, l_ref, acc_ref, *, tq, ck):
    j = pl.program_id(1)
    q0 = j * tq
    m_ref[...] = jnp.full(m_ref.shape, NEG, F32)
    l_ref[...] = jnp.zeros(l_ref.shape, F32)
    acc_ref[...] = jnp.zeros(acc_ref.shape, F32)
    nfull = (q0 + CHUNK) // ck

    def chunk(c, masked):
        k0 = pl.multiple_of(c * ck, ck)
        for h in range(MLA_HEADS):
            s_ref[h] = _dot_nt(k_ref[pl.ds(k0, ck), h * LANES:(h + 1) * LANES],
                               q_ref[:, h * LANES:(h + 1) * LANES])
        adm = _admissible(ck, tq, k0, q0) if masked else None
        for h in range(MLA_HEADS):
            s = s_ref[h]
            if masked:
                s = jnp.where(adm, s, NEG)
            _softmax_step(s, vt_ref[c, h * MLA_V:(h + 1) * MLA_V, :], m_ref, l_ref, acc_ref, h)

    def full_body(c, carry):
        chunk(c, False)
        return carry

    lax.fori_loop(0, nfull, full_body, 0)
    chunk(nfull, True)
    _finish(o_ref, l_ref, acc_ref, MLA_HEADS)


def _mla_attn(qm, km, vmt, tq, ck):
    b, t, hm = qm.shape
    dv = vmt.shape[2]
    assert ck % tq == 0 and tq > CHUNK and tq % CHUNK == 0 and t % ck == 0
    return pl.pallas_call(
        functools.partial(_mla_kernel, tq=tq, ck=ck),
        grid=(b, t // tq),
        in_specs=[pl.BlockSpec((None, tq, hm), lambda i, j: (i, j, 0)),
                  pl.BlockSpec((None, t, hm), lambda i, j: (i, 0, 0)),
                  pl.BlockSpec((None, t // ck, dv, ck), lambda i, j: (i, 0, 0, 0))],
        out_specs=pl.BlockSpec((None, tq, dv), lambda i, j: (i, j, 0)),
        out_shape=jax.ShapeDtypeStruct((b, t, dv), BF16),
        scratch_shapes=[pltpu.VMEM((MLA_HEADS, ck, tq), F32),
                        pltpu.VMEM((MLA_HEADS, 1, tq), F32), pltpu.VMEM((MLA_HEADS, 1, tq), F32),
                        pltpu.VMEM((MLA_HEADS, MLA_V, tq), F32)],
        compiler_params=pltpu.CompilerParams(dimension_semantics=("parallel", "arbitrary"),
                                             vmem_limit_bytes=VMEM_LIMIT),
        name="mla_attn",
    )(qm, km, vmt)


def _dsa_kernel(qb_ref, qi_ref, wt_ref, kk_ref, vt_ref, ki_ref, bt_ref, o_ref,
                qs_ref, qim_ref, keys_ref, s_ref, thr_ref, cut_ref, m_ref, l_ref, acc_ref, *, tq, topk):
    nh = DSA_HEADS
    ck = tq
    seq = kk_ref.shape[0]
    j = pl.program_id(1)
    q0 = j * tq
    nck = j + 1
    lane = lax.broadcasted_iota(I32, (tq, LANES), 1)

    for h in range(nh):
        pair = qb_ref[:, (h // 2) * LANES:(h // 2 + 1) * LANES].astype(F32)
        in_head = (lane >= (h % 2) * DSA_HEAD_DIM) & (lane < (h % 2 + 1) * DSA_HEAD_DIM)
        qs_ref[h * tq:(h + 1) * tq, :] = jnp.where(in_head, pair, 0.0).astype(BF16)
        grp = qi_ref[:, (h // 4) * LANES:(h // 4 + 1) * LANES].astype(F32)
        in_head = (lane >= (h % 4) * IDX_DIM) & (lane < (h % 4 + 1) * IDX_DIM)
        qim_ref[h * tq:(h + 1) * tq, :] = jnp.where(in_head, grp, 0.0).astype(BF16)

    w = wt_ref[...]

    def idx_chunk(c, masked):
        k0 = pl.multiple_of(c * ck, ck)
        kc = ki_ref[pl.ds(k0, ck), :]
        for h in range(nh):
            s_ref[h] = _dot_nt(kc, qim_ref[h * tq:(h + 1) * tq, :])
        score = jnp.zeros((ck, tq), F32)
        for h in range(nh):
            score = score + w[h:h + 1, :] * jnp.maximum(s_ref[h], 0.0)
        if masked:
            score = jnp.where(_admissible(ck, tq, k0, q0), score, NEG)
        score = jnp.where(score == 0.0, 0.0, score)
        bits = lax.bitcast_convert_type(score, I32)
        keys_ref[c] = bits ^ ((bits >> 31) & 0x7FFFFFFF)

    def idx_body(c, carry):
        idx_chunk(c, False)
        return carry

    lax.fori_loop(0, nck - 1, idx_body, 0)
    idx_chunk(nck - 1, True)

    thr_ref[...] = jnp.full(thr_ref.shape, _KEY_NEG + 1, I32)
    cut_ref[...] = jnp.full(cut_ref.shape, seq, I32)

    def count_keys(pred):
        def body(c, acc):
            part = jnp.where(pred(c), 1.0, 0.0).reshape(ck // 8, 8, tq)
            while part.shape[0] > 1:
                half = part.shape[0] // 2
                part = part[:half] + part[half:]
            return acc + part[0]
        acc = lax.fori_loop(0, nck, body, jnp.zeros((8, tq), F32))
        return jnp.sum(acc, axis=0, keepdims=True)

    @pl.when(q0 + tq > topk)
    def _search():
        def step(i, carry):
            thr, cge = carry
            cand = thr + lax.shift_left(jnp.int32(1), 31 - i)
            cnt = count_keys(lambda c: keys_ref[c] >= cand)
            ok = cnt >= topk
            return jnp.where(ok, cand, thr), jnp.where(ok, cnt, cge)

        thr0 = jnp.full((1, tq), _INT_MIN, I32)
        cge0 = jnp.full((1, tq), 1.0, F32) * (nck * ck).astype(F32)
        thr, cge = lax.fori_loop(0, 32, step, (thr0, cge0))
        thr_ref[...] = thr
        excess = cge - float(topk)

        @pl.when(jnp.max(excess) > 0.0)
        def _ties():
            nbits = max(1, int(seq - 1).bit_length())

            def tstep(i, cut):
                cand = cut + lax.shift_left(jnp.int32(1), nbits - 1 - i)

                def pred(c):
                    idx = c * ck + lax.broadcasted_iota(I32, (ck, tq), 0)
                    return (keys_ref[c] == thr) & (idx >= cand)

                return jnp.where(count_keys(pred) >= excess, cand, cut)

            cut = lax.fori_loop(0, nbits, tstep, jnp.zeros((1, tq), I32))
            cut_ref[...] = jnp.where(excess > 0.0, cut, seq)

    m_ref[...] = jnp.full(m_ref.shape, NEG, F32)
    l_ref[...] = jnp.zeros(l_ref.shape, F32)
    acc_ref[...] = jnp.zeros(acc_ref.shape, F32)
    thr = thr_ref[...]
    cut = cut_ref[...]

    def att_chunk(c, bias_slab, own):
        k0 = pl.multiple_of(c * ck, ck)
        kc = kk_ref[pl.ds(k0, ck), :]
        for h in range(nh):
            s_ref[h] = _dot_nt(kc, qs_ref[h * tq:(h + 1) * tq, :])
        key = keys_ref[c]
        idx = k0 + lax.broadcasted_iota(I32, (ck, tq), 0)
        sel = (key > thr) | ((key == thr) & (idx < cut))
        if own:
            sel = sel & _admissible(ck, tq, k0, q0)
        vt = vt_ref[c]
        for h in range(nh):
            s = s_ref[h]
            if bias_slab is not None:
                s = s + bt_ref[bias_slab, h]
            _softmax_step(jnp.where(sel, s, NEG), vt, m_ref, l_ref, acc_ref, h)

    def far_body(c, carry):
        att_chunk(c, None, False)
        return carry

    lax.fori_loop(0, j - 1, far_body, 0)

    @pl.when(j > 0)
    def _prev():
        att_chunk(j - 1, 0, False)

    att_chunk(j, 1, True)
    _finish(o_ref, l_ref, acc_ref, nh)


def _dsa_attn(qb, qi, wt, kk, vt, ki, bt, tq, topk):
    b, t, _ = qb.shape
    ck = tq
    assert tq == bt.shape[2] and tq % LANES == 0 and t % tq == 0
    nh = DSA_HEADS
    blk = lambda a: pl.BlockSpec((None, tq, a.shape[2]), lambda i, j: (i, j, 0))
    seqb = lambda a: pl.BlockSpec((None, t, a.shape[2]), lambda i, j: (i, 0, 0))
    return pl.pallas_call(
        functools.partial(_dsa_kernel, tq=tq, topk=topk),
        grid=(b, t // tq),
        in_specs=[blk(qb), blk(qi), pl.BlockSpec((None, nh, tq), lambda i, j: (i, 0, j)), seqb(kk),
                  pl.BlockSpec((None, t // ck, DSA_HEAD_DIM, ck), lambda i, j: (i, 0, 0, 0)), seqb(ki),
                  pl.BlockSpec(bt.shape, lambda i, j: (0, 0, 0, 0))],
        out_specs=blk(qb),
        out_shape=jax.ShapeDtypeStruct(qb.shape, BF16),
        scratch_shapes=[pltpu.VMEM((nh * tq, LANES), BF16), pltpu.VMEM((nh * tq, LANES), BF16),
                        pltpu.VMEM((t // ck, ck, tq), I32), pltpu.VMEM((nh, ck, tq), F32),
                        pltpu.VMEM((1, tq), I32),
                        pltpu.VMEM((1, tq), I32), pltpu.VMEM((nh, 1, tq), F32),
                        pltpu.VMEM((nh, 1, tq), F32), pltpu.VMEM((nh, DSA_HEAD_DIM, tq), F32)],
        compiler_params=pltpu.CompilerParams(dimension_semantics=("parallel", "arbitrary"),
                                             vmem_limit_bytes=VMEM_LIMIT),
        name="dsa_attn",
    )(qb, qi, wt, kk, vt, ki, bt)


_R_GRP = 0
_R_EXP = N_GROUPS


def _post_kernel(x_ref, oa_ref, ob_ref, g0_ref, b0_ref, wg_ref, bg_ref, wua_ref, wub_ref, wo_ref,
                 g1_ref, b1_ref, wr_ref, br_ref, hx_ref, gsel_ref, *, alpha):
    d = x_ref.shape[1]
    xn = _ln(x_ref[...], g0_ref[...], b0_ref[...])
    z = _dot(xn.astype(BF16), wg_ref[...]) + bg_ref[...]
    gates = 1.0 / (1.0 + jnp.exp(-z))
    mix = gates[:, :d] * _dot(oa_ref[...], wua_ref[...]) + gates[:, d:] * _dot(ob_ref[...], wub_ref[...])
    mixed = _dot(mix.astype(BF16), wo_ref[...])
    h = _ln(alpha * xn + mixed, g1_ref[...], b1_ref[...])
    hx_ref[:, :d] = h

    logits = _dot(h.astype(BF16), wr_ref[...]) + br_ref[...]
    lane = lax.broadcasted_iota(I32, logits.shape, 1).astype(F32)
    gmask = lane < N_GROUPS
    gl = jnp.where(gmask, logits, -jnp.inf)
    gmax = jnp.max(gl, axis=1, keepdims=True)
    g_sel = jnp.min(jnp.where(gl == gmax, lane, float(LANES)), axis=1, keepdims=True)
    g_w = 1.0 / jnp.sum(jnp.where(gmask, jnp.exp(gl - gmax), 0.0), axis=1, keepdims=True)
    e_lo = _R_EXP + g_sel * EXPERTS_PER_GROUP
    el = jnp.where((lane >= e_lo) & (lane < e_lo + EXPERTS_PER_GROUP), logits, -jnp.inf)
    m1 = jnp.max(el, axis=1, keepdims=True)
    i1 = jnp.min(jnp.where(el == m1, lane, float(LANES)), axis=1, keepdims=True)
    el2 = jnp.where(lane == i1, -jnp.inf, el)
    m2 = jnp.max(el2, axis=1, keepdims=True)
    i2 = jnp.min(jnp.where(el2 == m2, lane, float(LANES)), axis=1, keepdims=True)
    e21 = jnp.exp(m2 - m1)
    w1 = g_w / (1.0 + e21)
    w2 = g_w * e21 / (1.0 + e21)
    hx_ref[:, d:] = jnp.where(lane == i1, w1, 0.0) + jnp.where(lane == i2, w2, 0.0)
    gsel_ref[...] = g_sel.astype(I32)


def _post_attn(x2, oa, ob, g0, b0, wg, bg, wua, wub, wo, g1, b1, wr, br, alpha, tm):
    n, d = x2.shape
    row = lambda w: pl.BlockSpec((tm, w), lambda i: (i, 0))
    full = lambda a: pl.BlockSpec(a.shape, lambda i: (0,) * a.ndim)
    return pl.pallas_call(
        functools.partial(_post_kernel, alpha=alpha),
        grid=(n // tm,),
        in_specs=[row(d), row(oa.shape[1]), row(ob.shape[1])]
                 + [full(a) for a in (g0, b0, wg, bg, wua, wub, wo, g1, b1, wr, br)],
        out_specs=[row(d + LANES), row(1)],
        out_shape=[jax.ShapeDtypeStruct((n, d + LANES), F32), jax.ShapeDtypeStruct((n, 1), I32)],
        compiler_params=pltpu.CompilerParams(dimension_semantics=("parallel",),
                                             vmem_limit_bytes=VMEM_LIMIT),
        name="post_attn",
    )(x2, oa, ob, g0, b0, wg, bg, wua, wub, wo, g1, b1, wr, br)


def _moe_kernel(tg_ref, nv_ref, src_ref, srcn_ref, hx_hbm, weg_ref, weu_ref, wed_ref, g2_ref, b2_ref,
                out_hbm, gbuf, obuf, gsem, ssem, *, alpha, tm, d):
    k = pl.program_id(0)
    nk = pl.num_programs(0)
    slot = k % 2

    def gather_copy(row, r, s):
        return pltpu.make_async_copy(hx_hbm.at[pl.ds(row, 1)], gbuf.at[s, pl.ds(r, 1)], gsem.at[s])

    def scatter_copy(row, r, s):
        return pltpu.make_async_copy(obuf.at[s, pl.ds(r, 1)], out_hbm.at[pl.ds(row, 1)], ssem.at[s])

    def start_gather(tile_src_ref, s):
        def body(r, carry):
            gather_copy(jnp.maximum(tile_src_ref[0, r], 0), r, s).start()
            return carry
        lax.fori_loop(0, tm, body, 0, unroll=8)

    def wait_rows(copy_of, n_rows, s):
        def body(r, carry):
            copy_of(0, r, s).wait()
            return carry
        lax.fori_loop(0, n_rows, body, 0)

    @pl.when(k == 0)
    def _first():
        start_gather(src_ref, 0)

    pltpu.make_async_copy(hx_hbm.at[pl.ds(0, tm)], gbuf.at[slot], gsem.at[slot]).wait()

    @pl.when(k + 1 < nk)
    def _prefetch():
        start_gather(srcn_ref, 1 - slot)

    @pl.when(k >= 2)
    def _free_obuf():
        wait_rows(scatter_copy, nv_ref[k - 2], slot)

    x = gbuf[slot]
    h = x[:, :d]
    comb = x[:, d:]
    hb = h.astype(BF16)
    lane = lax.broadcasted_iota(I32, comb.shape, 1)
    e0 = _R_EXP + tg_ref[k] * EXPERTS_PER_GROUP
    acc = jnp.zeros((tm, d), F32)
    for e in range(EXPERTS_PER_GROUP):
        a = _dot(hb, weg_ref[e])
        u = _dot(hb, weu_ref[e])
        cw = jnp.sum(jnp.where(lane == e0 + e, comb, 0.0), axis=1, keepdims=True)
        hid = (a / (1.0 + jnp.exp(-a))) * u * cw
        acc = acc + _dot(hid.astype(BF16), wed_ref[e])
    obuf[slot] = _ln(alpha * h + acc, g2_ref[...], b2_ref[...])

    nv = nv_ref[k]

    def sbody(r, carry):
        scatter_copy(src_ref[0, r], r, slot).start()
        return carry

    lax.fori_loop(0, nv, sbody, 0)

    @pl.when(k == nk - 1)
    def _drain():
        wait_rows(scatter_copy, nv, slot)

        @pl.when(k >= 1)
        def _():
            wait_rows(scatter_copy, nv_ref[k - 1], 1 - slot)


def _moe(hx, tile_group, n_valid, src, weg, weu, wed, g2, b2, alpha, tm, n):
    d = hx.shape[1] - LANES
    nk = src.shape[0]
    ne, _, f = weg.shape
    gsz = EXPERTS_PER_GROUP
    grid_spec = pltpu.PrefetchScalarGridSpec(
        num_scalar_prefetch=2,
        grid=(nk,),
        in_specs=[pl.BlockSpec((None, 1, tm), lambda k, tg, nv: (k, 0, 0), memory_space=pltpu.SMEM),
                  pl.BlockSpec((None, 1, tm), lambda k, tg, nv: (jnp.minimum(k + 1, nk - 1), 0, 0),
                               memory_space=pltpu.SMEM),
                  pl.BlockSpec(memory_space=pl.ANY),
                  pl.BlockSpec((gsz, d, f), lambda k, tg, nv: (tg[k], 0, 0)),
                  pl.BlockSpec((gsz, d, f), lambda k, tg, nv: (tg[k], 0, 0)),
                  pl.BlockSpec((gsz, f, d), lambda k, tg, nv: (tg[k], 0, 0)),
                  pl.BlockSpec(g2.shape, lambda k, tg, nv: (0, 0)),
                  pl.BlockSpec(b2.shape, lambda k, tg, nv: (0, 0))],
        out_specs=pl.BlockSpec(memory_space=pl.ANY),
        scratch_shapes=[pltpu.VMEM((2, tm, d + LANES), F32), pltpu.VMEM((2, tm, d), F32),
                        pltpu.SemaphoreType.DMA((2,)), pltpu.SemaphoreType.DMA((2,))])
    return pl.pallas_call(
        functools.partial(_moe_kernel, alpha=alpha, tm=tm, d=d),
        grid_spec=grid_spec,
        out_shape=jax.ShapeDtypeStruct((n, d), F32),
        compiler_params=pltpu.CompilerParams(dimension_semantics=("arbitrary",),
                                             vmem_limit_bytes=VMEM_LIMIT),
        name="moe",
    )(tile_group, n_valid, src, src, hx, weg, weu, wed, g2, b2)


def _group_tiles(gsel, tm):
    n = gsel.shape[0]
    n_tiles = n // tm + N_GROUPS
    onehot = (gsel[:, None] == jnp.arange(N_GROUPS, dtype=I32)[None, :]).astype(I32)
    csum = jnp.cumsum(onehot, axis=0)
    rank = jnp.sum(csum * onehot, axis=1) - 1
    counts = csum[-1]
    padded = (counts + tm - 1) // tm * tm
    ends = jnp.cumsum(padded)
    offs = ends - padded
    dest = offs[gsel] + rank
    src = jnp.full((n_tiles * tm,), -1, I32).at[dest].set(jnp.arange(n, dtype=I32))
    tile_start = jnp.arange(n_tiles, dtype=I32) * tm
    tile_group = jnp.minimum(jnp.sum((tile_start[:, None] >= ends[None, :]).astype(I32), axis=1), N_GROUPS - 1)
    n_valid = jnp.clip(offs[tile_group] + counts[tile_group] - tile_start, 0, tm)
    return tile_group.astype(I32), n_valid.astype(I32), src.reshape(n_tiles, 1, tm)


def _layout_weights(w_in, w_uq, w_uk, w_uv):
    d = w_in.shape[0]
    sizes = (MLA_Q_RANK, MLA_KV_RANK, MLA_ROPE, DSA_HEADS * DSA_HEAD_DIM, DSA_HEAD_DIM, DSA_HEAD_DIM,
             IDX_HEADS * IDX_DIM, IDX_DIM, IDX_HEADS)
    offs = np.cumsum((0,) + sizes)
    c_q, c_kv, k_rope, q_b, k_b, v_b, q_idx, k_idx, w_idx = (
        w_in[:, int(offs[i]):int(offs[i + 1])] for i in range(len(sizes)))
    half = MLA_ROPE // 2
    z = lambda w: jnp.zeros((d, w), w_in.dtype)
    pad = LANES - MLA_NOPE - MLA_ROPE
    grp_a = jnp.concatenate([w_idx, z(MLA_NOPE - IDX_HEADS), k_rope, z(pad)], axis=1)
    grp_b = jnp.concatenate([z(MLA_NOPE), k_rope[:, half:], k_rope[:, :half], z(pad)], axis=1)
    w1 = jnp.concatenate([c_q, c_kv, grp_a, grp_b, q_b * DSA_HEAD_DIM ** -0.5, k_b, k_b, v_b, v_b, q_idx,
                          k_idx, k_idx, k_idx, k_idx], axis=1)
    assert w1.shape[1] == _C_END

    dk = MLA_NOPE + MLA_ROPE
    rq = w_uq.shape[0]
    zq = lambda w: jnp.zeros((rq, w), w_uq.dtype)
    wq_parts, wqr_parts, wk_parts = [], [], []
    for h in range(MLA_HEADS):
        nope = w_uq[:, h * dk:h * dk + MLA_NOPE]
        rope = w_uq[:, h * dk + MLA_NOPE:(h + 1) * dk]
        wq_parts += [nope, rope, zq(pad)]
        wqr_parts += [zq(MLA_NOPE), rope[:, half:], rope[:, :half], zq(pad)]
        wk_parts += [w_uk[:, h * MLA_NOPE:(h + 1) * MLA_NOPE],
                     jnp.zeros((w_uk.shape[0], LANES - MLA_NOPE), w_uk.dtype)]
    wq = jnp.concatenate(wq_parts, axis=1)
    wqr = jnp.concatenate(wqr_parts, axis=1)
    wk = jnp.concatenate(wk_parts, axis=1)

    inv = ROPE_THETA ** (-np.arange(half, dtype=np.float32) / half)
    invf = np.zeros((1, LANES), np.float32)
    invf[0, MLA_NOPE:MLA_NOPE + half] = inv
    invf[0, MLA_NOPE + half:MLA_NOPE + MLA_ROPE] = inv
    sgn = np.zeros((1, LANES), np.float32)
    sgn[0, MLA_NOPE:MLA_NOPE + half] = -1.0
    sgn[0, MLA_NOPE + half:MLA_NOPE + MLA_ROPE] = 1.0
    return (w1.astype(BF16), wq.astype(BF16), wqr.astype(BF16), wk.astype(BF16), w_uv.astype(BF16),
            jnp.asarray(invf), jnp.asarray(sgn))


def kernel(x, positions, ln0_g, ln0_b, w_in, q_norm_g, w_uq, kv_norm_g, w_uk, w_uv, rel_bias, w_up_a, w_up_b,
           w_gate, b_gate, w_o, ln1_g, ln1_b, w_grp, b_grp, w_rt, b_rt, w_exp_gate, w_exp_up, w_exp_down,
           ln2_g, ln2_b):
    bsz, seq, d = x.shape
    depth = w_in.shape[0]
    assert depth == 1
    alpha = (2.0 * depth) ** 0.25
    n = bsz * seq
    topk = min(IDX_TOPK_MAX, seq // 4)
    r2 = lambda v: v.reshape(1, -1)

    x2 = x.reshape(n, d)
    pos2 = positions.reshape(n, 1)
    w1, wq, wqr, wk, wv, invf, sgn = _layout_weights(w_in[0], w_uq[0], w_uk[0], w_uv[0])

    blk = 256
    bt = _bias_tiles(rel_bias, blk)
    qm, km, vm, qb, kk, vv, qi, ki, wi = _in_proj(
        x2, pos2, r2(ln0_g), r2(ln0_b), w1, r2(q_norm_g[0]), wq, wqr, r2(kv_norm_g[0]), wk, wv, invf, sgn,
        tm=512)
    b3 = lambda a: a.reshape(bsz, seq, a.shape[1])
    vchunks = lambda a, dv: a[:, :dv].reshape(bsz, seq // blk, blk, dv).transpose(0, 1, 3, 2)
    o_a = _mla_attn(b3(qm), b3(km), vchunks(vm, MLA_HEADS * MLA_V), tq=blk, ck=blk)
    o_b = _dsa_attn(b3(qb), b3(qi), b3(wi).transpose(0, 2, 1), b3(kk), vchunks(vv, DSA_HEAD_DIM), b3(ki), bt,
                    tq=blk, topk=topk)

    wr = jnp.concatenate([w_grp[0], w_rt[0], jnp.zeros((d, LANES - N_GROUPS - N_EXPERTS), F32)], axis=1)
    br = jnp.concatenate([b_grp[0], b_rt[0], jnp.zeros((LANES - N_GROUPS - N_EXPERTS,), F32)])
    hx, gsel = _post_attn(
        x2, o_a.reshape(n, -1), o_b.reshape(n, -1), r2(ln0_g), r2(ln0_b), w_gate[0].astype(BF16), r2(b_gate[0]),
        w_up_a[0].astype(BF16), w_up_b[0].astype(BF16), w_o[0].astype(BF16), r2(ln1_g[0]), r2(ln1_b[0]),
        wr.astype(BF16), r2(br), alpha, tm=256)
    tm_moe = 512
    tile_group, n_valid, src = _group_tiles(gsel[:, 0], tm_moe)
    out = _moe(hx, tile_group, n_valid, src, w_exp_gate[0].astype(BF16), w_exp_up[0].astype(BF16),
               w_exp_down[0].astype(BF16), r2(ln2_g[0]), r2(ln2_b[0]), alpha, tm_moe, n)
    return out.reshape(bsz, seq, d)
```

```python
import functools
import math

import numpy as np
import jax
import jax.numpy as jnp
from jax import lax
from jax.experimental import pallas as pl
from jax.experimental.pallas import tpu as pltpu

F32 = jnp.float32
BF16 = jnp.bfloat16
I32 = jnp.int32

CHUNK = 64
MLA_HEADS = 8
MLA_Q_RANK = 256
MLA_KV_RANK = 128
MLA_NOPE = 64
MLA_ROPE = 32
MLA_V = 64
ROPE_THETA = 10000.0
DSA_HEADS = 8
DSA_HEAD_DIM = 64
IDX_HEADS = 8
IDX_DIM = 32
IDX_TOPK_MAX = 256
REL_BUCKETS = 32
REL_MAX_DIST = 128
N_GROUPS = 4
EXPERTS_PER_GROUP = 8
N_EXPERTS = N_GROUPS * EXPERTS_PER_GROUP
EXPERT_HIDDEN = 256
LN_EPS = 1e-5
RMS_EPS = 1e-6
NEG = -1e30

LANES = 128
_SUBLANES = 8
VMEM_LIMIT = 48 * 1024 * 1024

_KEY_NEG = int(np.array(NEG, np.float32).view(np.int32)) ^ 0x7FFFFFFF
_INT_MIN = -(2 ** 31)
_LOG2E = math.log2(math.e)
_CHUNK_SHIFT = CHUNK.bit_length() - 1
assert 1 << _CHUNK_SHIFT == CHUNK


def _ln(x, g, b):
    mu = jnp.mean(x, axis=-1, keepdims=True)
    xc = x - mu
    var = jnp.mean(xc * xc, axis=-1, keepdims=True)
    return xc * lax.rsqrt(var + LN_EPS) * g + b


def _rms(x, g):
    return x * lax.rsqrt(jnp.mean(x * x, axis=-1, keepdims=True) + RMS_EPS) * g


def _dot(a, b):
    return jnp.dot(a, b, preferred_element_type=F32)


def _dot_nt(a, b):
    return lax.dot_general(a, b, (((1,), (1,)), ((), ())), preferred_element_type=F32)


def _bias_kernel(rb_ref, o_ref):
    nb = REL_BUCKETS // 2
    max_exact = nb // 2
    blk = o_ref.shape[2]
    s = lax.broadcasted_iota(I32, (blk, blk), 0)
    t = lax.broadcasted_iota(I32, (blk, blk), 1)
    for which, off in ((0, -blk), (1, 0)):
        rel = s - t + off
        ret = jnp.where(rel > 0, nb, 0)
        n = jnp.abs(rel)
        large = max_exact + (jnp.log(jnp.maximum(n, 1).astype(F32) / max_exact)
                             / math.log(REL_MAX_DIST / max_exact) * (nb - max_exact)).astype(I32)
        large = jnp.minimum(large, nb - 1)
        bucket = ret + jnp.where(n < max_exact, n, large)
        for h in range(DSA_HEADS):
            acc = jnp.zeros((blk, blk), F32)
            for b in range(REL_BUCKETS):
                acc = jnp.where(bucket == b, rb_ref[b, h], acc)
            o_ref[which, h] = (acc - rb_ref[nb - 1, h]) * _LOG2E


def _bias_tiles(rel_bias, blk):
    assert blk >= REL_MAX_DIST
    return pl.pallas_call(
        _bias_kernel,
        out_shape=jax.ShapeDtypeStruct((2, DSA_HEADS, blk, blk), F32),
        in_specs=[pl.BlockSpec(memory_space=pltpu.SMEM)],
        out_specs=pl.BlockSpec(memory_space=pltpu.VMEM),
        name="bias_tiles",
    )(rel_bias)


_C_Q, _C_KV, _C_RA, _C_RB, _C_QB, _C_KK, _C_VV, _C_QI, _C_KI, _C_END = (
    0, 256, 384, 512, 640, 1152, 1280, 1408, 1664, 1792)


def _in_proj_kernel(x_ref, pos_ref, g0_ref, b0_ref, w1_ref, qg_ref, wq_ref, wqr_ref, kg_ref, wk_ref,
                    wv_ref, invf_ref, sgn_ref,
                    qm_ref, km_ref, vm_ref, qb_ref, kk_ref, vv_ref, qi_ref, ki_ref, wi_ref):
    xn = _ln(x_ref[...], g0_ref[...], b0_ref[...])
    proj = _dot(xn.astype(BF16), w1_ref[...])
    qb_ref[...] = (proj[:, _C_QB:_C_KK] * _LOG2E).astype(BF16)
    kk_ref[...] = proj[:, _C_KK:_C_VV].astype(BF16)
    vv_ref[...] = proj[:, _C_VV:_C_QI].astype(BF16)
    qi_ref[...] = proj[:, _C_QI:_C_KI].astype(BF16)
    ki_ref[...] = proj[:, _C_KI:_C_END].astype(BF16)
    ga = proj[:, _C_RA:_C_RB]
    gb = proj[:, _C_RB:_C_QB]
    wi_ref[...] = ga[:, 0:IDX_HEADS] * (IDX_HEADS ** -0.5 * IDX_DIM ** -0.5)

    ang = pos_ref[...].astype(F32) * invf_ref[...]
    cosv = jnp.cos(ang)
    sinv = jnp.sin(ang) * sgn_ref[...]
    lane = lax.broadcasted_iota(I32, ang.shape, 1)
    rope_lane = (lane >= MLA_NOPE) & (lane < MLA_NOPE + MLA_ROPE)
    kr = jnp.where(rope_lane, ga * cosv + gb * sinv, 0.0)

    scale = (MLA_NOPE + MLA_ROPE) ** -0.5 * _LOG2E
    cos8 = jnp.concatenate([cosv * scale] * MLA_HEADS, axis=1)
    sin8 = jnp.concatenate([sinv * scale] * MLA_HEADS, axis=1)
    cqn = _rms(proj[:, _C_Q:_C_KV], qg_ref[...]).astype(BF16)
    q = _dot(cqn, wq_ref[...]) * cos8 + _dot(cqn, wqr_ref[...]) * sin8
    qm_ref[...] = q.astype(BF16)

    ckn = _rms(proj[:, _C_KV:_C_RA], kg_ref[...]).astype(BF16)
    k = _dot(ckn, wk_ref[...]) + jnp.concatenate([kr] * MLA_HEADS, axis=1)
    km_ref[...] = k.astype(BF16)
    vm_ref[...] = _dot(ckn, wv_ref[...]).astype(BF16)


def _in_proj(x2, pos2, g0, b0, w1, qg, wq, wqr, kg, wk, wv, invf, sgn, tm):
    n, d = x2.shape
    hm = MLA_HEADS * LANES
    row = lambda w: pl.BlockSpec((tm, w), lambda i: (i, 0))
    full = lambda a: pl.BlockSpec(a.shape, lambda i: (0,) * a.ndim)
    outs = [(hm, BF16), (hm, BF16), (MLA_HEADS * MLA_V, BF16), (DSA_HEADS * DSA_HEAD_DIM, BF16),
            (LANES, BF16), (LANES, BF16), (IDX_HEADS * IDX_DIM, BF16), (LANES, BF16), (IDX_HEADS, F32)]
    return pl.pallas_call(
        _in_proj_kernel,
        grid=(n // tm,),
        in_specs=[row(d), row(1)] + [full(a) for a in (g0, b0, w1, qg, wq, wqr, kg, wk, wv, invf, sgn)],
        out_specs=[row(w) for w, _ in outs],
        out_shape=[jax.ShapeDtypeStruct((n, w), dt) for w, dt in outs],
        compiler_params=pltpu.CompilerParams(dimension_semantics=("parallel",),
                                             vmem_limit_bytes=VMEM_LIMIT),
        name="in_proj",
    )(x2, pos2, g0, b0, w1, qg, wq, wqr, kg, wk, wv, invf, sgn)


def _admissible(ck, tq, k0, q0):
    s = k0 + lax.broadcasted_iota(I32, (ck, tq), 0)
    t = q0 + lax.broadcasted_iota(I32, (ck, tq), 1)
    return (s >> _CHUNK_SHIFT) <= (t >> _CHUNK_SHIFT)


def _softmax_step(s, vt, m_ref, l_ref, acc_ref, h):
    m_prev = m_ref[h]
    m_new = jnp.maximum(m_prev, jnp.max(s, axis=0, keepdims=True))
    alpha = jnp.exp2(m_prev - m_new)
    p = jnp.exp2(s - m_new)
    l_ref[h] = alpha * l_ref[h] + jnp.sum(p, axis=0, keepdims=True)
    acc_ref[h] = alpha * acc_ref[h] + _dot(vt, p.astype(BF16))
    m_ref[h] = m_new


def _finish(o_ref, l_ref, acc_ref, nh):
    ot = jnp.concatenate([acc_ref[h] / l_ref[h] for h in range(nh)], axis=0)
    o_ref[...] = ot.T.astype(BF16)


def _mla_kernel(q_ref, k_ref, vt_ref, o_ref, s_ref, m_ref, l_ref, acc_ref, *, tq, ck):
    j = pl.program_id(1)
    q0 = j * tq
    m_ref[...] = jnp.full(m_ref.shape, NEG, F32)
    l_ref[...] = jnp.zeros(l_ref.shape, F32)
    acc_ref[...] = jnp.zeros(acc_ref.shape, F32)
    nfull = (q0 + CHUNK) // ck

    def chunk(c, masked):
        k0 = pl.multiple_of(c * ck, ck)
        for h in range(MLA_HEADS):
            s_ref[h] = _dot_nt(k_ref[pl.ds(k0, ck), h * LANES:(h + 1) * LANES],
                               q_ref[:, h * LANES:(h + 1) * LANES])
        adm = _admissible(ck, tq, k0, q0) if masked else None
        for h in range(MLA_HEADS):
            s = s_ref[h]
            if masked:
                s = jnp.where(adm, s, NEG)
            _softmax_step(s, vt_ref[c, h * MLA_V:(h + 1) * MLA_V, :], m_ref, l_ref, acc_ref, h)

    def full_body(c, carry):
        chunk(c, False)
        return carry

    lax.fori_loop(0, nfull, full_body, 0)
    chunk(nfull, True)
    _finish(o_ref, l_ref, acc_ref, MLA_HEADS)


def _mla_attn(qm, km, vmt, tq, ck):
    b, t, hm = qm.shape
    dv = vmt.shape[2]
    assert ck % tq == 0 and tq > CHUNK and tq % CHUNK == 0 and t % ck == 0
    return pl.pallas_call(
        functools.partial(_mla_kernel, tq=tq, ck=ck),
        grid=(b, t // tq),
        in_specs=[pl.BlockSpec((None, tq, hm), lambda i, j: (i, j, 0)),
                  pl.BlockSpec((None, t, hm), lambda i, j: (i, 0, 0)),
                  pl.BlockSpec((None, t // ck, dv, ck), lambda i, j: (i, 0, 0, 0))],
        out_specs=pl.BlockSpec((None, tq, dv), lambda i, j: (i, j, 0)),
        out_shape=jax.ShapeDtypeStruct((b, t, dv), BF16),
        scratch_shapes=[pltpu.VMEM((MLA_HEADS, ck, tq), F32),
                        pltpu.VMEM((MLA_HEADS, 1, tq), F32), pltpu.VMEM((MLA_HEADS, 1, tq), F32),
                        pltpu.VMEM((MLA_HEADS, MLA_V, tq), F32)],
        compiler_params=pltpu.CompilerParams(dimension_semantics=("parallel", "arbitrary"),
                                             vmem_limit_bytes=VMEM_LIMIT),
        name="mla_attn",
    )(qm, km, vmt)


def _dsa_kernel(qb_ref, qi_ref, wt_ref, kk_ref, vt_ref, ki_ref, bt_ref, o_ref,
                qs_ref, qim_ref, keys_ref, s_ref, thr_ref, cut_ref, m_ref, l_ref, acc_ref, *, tq, topk):
    nh = DSA_HEADS
    ck = tq
    seq = kk_ref.shape[0]
    j = pl.program_id(1)
    q0 = j * tq
    nck = j + 1
    lane = lax.broadcasted_iota(I32, (tq, LANES), 1)

    for h in range(nh):
        pair = qb_ref[:, (h // 2) * LANES:(h // 2 + 1) * LANES].astype(F32)
        in_head = (lane >= (h % 2) * DSA_HEAD_DIM) & (lane < (h % 2 + 1) * DSA_HEAD_DIM)
        qs_ref[h * tq:(h + 1) * tq, :] = jnp.where(in_head, pair, 0.0).astype(BF16)
        grp = qi_ref[:, (h // 4) * LANES:(h // 4 + 1) * LANES].astype(F32)
        in_head = (lane >= (h % 4) * IDX_DIM) & (lane < (h % 4 + 1) * IDX_DIM)
        qim_ref[h * tq:(h + 1) * tq, :] = jnp.where(in_head, grp, 0.0).astype(BF16)

    w = wt_ref[...]

    def idx_chunk(c, masked):
        k0 = pl.multiple_of(c * ck, ck)
        kc = ki_ref[pl.ds(k0, ck), :]
        for h in range(nh):
            s_ref[h] = _dot_nt(kc, qim_ref[h * tq:(h + 1) * tq, :])
        score = jnp.zeros((ck, tq), F32)
        for h in range(nh):
            score = score + w[h:h + 1, :] * jnp.maximum(s_ref[h], 0.0)
        if masked:
            score = jnp.where(_admissible(ck, tq, k0, q0), score, NEG)
        score = jnp.where(score == 0.0, 0.0, score)
        bits = lax.bitcast_convert_type(score, I32)
        keys_ref[c] = bits ^ ((bits >> 31) & 0x7FFFFFFF)

    def idx_body(c, carry):
        idx_chunk(c, False)
        return carry

    lax.fori_loop(0, nck - 1, idx_body, 0)
    idx_chunk(nck - 1, True)

    thr_ref[...] = jnp.full(thr_ref.shape, _KEY_NEG + 1, I32)
    cut_ref[...] = jnp.full(cut_ref.shape, seq, I32)

    def count_keys(pred):
        def body(c, acc):
            part = jnp.where(pred(c), 1.0, 0.0).reshape(ck // 8, 8, tq)
            while part.shape[0] > 1:
                half = part.shape[0] // 2
                part = part[:half] + part[half:]
            return acc + part[0]
        acc = lax.fori_loop(0, nck, body, jnp.zeros((8, tq), F32))
        return jnp.sum(acc, axis=0, keepdims=True)

    @pl.when(q0 + tq > topk)
    def _search():
        def step(i, carry):
            thr, cge = carry
            cand = thr + lax.shift_left(jnp.int32(1), 31 - i)
            cnt = count_keys(lambda c: keys_ref[c] >= cand)
            ok = cnt >= topk
            return jnp.where(ok, cand, thr), jnp.where(ok, cnt, cge)

        thr0 = jnp.full((1, tq), _INT_MIN, I32)
        cge0 = jnp.full((1, tq), 1.0, F32) * (nck * ck).astype(F32)
        thr, cge = lax.fori_loop(0, 32, step, (thr0, cge0))
        thr_ref[...] = thr
        excess = cge - float(topk)

        @pl.when(jnp.max(excess) > 0.0)
        def _ties():
            nbits = max(1, int(seq - 1).bit_length())

            def tstep(i, cut):
                cand = cut + lax.shift_left(jnp.int32(1), nbits - 1 - i)

                def pred(c):
                    idx = c * ck + lax.broadcasted_iota(I32, (ck, tq), 0)
                    return (keys_ref[c] == thr) & (idx >= cand)

                return jnp.where(count_keys(pred) >= excess, cand, cut)

            cut = lax.fori_loop(0, nbits, tstep, jnp.zeros((1, tq), I32))
            cut_ref[...] = jnp.where(excess > 0.0, cut, seq)

    m_ref[...] = jnp.full(m_ref.shape, NEG, F32)
    l_ref[...] = jnp.zeros(l_ref.shape, F32)
    acc_ref[...] = jnp.zeros(acc_ref.shape, F32)
    thr = thr_ref[...]
    cut = cut_ref[...]

    def att_chunk(c, bias_slab, own):
        k0 = pl.multiple_of(c * ck, ck)
        kc = kk_ref[pl.ds(k0, ck), :]
        for h in range(nh):
            s_ref[h] = _dot_nt(kc, qs_ref[h * tq:(h + 1) * tq, :])
        key = keys_ref[c]
        idx = k0 + lax.broadcasted_iota(I32, (ck, tq), 0)
        sel = (key > thr) | ((key == thr) & (idx < cut))
        if own:
            sel = sel & _admissible(ck, tq, k0, q0)
        vt = vt_ref[c]
        for h in range(nh):
            s = s_ref[h]
            if bias_slab is not None:
                s = s + bt_ref[bias_slab, h]
            _softmax_step(jnp.where(sel, s, NEG), vt, m_ref, l_ref, acc_ref, h)

    def far_body(c, carry):
        att_chunk(c, None, False)
        return carry

    lax.fori_loop(0, j - 1, far_body, 0)

    @pl.when(j > 0)
    def _prev():
        att_chunk(j - 1, 0, False)

    att_chunk(j, 1, True)
    _finish(o_ref, l_ref, acc_ref, nh)


def _dsa_attn(qb, qi, wt, kk, vt, ki, bt, tq, topk):
    b, t, _ = qb.shape
    ck = tq
    assert tq == bt.shape[2] and tq % LANES == 0 and t % tq == 0
    nh = DSA_HEADS
    blk = lambda a: pl.BlockSpec((None, tq, a.shape[2]), lambda i, j: (i, j, 0))
    seqb = lambda a: pl.BlockSpec((None, t, a.shape[2]), lambda i, j: (i, 0, 0))
    return pl.pallas_call(
        functools.partial(_dsa_kernel, tq=tq, topk=topk),
        grid=(b, t // tq),
        in_specs=[blk(qb), blk(qi), pl.BlockSpec((None, nh, tq), lambda i, j: (i, 0, j)), seqb(kk),
                  pl.BlockSpec((None, t // ck, DSA_HEAD_DIM, ck), lambda i, j: (i, 0, 0, 0)), seqb(ki),
                  pl.BlockSpec(bt.shape, lambda i, j: (0, 0, 0, 0))],
        out_specs=blk(qb),
        out_shape=jax.ShapeDtypeStruct(qb.shape, BF16),
        scratch_shapes=[pltpu.VMEM((nh * tq, LANES), BF16), pltpu.VMEM((nh * tq, LANES), BF16),
                        pltpu.VMEM((t // ck, ck, tq), I32), pltpu.VMEM((nh, ck, tq), F32),
                        pltpu.VMEM((1, tq), I32),
                        pltpu.VMEM((1, tq), I32), pltpu.VMEM((nh, 1, tq), F32),
                        pltpu.VMEM((nh, 1, tq), F32), pltpu.VMEM((nh, DSA_HEAD_DIM, tq), F32)],
        compiler_params=pltpu.CompilerParams(dimension_semantics=("parallel", "arbitrary"),
                                             vmem_limit_bytes=VMEM_LIMIT),
        name="dsa_attn",
    )(qb, qi, wt, kk, vt, ki, bt)


_R_GRP = 0
_R_EXP = N_GROUPS


def _post_kernel(x_ref, oa_ref, ob_ref, g0_ref, b0_ref, wg_ref, bg_ref, wua_ref, wub_ref, wo_ref,
                 g1_ref, b1_ref, wr_ref, br_ref, hx_ref, gsel_ref, *, alpha):
    d = x_ref.shape[1]
    xn = _ln(x_ref[...], g0_ref[...], b0_ref[...])
    z = _dot(xn.astype(BF16), wg_ref[...]) + bg_ref[...]
    gates = 1.0 / (1.0 + jnp.exp(-z))
    mix = gates[:, :d] * _dot(oa_ref[...], wua_ref[...]) + gates[:, d:] * _dot(ob_ref[...], wub_ref[...])
    mixed = _dot(mix.astype(BF16), wo_ref[...])
    h = _ln(alpha * xn + mixed, g1_ref[...], b1_ref[...])
    hx_ref[:, :d] = h

    logits = _dot(h.astype(BF16), wr_ref[...]) + br_ref[...]
    lane = lax.broadcasted_iota(I32, logits.shape, 1).astype(F32)
    gmask = lane < N_GROUPS
    gl = jnp.where(gmask, logits, -jnp.inf)
    gmax = jnp.max(gl, axis=1, keepdims=True)
    g_sel = jnp.min(jnp.where(gl == gmax, lane, float(LANES)), axis=1, keepdims=True)
    g_w = 1.0 / jnp.sum(jnp.where(gmask, jnp.exp(gl - gmax), 0.0), axis=1, keepdims=True)
    e_lo = _R_EXP + g_sel * EXPERTS_PER_GROUP
    el = jnp.where((lane >= e_lo) & (lane < e_lo + EXPERTS_PER_GROUP), logits, -jnp.inf)
    m1 = jnp.max(el, axis=1, keepdims=True)
    i1 = jnp.min(jnp.where(el == m1, lane, float(LANES)), axis=1, keepdims=True)
    el2 = jnp.where(lane == i1, -jnp.inf, el)
    m2 = jnp.max(el2, axis=1, keepdims=True)
    i2 = jnp.min(jnp.where(el2 == m2, lane, float(LANES)), axis=1, keepdims=True)
    e21 = jnp.exp(m2 - m1)
    w1 = g_w / (1.0 + e21)
    w2 = g_w * e21 / (1.0 + e21)
    hx_ref[:, d:] = jnp.where(lane == i1, w1, 0.0) + jnp.where(lane == i2, w2, 0.0)
    gsel_ref[...] = g_sel.astype(I32)


def _post_attn(x2, oa, ob, g0, b0, wg, bg, wua, wub, wo, g1, b1, wr, br, alpha, tm):
    n, d = x2.shape
    row = lambda w: pl.BlockSpec((tm, w), lambda i: (i, 0))
    full = lambda a: pl.BlockSpec(a.shape, lambda i: (0,) * a.ndim)
    return pl.pallas_call(
        functools.partial(_post_kernel, alpha=alpha),
        grid=(n // tm,),
        in_specs=[row(d), row(oa.shape[1]), row(ob.shape[1])]
                 + [full(a) for a in (g0, b0, wg, bg, wua, wub, wo, g1, b1, wr, br)],
        out_specs=[row(d + LANES), row(1)],
        out_shape=[jax.ShapeDtypeStruct((n, d + LANES), F32), jax.ShapeDtypeStruct((n, 1), I32)],
        compiler_params=pltpu.CompilerParams(dimension_semantics=("parallel",),
                                             vmem_limit_bytes=VMEM_LIMIT),
        name="post_attn",
    )(x2, oa, ob, g0, b0, wg, bg, wua, wub, wo, g1, b1, wr, br)


def _moe_kernel(tg_ref, nv_ref, src_ref, srcn_ref, hx_hbm, weg_ref, weu_ref, wed_ref, g2_ref, b2_ref,
                out_hbm, gbuf, obuf, gsem, ssem, *, alpha, tm, d):
    k = pl.program_id(0)
    nk = pl.num_programs(0)
    slot = k % 2
    nrun = tm // LANES

    def gather_copy(tok, r, s):
        return pltpu.make_async_copy(hx_hbm.at[pl.ds(tok, 1)], gbuf.at[s, pl.ds(r, 1)], gsem.at[s])

    def scatter_copy(tok, r, s):
        return pltpu.make_async_copy(obuf.at[s, pl.ds(r, 1)], out_hbm.at[pl.ds(tok, 1)], ssem.at[s])

    def start_gather(tile_src_ref, s):
        for g in range(nrun):
            for c in range(LANES):
                gather_copy(tile_src_ref[g, c], g * LANES + c, s).start()

    def start_scatter(n_rows, s):
        for g in range(nrun):
            @pl.when(n_rows >= (g + 1) * LANES)
            def _run():
                for c in range(LANES):
                    scatter_copy(src_ref[g, c], g * LANES + c, s).start()

        def row(r, carry):
            scatter_copy(src_ref[r // LANES, r % LANES], r, s).start()
            return carry

        lax.fori_loop(n_rows // LANES * LANES, n_rows, row, 0)

    def by_slot(fn, s_of_slot):
        for par in (0, 1):
            pl.when(slot == par)(functools.partial(fn, s_of_slot(par)))

    def wait_scatter(n_rows, s):
        def waiter(rows):
            def body(i, carry):
                pltpu.make_async_copy(obuf.at[s, pl.ds(0, rows)], out_hbm.at[pl.ds(0, rows)], ssem.at[s]).wait()
                return carry
            return body

        lax.fori_loop(0, n_rows // LANES, waiter(LANES), 0)
        lax.fori_loop(0, n_rows % LANES // _SUBLANES, waiter(_SUBLANES), 0)
        lax.fori_loop(0, n_rows % _SUBLANES, waiter(1), 0)

    @pl.when(k == 0)
    def _first():
        start_gather(src_ref, 0)

    pltpu.make_async_copy(hx_hbm.at[pl.ds(0, tm)], gbuf.at[slot], gsem.at[slot]).wait()

    @pl.when(k + 1 < nk)
    def _prefetch():
        by_slot(lambda s: start_gather(srcn_ref, s), lambda par: 1 - par)

    @pl.when(k >= 2)
    def _free_obuf():
        wait_scatter(nv_ref[k - 2], slot)

    x = gbuf[slot]
    h = x[:, :d]
    comb = x[:, d:]
    hb = h.astype(BF16)
    lane = lax.broadcasted_iota(I32, comb.shape, 1)
    e0 = _R_EXP + tg_ref[k] * EXPERTS_PER_GROUP
    acc = jnp.zeros((tm, d), F32)
    for e in range(EXPERTS_PER_GROUP):
        a = _dot(hb, weg_ref[e])
        u = _dot(hb, weu_ref[e])
        cw = jnp.sum(jnp.where(lane == e0 + e, comb, 0.0), axis=1, keepdims=True)
        hid = (a / (1.0 + jnp.exp(-a))) * u * cw
        acc = acc + _dot(hid.astype(BF16), wed_ref[e])
    obuf[slot] = _ln(alpha * h + acc, g2_ref[...], b2_ref[...])

    nv = nv_ref[k]
    by_slot(lambda s: start_scatter(nv, s), lambda par: par)

    @pl.when(k == nk - 1)
    def _drain():
        wait_scatter(nv, slot)

        @pl.when(k >= 1)
        def _():
            wait_scatter(nv_ref[k - 1], 1 - slot)


def _moe(hx, tile_group, n_valid, src, weg, weu, wed, g2, b2, alpha, tm, n):
    d = hx.shape[1] - LANES
    nk = src.shape[0]
    ne, _, f = weg.shape
    gsz = EXPERTS_PER_GROUP
    grid_spec = pltpu.PrefetchScalarGridSpec(
        num_scalar_prefetch=2,
        grid=(nk,),
        in_specs=[pl.BlockSpec((None, tm // LANES, LANES), lambda k, tg, nv: (k, 0, 0),
                               memory_space=pltpu.SMEM),
                  pl.BlockSpec((None, tm // LANES, LANES),
                               lambda k, tg, nv: (jnp.minimum(k + 1, nk - 1), 0, 0), memory_space=pltpu.SMEM),
                  pl.BlockSpec(memory_space=pl.ANY),
                  pl.BlockSpec((gsz, d, f), lambda k, tg, nv: (tg[k], 0, 0)),
                  pl.BlockSpec((gsz, d, f), lambda k, tg, nv: (tg[k], 0, 0)),
                  pl.BlockSpec((gsz, f, d), lambda k, tg, nv: (tg[k], 0, 0)),
                  pl.BlockSpec(g2.shape, lambda k, tg, nv: (0, 0)),
                  pl.BlockSpec(b2.shape, lambda k, tg, nv: (0, 0))],
        out_specs=pl.BlockSpec(memory_space=pl.ANY),
        scratch_shapes=[pltpu.VMEM((2, tm, d + LANES), F32), pltpu.VMEM((2, tm, d), F32),
                        pltpu.SemaphoreType.DMA((2,)), pltpu.SemaphoreType.DMA((2,))])
    return pl.pallas_call(
        functools.partial(_moe_kernel, alpha=alpha, tm=tm, d=d),
        grid_spec=grid_spec,
        out_shape=jax.ShapeDtypeStruct((n, d), F32),
        compiler_params=pltpu.CompilerParams(dimension_semantics=("arbitrary",),
                                             vmem_limit_bytes=VMEM_LIMIT),
        name="moe",
    )(tile_group, n_valid, src, src, hx, weg, weu, wed, g2, b2)


def _group_tiles(gsel, tm):
    n = gsel.shape[0]
    n_tiles = n // tm + N_GROUPS
    onehot = (gsel[:, None] == jnp.arange(N_GROUPS, dtype=I32)[None, :]).astype(I32)
    csum = jnp.cumsum(onehot, axis=0)
    rank = jnp.sum(csum * onehot, axis=1) - 1
    counts = csum[-1]
    padded = (counts + tm - 1) // tm * tm
    ends = jnp.cumsum(padded)
    offs = ends - padded
    dest = offs[gsel] + rank
    src = jnp.zeros((n_tiles * tm,), I32).at[dest].set(jnp.arange(n, dtype=I32))
    tile_start = jnp.arange(n_tiles, dtype=I32) * tm
    tile_group = jnp.minimum(jnp.sum((tile_start[:, None] >= ends[None, :]).astype(I32), axis=1), N_GROUPS - 1)
    n_valid = jnp.clip(offs[tile_group] + counts[tile_group] - tile_start, 0, tm)
    return tile_group.astype(I32), n_valid.astype(I32), src.reshape(n_tiles, tm // LANES, LANES)


def _layout_weights(w_in, w_uq, w_uk, w_uv):
    d = w_in.shape[0]
    sizes = (MLA_Q_RANK, MLA_KV_RANK, MLA_ROPE, DSA_HEADS * DSA_HEAD_DIM, DSA_HEAD_DIM, DSA_HEAD_DIM,
             IDX_HEADS * IDX_DIM, IDX_DIM, IDX_HEADS)
    offs = np.cumsum((0,) + sizes)
    c_q, c_kv, k_rope, q_b, k_b, v_b, q_idx, k_idx, w_idx = (
        w_in[:, int(offs[i]):int(offs[i + 1])] for i in range(len(sizes)))
    half = MLA_ROPE // 2
    z = lambda w: jnp.zeros((d, w), w_in.dtype)
    pad = LANES - MLA_NOPE - MLA_ROPE
    grp_a = jnp.concatenate([w_idx, z(MLA_NOPE - IDX_HEADS), k_rope, z(pad)], axis=1)
    grp_b = jnp.concatenate([z(MLA_NOPE), k_rope[:, half:], k_rope[:, :half], z(pad)], axis=1)
    w1 = jnp.concatenate([c_q, c_kv, grp_a, grp_b, q_b * DSA_HEAD_DIM ** -0.5, k_b, k_b, v_b, v_b, q_idx,
                          k_idx, k_idx, k_idx, k_idx], axis=1)
    assert w1.shape[1] == _C_END

    dk = MLA_NOPE + MLA_ROPE
    rq = w_uq.shape[0]
    zq = lambda w: jnp.zeros((rq, w), w_uq.dtype)
    wq_parts, wqr_parts, wk_parts = [], [], []
    for h in range(MLA_HEADS):
        nope = w_uq[:, h * dk:h * dk + MLA_NOPE]
        rope = w_uq[:, h * dk + MLA_NOPE:(h + 1) * dk]
        wq_parts += [nope, rope, zq(pad)]
        wqr_parts += [zq(MLA_NOPE), rope[:, half:], rope[:, :half], zq(pad)]
        wk_parts += [w_uk[:, h * MLA_NOPE:(h + 1) * MLA_NOPE],
                     jnp.zeros((w_uk.shape[0], LANES - MLA_NOPE), w_uk.dtype)]
    wq = jnp.concatenate(wq_parts, axis=1)
    wqr = jnp.concatenate(wqr_parts, axis=1)
    wk = jnp.concatenate(wk_parts, axis=1)

    inv = ROPE_THETA ** (-np.arange(half, dtype=np.float32) / half)
    invf = np.zeros((1, LANES), np.float32)
    invf[0, MLA_NOPE:MLA_NOPE + half] = inv
    invf[0, MLA_NOPE + half:MLA_NOPE + MLA_ROPE] = inv
    sgn = np.zeros((1, LANES), np.float32)
    sgn[0, MLA_NOPE:MLA_NOPE + half] = -1.0
    sgn[0, MLA_NOPE + half:MLA_NOPE + MLA_ROPE] = 1.0
    return (w1.astype(BF16), wq.astype(BF16), wqr.astype(BF16), wk.astype(BF16), w_uv.astype(BF16),
            jnp.asarray(invf), jnp.asarray(sgn))


def kernel(x, positions, ln0_g, ln0_b, w_in, q_norm_g, w_uq, kv_norm_g, w_uk, w_uv, rel_bias, w_up_a, w_up_b,
           w_gate, b_gate, w_o, ln1_g, ln1_b, w_grp, b_grp, w_rt, b_rt, w_exp_gate, w_exp_up, w_exp_down,
           ln2_g, ln2_b):
    bsz, seq, d = x.shape
    depth = w_in.shape[0]
    assert depth == 1
    alpha = (2.0 * depth) ** 0.25
    n = bsz * seq
    topk = min(IDX_TOPK_MAX, seq // 4)
    r2 = lambda v: v.reshape(1, -1)

    x2 = x.reshape(n, d)
    pos2 = positions.reshape(n, 1)
    w1, wq, wqr, wk, wv, invf, sgn = _layout_weights(w_in[0], w_uq[0], w_uk[0], w_uv[0])

    blk = 256
    bt = _bias_tiles(rel_bias, blk)
    qm, km, vm, qb, kk, vv, qi, ki, wi = _in_proj(
        x2, pos2, r2(ln0_g), r2(ln0_b), w1, r2(q_norm_g[0]), wq, wqr, r2(kv_norm_g[0]), wk, wv, invf, sgn,
        tm=512)
    b3 = lambda a: a.reshape(bsz, seq, a.shape[1])
    vchunks = lambda a, dv: a[:, :dv].reshape(bsz, seq // blk, blk, dv).transpose(0, 1, 3, 2)
    o_a = _mla_attn(b3(qm), b3(km), vchunks(vm, MLA_HEADS * MLA_V), tq=blk, ck=blk)
    o_b = _dsa_attn(b3(qb), b3(qi), b3(wi).transpose(0, 2, 1), b3(kk), vchunks(vv, DSA_HEAD_DIM), b3(ki), bt,
                    tq=blk, topk=topk)

    wr = jnp.concatenate([w_grp[0], w_rt[0], jnp.zeros((d, LANES - N_GROUPS - N_EXPERTS), F32)], axis=1)
    br = jnp.concatenate([b_grp[0], b_rt[0], jnp.zeros((LANES - N_GROUPS - N_EXPERTS,), F32)])
    hx, gsel = _post_attn(
        x2, o_a.reshape(n, -1), o_b.reshape(n, -1), r2(ln0_g), r2(ln0_b), w_gate[0].astype(BF16), r2(b_gate[0]),
        w_up_a[0].astype(BF16), w_up_b[0].astype(BF16), w_o[0].astype(BF16), r2(ln1_g[0]), r2(ln1_b[0]),
        wr.astype(BF16), r2(br), alpha, tm=512)
    tm_moe = 512
    tile_group, n_valid, src = _group_tiles(gsel[:, 0], tm_moe)
    out = _moe(hx, tile_group, n_valid, src, w_exp_gate[0].astype(BF16), w_exp_up[0].astype(BF16),
               w_exp_down[0].astype(BF16), r2(ln2_g[0]), r2(ln2_b[0]), alpha, tm_moe, n)
    return out.reshape(bsz, seq, d)
```

```python
import functools
import math

import numpy as np
import jax
import jax.numpy as jnp
from jax import lax
from jax.experimental import pallas as pl
from jax.experimental.pallas import tpu as pltpu

F32 = jnp.float32
BF16 = jnp.bfloat16
I32 = jnp.int32

CHUNK = 64
MLA_HEADS = 8
MLA_Q_RANK = 256
MLA_KV_RANK = 128
MLA_NOPE = 64
MLA_ROPE = 32
MLA_V = 64
ROPE_THETA = 10000.0
DSA_HEADS = 8
DSA_HEAD_DIM = 64
IDX_HEADS = 8
IDX_DIM = 32
IDX_TOPK_MAX = 256
REL_BUCKETS = 32
REL_MAX_DIST = 128
N_GROUPS = 4
EXPERTS_PER_GROUP = 8
N_EXPERTS = N_GROUPS * EXPERTS_PER_GROUP
EXPERT_HIDDEN = 256
LN_EPS = 1e-5
RMS_EPS = 1e-6
NEG = -1e30

LANES = 128
_SUBLANES = 8
_VROWS = 80
VMEM_LIMIT = 48 * 1024 * 1024

_KEY_NEG = int(np.array(NEG, np.float32).view(np.int32)) ^ 0x7FFFFFFF
_INT_MIN = -(2 ** 31)
_LOG2E = math.log2(math.e)
_CHUNK_SHIFT = CHUNK.bit_length() - 1
assert 1 << _CHUNK_SHIFT == CHUNK


def _ln(x, g, b):
    mu = jnp.mean(x, axis=-1, keepdims=True)
    xc = x - mu
    var = jnp.mean(xc * xc, axis=-1, keepdims=True)
    return xc * lax.rsqrt(var + LN_EPS) * g + b


def _rms(x, g):
    return x * lax.rsqrt(jnp.mean(x * x, axis=-1, keepdims=True) + RMS_EPS) * g


def _dot(a, b):
    return jnp.dot(a, b, preferred_element_type=F32)


def _dot_nt(a, b):
    return lax.dot_general(a, b, (((1,), (1,)), ((), ())), preferred_element_type=F32)


def _bias_kernel(rb_ref, o_ref):
    nb = REL_BUCKETS // 2
    max_exact = nb // 2
    blk = o_ref.shape[2]
    s = lax.broadcasted_iota(I32, (blk, blk), 0)
    t = lax.broadcasted_iota(I32, (blk, blk), 1)
    for which, off in ((0, -blk), (1, 0)):
        rel = s - t + off
        ret = jnp.where(rel > 0, nb, 0)
        n = jnp.abs(rel)
        large = max_exact + (jnp.log(jnp.maximum(n, 1).astype(F32) / max_exact)
                             / math.log(REL_MAX_DIST / max_exact) * (nb - max_exact)).astype(I32)
        large = jnp.minimum(large, nb - 1)
        bucket = ret + jnp.where(n < max_exact, n, large)
        for h in range(DSA_HEADS):
            acc = jnp.zeros((blk, blk), F32)
            for b in range(REL_BUCKETS):
                acc = jnp.where(bucket == b, rb_ref[b, h], acc)
            o_ref[which, h] = (acc - rb_ref[nb - 1, h]) * _LOG2E


def _bias_tiles(rel_bias, blk):
    assert blk >= REL_MAX_DIST
    return pl.pallas_call(
        _bias_kernel,
        out_shape=jax.ShapeDtypeStruct((2, DSA_HEADS, blk, blk), F32),
        in_specs=[pl.BlockSpec(memory_space=pltpu.SMEM)],
        out_specs=pl.BlockSpec(memory_space=pltpu.VMEM),
        name="bias_tiles",
    )(rel_bias)


_C_Q, _C_KV, _C_RA, _C_RB, _C_QB, _C_KK, _C_VV, _C_QI, _C_KI, _C_END = (
    0, 256, 384, 512, 640, 1152, 1280, 1408, 1664, 1792)


def _in_proj_kernel(x_ref, pos_ref, g0_ref, b0_ref, w1_ref, qg_ref, wq_ref, wqr_ref, kg_ref, wk_ref,
                    wv_ref, invf_ref, sgn_ref,
                    qm_ref, km_ref, vm_ref, qb_ref, kk_ref, vv_ref, qi_ref, ki_ref, wi_ref):
    xn = _ln(x_ref[...], g0_ref[...], b0_ref[...])
    proj = _dot(xn.astype(BF16), w1_ref[...])
    qb_ref[...] = (proj[:, _C_QB:_C_KK] * _LOG2E).astype(BF16)
    kk_ref[...] = proj[:, _C_KK:_C_VV].astype(BF16)
    vv_ref[...] = proj[:, _C_VV:_C_QI].astype(BF16)
    qi_ref[...] = proj[:, _C_QI:_C_KI].astype(BF16)
    ki_ref[...] = proj[:, _C_KI:_C_END].astype(BF16)
    ga = proj[:, _C_RA:_C_RB]
    gb = proj[:, _C_RB:_C_QB]
    wi_ref[...] = ga[:, 0:IDX_HEADS] * (IDX_HEADS ** -0.5 * IDX_DIM ** -0.5)

    ang = pos_ref[...].astype(F32) * invf_ref[...]
    cosv = jnp.cos(ang)
    sinv = jnp.sin(ang) * sgn_ref[...]
    lane = lax.broadcasted_iota(I32, ang.shape, 1)
    rope_lane = (lane >= MLA_NOPE) & (lane < MLA_NOPE + MLA_ROPE)
    kr = jnp.where(rope_lane, ga * cosv + gb * sinv, 0.0)

    scale = (MLA_NOPE + MLA_ROPE) ** -0.5 * _LOG2E
    cos8 = jnp.concatenate([cosv * scale] * MLA_HEADS, axis=1)
    sin8 = jnp.concatenate([sinv * scale] * MLA_HEADS, axis=1)
    cqn = _rms(proj[:, _C_Q:_C_KV], qg_ref[...]).astype(BF16)
    q = _dot(cqn, wq_ref[...]) * cos8 + _dot(cqn, wqr_ref[...]) * sin8
    qm_ref[...] = q.astype(BF16)

    ckn = _rms(proj[:, _C_KV:_C_RA], kg_ref[...]).astype(BF16)
    k = _dot(ckn, wk_ref[...]) + jnp.concatenate([kr] * MLA_HEADS, axis=1)
    km_ref[...] = k.astype(BF16)
    vm_ref[...] = _dot(ckn, wv_ref[...]).astype(BF16)


def _in_proj(x2, pos2, g0, b0, w1, qg, wq, wqr, kg, wk, wv, invf, sgn, tm):
    n, d = x2.shape
    hm = MLA_HEADS * LANES
    row = lambda w: pl.BlockSpec((tm, w), lambda i: (i, 0))
    full = lambda a: pl.BlockSpec(a.shape, lambda i: (0,) * a.ndim)
    outs = [(hm, BF16), (hm, BF16), (MLA_HEADS * MLA_V, BF16), (DSA_HEADS * DSA_HEAD_DIM, BF16),
            (LANES, BF16), (LANES, BF16), (IDX_HEADS * IDX_DIM, BF16), (LANES, BF16), (IDX_HEADS, F32)]
    return pl.pallas_call(
        _in_proj_kernel,
        grid=(n // tm,),
        in_specs=[row(d), row(1)] + [full(a) for a in (g0, b0, w1, qg, wq, wqr, kg, wk, wv, invf, sgn)],
        out_specs=[row(w) for w, _ in outs],
        out_shape=[jax.ShapeDtypeStruct((n, w), dt) for w, dt in outs],
        compiler_params=pltpu.CompilerParams(dimension_semantics=("parallel",),
                                             vmem_limit_bytes=VMEM_LIMIT),
        name="in_proj",
    )(x2, pos2, g0, b0, w1, qg, wq, wqr, kg, wk, wv, invf, sgn)


def _admissible(ck, tq, k0, q0):
    s = k0 + lax.broadcasted_iota(I32, (ck, tq), 0)
    t = q0 + lax.broadcasted_iota(I32, (ck, tq), 1)
    return (s >> _CHUNK_SHIFT) <= (t >> _CHUNK_SHIFT)


def _softmax_step(s, vt, m_ref, acc_ref, h):
    sb = s.astype(BF16)
    m_prev = m_ref[h]
    m_new = jnp.maximum(m_prev, jnp.max(sb, axis=0, keepdims=True).astype(F32))
    alpha = jnp.exp2(m_prev - m_new)
    p = jnp.exp2(sb - m_new.astype(BF16))
    acc_ref[h] = alpha * acc_ref[h] + _dot(vt, p)
    m_ref[h] = m_new


def _scores_first(s_ref, nh, produce, consume):
    for h in range(nh):
        s_ref[h] = produce(h)
    for h in range(nh):
        consume(h, s_ref[h])


def _finish(o_ref, acc_ref, nh, dv):
    ot = jnp.concatenate([acc_ref[h, :dv, :] / acc_ref[h, dv:dv + 1, :] for h in range(nh)], axis=0)
    o_ref[...] = ot.T.astype(BF16)


def _value_slabs(v, bsz, seq, blk, nh, dv):
    nc = seq // blk
    vt = v.reshape(bsz, nc, blk, nh, dv).transpose(0, 1, 3, 4, 2)
    ones = jnp.ones((bsz, nc, nh, 1, blk), v.dtype)
    zeros = jnp.zeros((bsz, nc, nh, _VROWS - dv - 1, blk), v.dtype)
    return jnp.concatenate([vt, ones, zeros], axis=3).reshape(bsz, nc, nh * _VROWS, blk)


def _mla_kernel(q_ref, k_ref, vt_ref, o_ref, s_ref, m_ref, acc_ref, *, tq, ck):
    j = pl.program_id(1)
    q0 = j * tq
    m_ref[...] = jnp.full(m_ref.shape, NEG, F32)
    acc_ref[...] = jnp.zeros(acc_ref.shape, F32)
    nfull = (q0 + CHUNK) // ck

    def chunk(c, masked):
        k0 = pl.multiple_of(c * ck, ck)
        adm = _admissible(ck, tq, k0, q0) if masked else None

        def score(h):
            return _dot_nt(k_ref[pl.ds(k0, ck), h * LANES:(h + 1) * LANES], q_ref[:, h * LANES:(h + 1) * LANES])

        def use(h, s):
            if masked:
                s = jnp.where(adm, s, NEG)
            _softmax_step(s, vt_ref[c, h * _VROWS:(h + 1) * _VROWS, :], m_ref, acc_ref, h)

        _scores_first(s_ref, MLA_HEADS, score, use)

    def full_body(c, carry):
        chunk(c, False)
        return carry

    lax.fori_loop(0, nfull, full_body, 0)
    chunk(nfull, True)
    _finish(o_ref, acc_ref, MLA_HEADS, MLA_V)


def _mla_attn(qm, km, vmt, tq, ck):
    b, t, hm = qm.shape
    dv = MLA_HEADS * MLA_V
    assert ck % tq == 0 and tq > CHUNK and tq % CHUNK == 0 and t % ck == 0
    return pl.pallas_call(
        functools.partial(_mla_kernel, tq=tq, ck=ck),
        grid=(b, t // tq),
        in_specs=[pl.BlockSpec((None, tq, hm), lambda i, j: (i, j, 0)),
                  pl.BlockSpec((None, t, hm), lambda i, j: (i, 0, 0)),
                  pl.BlockSpec((None, t // ck, MLA_HEADS * _VROWS, ck), lambda i, j: (i, 0, 0, 0))],
        out_specs=pl.BlockSpec((None, tq, dv), lambda i, j: (i, j, 0)),
        out_shape=jax.ShapeDtypeStruct((b, t, dv), BF16),
        scratch_shapes=[pltpu.VMEM((MLA_HEADS, ck, tq), F32), pltpu.VMEM((MLA_HEADS, 1, tq), F32),
                        pltpu.VMEM((MLA_HEADS, _VROWS, tq), F32)],
        compiler_params=pltpu.CompilerParams(dimension_semantics=("parallel", "arbitrary"),
                                             vmem_limit_bytes=VMEM_LIMIT),
        name="mla_attn",
    )(qm, km, vmt)


def _dsa_kernel(qb_ref, qi_ref, wt_ref, kk_ref, vt_ref, ki_ref, bt_ref, o_ref,
                qs_ref, qim_ref, keys_ref, s_ref, thr_ref, cut_ref, m_ref, acc_ref, *, tq, topk):
    nh = DSA_HEADS
    ck = tq
    seq = kk_ref.shape[0]
    j = pl.program_id(1)
    q0 = j * tq
    nck = j + 1
    lane = lax.broadcasted_iota(I32, (tq, LANES), 1)

    for h in range(nh):
        pair = qb_ref[:, (h // 2) * LANES:(h // 2 + 1) * LANES].astype(F32)
        in_head = (lane >= (h % 2) * DSA_HEAD_DIM) & (lane < (h % 2 + 1) * DSA_HEAD_DIM)
        qs_ref[h * tq:(h + 1) * tq, :] = jnp.where(in_head, pair, 0.0).astype(BF16)
        grp = qi_ref[:, (h // 4) * LANES:(h // 4 + 1) * LANES].astype(F32)
        in_head = (lane >= (h % 4) * IDX_DIM) & (lane < (h % 4 + 1) * IDX_DIM)
        qim_ref[h * tq:(h + 1) * tq, :] = jnp.where(in_head, grp, 0.0).astype(BF16)

    w = wt_ref[...]

    def idx_chunk(c, masked):
        k0 = pl.multiple_of(c * ck, ck)
        kc = ki_ref[pl.ds(k0, ck), :]
        total = [jnp.zeros((ck, tq), F32)]

        def add(h, r):
            total[0] = total[0] + w[h:h + 1, :] * jnp.maximum(r, 0.0)

        _scores_first(s_ref, nh, lambda h: _dot_nt(kc, qim_ref[h * tq:(h + 1) * tq, :]), add)
        score = total[0]
        if masked:
            score = jnp.where(_admissible(ck, tq, k0, q0), score, NEG)
        score = jnp.where(score == 0.0, 0.0, score)
        bits = lax.bitcast_convert_type(score, I32)
        keys_ref[c] = bits ^ ((bits >> 31) & 0x7FFFFFFF)

    def idx_body(c, carry):
        idx_chunk(c, False)
        return carry

    lax.fori_loop(0, j, idx_body, 0)
    idx_chunk(j, True)

    thr_ref[...] = jnp.full(thr_ref.shape, _KEY_NEG + 1, I32)
    cut_ref[...] = jnp.full(cut_ref.shape, seq, I32)

    def count_keys(pred):
        def body(c, acc):
            part = jnp.where(pred(c), 1.0, 0.0).reshape(ck // 8, 8, tq)
            while part.shape[0] > 1:
                half = part.shape[0] // 2
                part = part[:half] + part[half:]
            return acc + part[0]
        acc = lax.fori_loop(0, nck, body, jnp.zeros((8, tq), F32))
        return jnp.sum(acc, axis=0, keepdims=True)

    @pl.when(q0 + tq > topk)
    def _search():
        def step(i, carry):
            thr, cge = carry
            cand = thr + lax.shift_left(jnp.int32(1), 31 - i)
            cnt = count_keys(lambda c: keys_ref[c] >= cand)
            ok = cnt >= topk
            return jnp.where(ok, cand, thr), jnp.where(ok, cnt, cge)

        thr0 = jnp.full((1, tq), _INT_MIN, I32)
        cge0 = jnp.full((1, tq), 1.0, F32) * (nck * ck).astype(F32)
        thr, cge = lax.fori_loop(0, 32, step, (thr0, cge0))
        thr_ref[...] = thr
        excess = cge - float(topk)

        @pl.when(jnp.max(excess) > 0.0)
        def _ties():
            nbits = max(1, int(seq - 1).bit_length())

            def tstep(i, cut):
                cand = cut + lax.shift_left(jnp.int32(1), nbits - 1 - i)

                def pred(c):
                    idx = c * ck + lax.broadcasted_iota(I32, (ck, tq), 0)
                    return (keys_ref[c] == thr) & (idx >= cand)

                return jnp.where(count_keys(pred) >= excess, cand, cut)

            cut = lax.fori_loop(0, nbits, tstep, jnp.zeros((1, tq), I32))
            cut_ref[...] = jnp.where(excess > 0.0, cut, seq)

    m_ref[...] = jnp.full(m_ref.shape, NEG, F32)
    acc_ref[...] = jnp.zeros(acc_ref.shape, F32)
    thr = thr_ref[...]
    cut = cut_ref[...]

    def att_chunk(c, bias_slab, own):
        k0 = pl.multiple_of(c * ck, ck)
        kc = kk_ref[pl.ds(k0, ck), :]
        key = keys_ref[c]
        idx = k0 + lax.broadcasted_iota(I32, (ck, tq), 0)
        sel = (key > thr) | ((key == thr) & (idx < cut))
        if own:
            sel = sel & _admissible(ck, tq, k0, q0)
        vt = vt_ref[c]

        def use(h, s):
            if bias_slab is not None:
                s = s + bt_ref[bias_slab, h]
            _softmax_step(jnp.where(sel, s, NEG), vt, m_ref, acc_ref, h)

        _scores_first(s_ref, nh, lambda h: _dot_nt(kc, qs_ref[h * tq:(h + 1) * tq, :]), use)

    def far_body(c, carry):
        att_chunk(c, None, False)
        return carry

    lax.fori_loop(0, j - 1, far_body, 0)

    @pl.when(j > 0)
    def _prev():
        att_chunk(j - 1, 0, False)

    att_chunk(j, 1, True)
    _finish(o_ref, acc_ref, nh, DSA_HEAD_DIM)


def _dsa_attn(qb, qi, wt, kk, vt, ki, bt, tq, topk):
    b, t, _ = qb.shape
    ck = tq
    assert tq == bt.shape[2] and tq % LANES == 0 and t % tq == 0
    nh = DSA_HEADS
    blk = lambda a: pl.BlockSpec((None, tq, a.shape[2]), lambda i, j: (i, j, 0))
    seqb = lambda a: pl.BlockSpec((None, t, a.shape[2]), lambda i, j: (i, 0, 0))
    return pl.pallas_call(
        functools.partial(_dsa_kernel, tq=tq, topk=topk),
        grid=(b, t // tq),
        in_specs=[blk(qb), blk(qi), pl.BlockSpec((None, nh, tq), lambda i, j: (i, 0, j)), seqb(kk),
                  pl.BlockSpec((None, t // ck, _VROWS, ck), lambda i, j: (i, 0, 0, 0)), seqb(ki),
                  pl.BlockSpec(bt.shape, lambda i, j: (0, 0, 0, 0))],
        out_specs=blk(qb),
        out_shape=jax.ShapeDtypeStruct(qb.shape, BF16),
        scratch_shapes=[pltpu.VMEM((nh * tq, LANES), BF16), pltpu.VMEM((nh * tq, LANES), BF16),
                        pltpu.VMEM((t // ck, ck, tq), I32), pltpu.VMEM((nh, ck, tq), F32),
                        pltpu.VMEM((1, tq), I32),
                        pltpu.VMEM((1, tq), I32), pltpu.VMEM((nh, 1, tq), F32),
                        pltpu.VMEM((nh, _VROWS, tq), F32)],
        compiler_params=pltpu.CompilerParams(dimension_semantics=("parallel", "arbitrary"),
                                             vmem_limit_bytes=VMEM_LIMIT),
        name="dsa_attn",
    )(qb, qi, wt, kk, vt, ki, bt)


_R_GRP = 0
_R_EXP = N_GROUPS


def _post_kernel(x_ref, oa_ref, ob_ref, g0_ref, b0_ref, wg_ref, bg_ref, wua_ref, wub_ref, wo_ref,
                 g1_ref, b1_ref, wr_ref, br_ref, hx_ref, gsel_ref, *, alpha):
    d = x_ref.shape[1]
    xn = _ln(x_ref[...], g0_ref[...], b0_ref[...])
    z = _dot(xn.astype(BF16), wg_ref[...]) + bg_ref[...]
    gates = 1.0 / (1.0 + jnp.exp(-z))
    mix = gates[:, :d] * _dot(oa_ref[...], wua_ref[...]) + gates[:, d:] * _dot(ob_ref[...], wub_ref[...])
    mixed = _dot(mix.astype(BF16), wo_ref[...])
    h = _ln(alpha * xn + mixed, g1_ref[...], b1_ref[...])
    hx_ref[:, :d] = h

    logits = _dot(h.astype(BF16), wr_ref[...]) + br_ref[...]
    lane = lax.broadcasted_iota(I32, logits.shape, 1).astype(F32)
    gmask = lane < N_GROUPS
    gl = jnp.where(gmask, logits, -jnp.inf)
    gmax = jnp.max(gl, axis=1, keepdims=True)
    g_sel = jnp.min(jnp.where(gl == gmax, lane, float(LANES)), axis=1, keepdims=True)
    g_w = 1.0 / jnp.sum(jnp.where(gmask, jnp.exp(gl - gmax), 0.0), axis=1, keepdims=True)
    e_lo = _R_EXP + g_sel * EXPERTS_PER_GROUP
    el = jnp.where((lane >= e_lo) & (lane < e_lo + EXPERTS_PER_GROUP), logits, -jnp.inf)
    m1 = jnp.max(el, axis=1, keepdims=True)
    i1 = jnp.min(jnp.where(el == m1, lane, float(LANES)), axis=1, keepdims=True)
    el2 = jnp.where(lane == i1, -jnp.inf, el)
    m2 = jnp.max(el2, axis=1, keepdims=True)
    i2 = jnp.min(jnp.where(el2 == m2, lane, float(LANES)), axis=1, keepdims=True)
    e21 = jnp.exp(m2 - m1)
    w1 = g_w / (1.0 + e21)
    w2 = g_w * e21 / (1.0 + e21)
    hx_ref[:, d:] = jnp.where(lane == i1, w1, 0.0) + jnp.where(lane == i2, w2, 0.0)
    gsel_ref[...] = g_sel.astype(I32)


def _post_attn(x2, oa, ob, g0, b0, wg, bg, wua, wub, wo, g1, b1, wr, br, alpha, tm):
    n, d = x2.shape
    row = lambda w: pl.BlockSpec((tm, w), lambda i: (i, 0))
    full = lambda a: pl.BlockSpec(a.shape, lambda i: (0,) * a.ndim)
    return pl.pallas_call(
        functools.partial(_post_kernel, alpha=alpha),
        grid=(n // tm,),
        in_specs=[row(d), row(oa.shape[1]), row(ob.shape[1])]
                 + [full(a) for a in (g0, b0, wg, bg, wua, wub, wo, g1, b1, wr, br)],
        out_specs=[row(d + LANES), row(1)],
        out_shape=[jax.ShapeDtypeStruct((n, d + LANES), F32), jax.ShapeDtypeStruct((n, 1), I32)],
        compiler_params=pltpu.CompilerParams(dimension_semantics=("parallel",),
                                             vmem_limit_bytes=VMEM_LIMIT),
        name="post_attn",
    )(x2, oa, ob, g0, b0, wg, bg, wua, wub, wo, g1, b1, wr, br)


def _moe_kernel(tg_ref, nv_ref, src_ref, srcn_ref, hx_hbm, weg_ref, weu_ref, wed_ref, g2_ref, b2_ref,
                out_hbm, gbuf, obuf, gsem, ssem, *, alpha, tm, d):
    k = pl.program_id(0)
    nk = pl.num_programs(0)
    slot = k % 2
    nrun = tm // LANES

    def gather_copy(tok, r, s):
        return pltpu.make_async_copy(hx_hbm.at[pl.ds(tok, 1)], gbuf.at[s, pl.ds(r, 1)], gsem.at[s])

    def scatter_copy(tok, r, s):
        return pltpu.make_async_copy(obuf.at[s, pl.ds(r, 1)], out_hbm.at[pl.ds(tok, 1)], ssem.at[s])

    def start_gather(tile_src_ref, s):
        for g in range(nrun):
            for c in range(LANES):
                gather_copy(tile_src_ref[g, c], g * LANES + c, s).start()

    def start_scatter(n_rows, s):
        for g in range(nrun):
            @pl.when(n_rows >= (g + 1) * LANES)
            def _run():
                for c in range(LANES):
                    scatter_copy(src_ref[g, c], g * LANES + c, s).start()

        def row(r, carry):
            scatter_copy(src_ref[r // LANES, r % LANES], r, s).start()
            return carry

        lax.fori_loop(n_rows // LANES * LANES, n_rows, row, 0)

    def by_slot(fn, s_of_slot):
        for par in (0, 1):
            pl.when(slot == par)(functools.partial(fn, s_of_slot(par)))

    def wait_scatter(n_rows, s):
        def waiter(rows):
            def body(i, carry):
                pltpu.make_async_copy(obuf.at[s, pl.ds(0, rows)], out_hbm.at[pl.ds(0, rows)], ssem.at[s]).wait()
                return carry
            return body

        lax.fori_loop(0, n_rows // LANES, waiter(LANES), 0)
        lax.fori_loop(0, n_rows % LANES // _SUBLANES, waiter(_SUBLANES), 0)
        lax.fori_loop(0, n_rows % _SUBLANES, waiter(1), 0)

    @pl.when(k == 0)
    def _first():
        start_gather(src_ref, 0)

    pltpu.make_async_copy(hx_hbm.at[pl.ds(0, tm)], gbuf.at[slot], gsem.at[slot]).wait()

    @pl.when(k + 1 < nk)
    def _prefetch():
        by_slot(lambda s: start_gather(srcn_ref, s), lambda par: 1 - par)

    @pl.when(k >= 2)
    def _free_obuf():
        wait_scatter(nv_ref[k - 2], slot)

    x = gbuf[slot]
    h = x[:, :d]
    comb = x[:, d:]
    hb = h.astype(BF16)
    lane = lax.broadcasted_iota(I32, comb.shape, 1)
    e0 = _R_EXP + tg_ref[k] * EXPERTS_PER_GROUP
    acc = jnp.zeros((tm, d), F32)
    for e in range(EXPERTS_PER_GROUP):
        a = _dot(hb, weg_ref[e])
        u = _dot(hb, weu_ref[e])
        cw = jnp.sum(jnp.where(lane == e0 + e, comb, 0.0), axis=1, keepdims=True)
        hid = (a / (1.0 + jnp.exp(-a))) * u * cw
        acc = acc + _dot(hid.astype(BF16), wed_ref[e])
    obuf[slot] = _ln(alpha * h + acc, g2_ref[...], b2_ref[...])

    nv = nv_ref[k]
    by_slot(lambda s: start_scatter(nv, s), lambda par: par)

    @pl.when(k == nk - 1)
    def _drain():
        wait_scatter(nv, slot)

        @pl.when(k >= 1)
        def _():
            wait_scatter(nv_ref[k - 1], 1 - slot)


def _moe(hx, tile_group, n_valid, src, weg, weu, wed, g2, b2, alpha, tm, n):
    d = hx.shape[1] - LANES
    nk = src.shape[0]
    ne, _, f = weg.shape
    gsz = EXPERTS_PER_GROUP
    grid_spec = pltpu.PrefetchScalarGridSpec(
        num_scalar_prefetch=2,
        grid=(nk,),
        in_specs=[pl.BlockSpec((None, tm // LANES, LANES), lambda k, tg, nv: (k, 0, 0),
                               memory_space=pltpu.SMEM),
                  pl.BlockSpec((None, tm // LANES, LANES),
                               lambda k, tg, nv: (jnp.minimum(k + 1, nk - 1), 0, 0), memory_space=pltpu.SMEM),
                  pl.BlockSpec(memory_space=pl.ANY),
                  pl.BlockSpec((gsz, d, f), lambda k, tg, nv: (tg[k], 0, 0)),
                  pl.BlockSpec((gsz, d, f), lambda k, tg, nv: (tg[k], 0, 0)),
                  pl.BlockSpec((gsz, f, d), lambda k, tg, nv: (tg[k], 0, 0)),
                  pl.BlockSpec(g2.shape, lambda k, tg, nv: (0, 0)),
                  pl.BlockSpec(b2.shape, lambda k, tg, nv: (0, 0))],
        out_specs=pl.BlockSpec(memory_space=pl.ANY),
        scratch_shapes=[pltpu.VMEM((2, tm, d + LANES), F32), pltpu.VMEM((2, tm, d), F32),
                        pltpu.SemaphoreType.DMA((2,)), pltpu.SemaphoreType.DMA((2,))])
    return pl.pallas_call(
        functools.partial(_moe_kernel, alpha=alpha, tm=tm, d=d),
        grid_spec=grid_spec,
        out_shape=jax.ShapeDtypeStruct((n, d), F32),
        compiler_params=pltpu.CompilerParams(dimension_semantics=("arbitrary",),
                                             vmem_limit_bytes=VMEM_LIMIT),
        name="moe",
    )(tile_group, n_valid, src, src, hx, weg, weu, wed, g2, b2)


def _group_tiles(gsel, tm):
    n = gsel.shape[0]
    n_tiles = n // tm + N_GROUPS
    onehot = (gsel[:, None] == jnp.arange(N_GROUPS, dtype=I32)[None, :]).astype(I32)
    csum = jnp.cumsum(onehot, axis=0)
    rank = jnp.sum(csum * onehot, axis=1) - 1
    counts = csum[-1]
    padded = (counts + tm - 1) // tm * tm
    ends = jnp.cumsum(padded)
    offs = ends - padded
    dest = offs[gsel] + rank
    src = jnp.zeros((n_tiles * tm,), I32).at[dest].set(jnp.arange(n, dtype=I32))
    tile_start = jnp.arange(n_tiles, dtype=I32) * tm
    tile_group = jnp.minimum(jnp.sum((tile_start[:, None] >= ends[None, :]).astype(I32), axis=1), N_GROUPS - 1)
    n_valid = jnp.clip(offs[tile_group] + counts[tile_group] - tile_start, 0, tm)
    return tile_group.astype(I32), n_valid.astype(I32), src.reshape(n_tiles, tm // LANES, LANES)


def _layout_weights(w_in, w_uq, w_uk, w_uv):
    d = w_in.shape[0]
    sizes = (MLA_Q_RANK, MLA_KV_RANK, MLA_ROPE, DSA_HEADS * DSA_HEAD_DIM, DSA_HEAD_DIM, DSA_HEAD_DIM,
             IDX_HEADS * IDX_DIM, IDX_DIM, IDX_HEADS)
    offs = np.cumsum((0,) + sizes)
    c_q, c_kv, k_rope, q_b, k_b, v_b, q_idx, k_idx, w_idx = (
        w_in[:, int(offs[i]):int(offs[i + 1])] for i in range(len(sizes)))
    half = MLA_ROPE // 2
    z = lambda w: jnp.zeros((d, w), w_in.dtype)
    pad = LANES - MLA_NOPE - MLA_ROPE
    grp_a = jnp.concatenate([w_idx, z(MLA_NOPE - IDX_HEADS), k_rope, z(pad)], axis=1)
    grp_b = jnp.concatenate([z(MLA_NOPE), k_rope[:, half:], k_rope[:, :half], z(pad)], axis=1)
    w1 = jnp.concatenate([c_q, c_kv, grp_a, grp_b, q_b * DSA_HEAD_DIM ** -0.5, k_b, k_b, v_b, v_b, q_idx,
                          k_idx, k_idx, k_idx, k_idx], axis=1)
    assert w1.shape[1] == _C_END

    dk = MLA_NOPE + MLA_ROPE
    rq = w_uq.shape[0]
    zq = lambda w: jnp.zeros((rq, w), w_uq.dtype)
    wq_parts, wqr_parts, wk_parts = [], [], []
    for h in range(MLA_HEADS):
        nope = w_uq[:, h * dk:h * dk + MLA_NOPE]
        rope = w_uq[:, h * dk + MLA_NOPE:(h + 1) * dk]
        wq_parts += [nope, rope, zq(pad)]
        wqr_parts += [zq(MLA_NOPE), rope[:, half:], rope[:, :half], zq(pad)]
        wk_parts += [w_uk[:, h * MLA_NOPE:(h + 1) * MLA_NOPE],
                     jnp.zeros((w_uk.shape[0], LANES - MLA_NOPE), w_uk.dtype)]
    wq = jnp.concatenate(wq_parts, axis=1)
    wqr = jnp.concatenate(wqr_parts, axis=1)
    wk = jnp.concatenate(wk_parts, axis=1)

    inv = ROPE_THETA ** (-np.arange(half, dtype=np.float32) / half)
    invf = np.zeros((1, LANES), np.float32)
    invf[0, MLA_NOPE:MLA_NOPE + half] = inv
    invf[0, MLA_NOPE + half:MLA_NOPE + MLA_ROPE] = inv
    sgn = np.zeros((1, LANES), np.float32)
    sgn[0, MLA_NOPE:MLA_NOPE + half] = -1.0
    sgn[0, MLA_NOPE + half:MLA_NOPE + MLA_ROPE] = 1.0
    return (w1.astype(BF16), wq.astype(BF16), wqr.astype(BF16), wk.astype(BF16), w_uv.astype(BF16),
            jnp.asarray(invf), jnp.asarray(sgn))


def kernel(x, positions, ln0_g, ln0_b, w_in, q_norm_g, w_uq, kv_norm_g, w_uk, w_uv, rel_bias, w_up_a, w_up_b,
           w_gate, b_gate, w_o, ln1_g, ln1_b, w_grp, b_grp, w_rt, b_rt, w_exp_gate, w_exp_up, w_exp_down,
           ln2_g, ln2_b):
    bsz, seq, d = x.shape
    depth = w_in.shape[0]
    assert depth == 1
    alpha = (2.0 * depth) ** 0.25
    n = bsz * seq
    topk = min(IDX_TOPK_MAX, seq // 4)
    r2 = lambda v: v.reshape(1, -1)

    x2 = x.reshape(n, d)
    pos2 = positions.reshape(n, 1)
    w1, wq, wqr, wk, wv, invf, sgn = _layout_weights(w_in[0], w_uq[0], w_uk[0], w_uv[0])

    blk = 256
    bt = _bias_tiles(rel_bias, blk)
    qm, km, vm, qb, kk, vv, qi, ki, wi = _in_proj(
        x2, pos2, r2(ln0_g), r2(ln0_b), w1, r2(q_norm_g[0]), wq, wqr, r2(kv_norm_g[0]), wk, wv, invf, sgn,
        tm=512)
    b3 = lambda a: a.reshape(bsz, seq, a.shape[1])
    o_a = _mla_attn(b3(qm), b3(km), _value_slabs(vm, bsz, seq, blk, MLA_HEADS, MLA_V), tq=blk, ck=blk)
    o_b = _dsa_attn(b3(qb), b3(qi), b3(wi).transpose(0, 2, 1), b3(kk),
                    _value_slabs(vv[:, :DSA_HEAD_DIM], bsz, seq, blk, 1, DSA_HEAD_DIM), b3(ki), bt,
                    tq=blk, topk=topk)

    wr = jnp.concatenate([w_grp[0], w_rt[0], jnp.zeros((d, LANES - N_GROUPS - N_EXPERTS), F32)], axis=1)
    br = jnp.concatenate([b_grp[0], b_rt[0], jnp.zeros((LANES - N_GROUPS - N_EXPERTS,), F32)])
    hx, gsel = _post_attn(
        x2, o_a.reshape(n, -1), o_b.reshape(n, -1), r2(ln0_g), r2(ln0_b), w_gate[0].astype(BF16), r2(b_gate[0]),
        w_up_a[0].astype(BF16), w_up_b[0].astype(BF16), w_o[0].astype(BF16), r2(ln1_g[0]), r2(ln1_b[0]),
        wr.astype(BF16), r2(br), alpha, tm=512)
    tm_moe = 512
    tile_group, n_valid, src = _group_tiles(gsel[:, 0], tm_moe)
    out = _moe(hx, tile_group, n_valid, src, w_exp_gate[0].astype(BF16), w_exp_up[0].astype(BF16),
               w_exp_down[0].astype(BF16), r2(ln2_g[0]), r2(ln2_b[0]), alpha, tm_moe, n)
    return out.reshape(bsz, seq, d)
```

```python
import functools
import math

import numpy as np
import jax
import jax.numpy as jnp
from jax import lax
from jax.experimental import pallas as pl
from jax.experimental.pallas import tpu as pltpu

F32 = jnp.float32
BF16 = jnp.bfloat16
I32 = jnp.int32
I16 = jnp.int16

CHUNK = 64
MLA_HEADS = 8
MLA_Q_RANK = 256
MLA_KV_RANK = 128
MLA_NOPE = 64
MLA_ROPE = 32
MLA_V = 64
ROPE_THETA = 10000.0
DSA_HEADS = 8
DSA_HEAD_DIM = 64
IDX_HEADS = 8
IDX_DIM = 32
IDX_TOPK_MAX = 256
REL_BUCKETS = 32
REL_MAX_DIST = 128
N_GROUPS = 4
EXPERTS_PER_GROUP = 8
N_EXPERTS = N_GROUPS * EXPERTS_PER_GROUP
EXPERT_HIDDEN = 256
LN_EPS = 1e-5
RMS_EPS = 1e-6
NEG = -1e30

LANES = 128
_SUBLANES = 8
_VROWS = 80
VMEM_LIMIT = 48 * 1024 * 1024

_KEY_NEG = int(np.array(NEG, np.float32).view(np.int32)) ^ 0x7FFFFFFF
_I16_MIN = -(2 ** 15)
_LOG2E = math.log2(math.e)
_CHUNK_SHIFT = CHUNK.bit_length() - 1
assert 1 << _CHUNK_SHIFT == CHUNK


def _ln(x, g, b):
    mu = jnp.mean(x, axis=-1, keepdims=True)
    xc = x - mu
    var = jnp.mean(xc * xc, axis=-1, keepdims=True)
    return xc * lax.rsqrt(var + LN_EPS) * g + b


def _rms(x, g):
    return x * lax.rsqrt(jnp.mean(x * x, axis=-1, keepdims=True) + RMS_EPS) * g


def _dot(a, b):
    return jnp.dot(a, b, preferred_element_type=F32)


def _dot_nt(a, b):
    return lax.dot_general(a, b, (((1,), (1,)), ((), ())), preferred_element_type=F32)


def _bias_kernel(rb_ref, o_ref):
    nb = REL_BUCKETS // 2
    max_exact = nb // 2
    blk = o_ref.shape[2]
    s = lax.broadcasted_iota(I32, (blk, blk), 0)
    t = lax.broadcasted_iota(I32, (blk, blk), 1)
    for which, off in ((0, -blk), (1, 0)):
        rel = s - t + off
        ret = jnp.where(rel > 0, nb, 0)
        n = jnp.abs(rel)
        large = max_exact + (jnp.log(jnp.maximum(n, 1).astype(F32) / max_exact)
                             / math.log(REL_MAX_DIST / max_exact) * (nb - max_exact)).astype(I32)
        large = jnp.minimum(large, nb - 1)
        bucket = ret + jnp.where(n < max_exact, n, large)
        for h in range(DSA_HEADS):
            acc = jnp.zeros((blk, blk), F32)
            for b in range(REL_BUCKETS):
                acc = jnp.where(bucket == b, rb_ref[b, h], acc)
            o_ref[which, h] = (acc - rb_ref[nb - 1, h]) * _LOG2E


def _bias_tiles(rel_bias, blk):
    assert blk >= REL_MAX_DIST
    return pl.pallas_call(
        _bias_kernel,
        out_shape=jax.ShapeDtypeStruct((2, DSA_HEADS, blk, blk), F32),
        in_specs=[pl.BlockSpec(memory_space=pltpu.SMEM)],
        out_specs=pl.BlockSpec(memory_space=pltpu.VMEM),
        name="bias_tiles",
    )(rel_bias)


_C_Q, _C_KV, _C_RA, _C_RB, _C_QB, _C_KK, _C_VV, _C_QI, _C_KI, _C_END = (
    0, 256, 384, 512, 640, 1152, 1280, 1408, 1664, 1792)


def _in_proj_kernel(x_ref, pos_ref, g0_ref, b0_ref, w1_ref, qg_ref, wq_ref, wqr_ref, kg_ref, wk_ref,
                    wv_ref, invf_ref, sgn_ref,
                    qm_ref, km_ref, vm_ref, qb_ref, kk_ref, vv_ref, qi_ref, ki_ref, wi_ref):
    xn = _ln(x_ref[...], g0_ref[...], b0_ref[...])
    proj = _dot(xn.astype(BF16), w1_ref[...])
    qb_ref[...] = (proj[:, _C_QB:_C_KK] * _LOG2E).astype(BF16)
    kk_ref[...] = proj[:, _C_KK:_C_VV].astype(BF16)
    vv_ref[...] = proj[:, _C_VV:_C_QI].astype(BF16)
    qi_ref[...] = proj[:, _C_QI:_C_KI].astype(BF16)
    ki_ref[...] = proj[:, _C_KI:_C_END].astype(BF16)
    ga = proj[:, _C_RA:_C_RB]
    gb = proj[:, _C_RB:_C_QB]
    wi_ref[...] = ga[:, 0:IDX_HEADS] * (IDX_HEADS ** -0.5 * IDX_DIM ** -0.5)

    ang = pos_ref[...].astype(F32) * invf_ref[...]
    cosv = jnp.cos(ang)
    sinv = jnp.sin(ang) * sgn_ref[...]
    lane = lax.broadcasted_iota(I32, ang.shape, 1)
    rope_lane = (lane >= MLA_NOPE) & (lane < MLA_NOPE + MLA_ROPE)
    kr = jnp.where(rope_lane, ga * cosv + gb * sinv, 0.0)

    scale = (MLA_NOPE + MLA_ROPE) ** -0.5 * _LOG2E
    cos8 = jnp.concatenate([cosv * scale] * MLA_HEADS, axis=1)
    sin8 = jnp.concatenate([sinv * scale] * MLA_HEADS, axis=1)
    cqn = _rms(proj[:, _C_Q:_C_KV], qg_ref[...]).astype(BF16)
    q = _dot(cqn, wq_ref[...]) * cos8 + _dot(cqn, wqr_ref[...]) * sin8
    qm_ref[...] = q.astype(BF16)

    ckn = _rms(proj[:, _C_KV:_C_RA], kg_ref[...]).astype(BF16)
    k = _dot(ckn, wk_ref[...]) + jnp.concatenate([kr] * MLA_HEADS, axis=1)
    km_ref[...] = k.astype(BF16)
    vm_ref[...] = _dot(ckn, wv_ref[...]).astype(BF16)


def _in_proj(x2, pos2, g0, b0, w1, qg, wq, wqr, kg, wk, wv, invf, sgn, tm):
    n, d = x2.shape
    hm = MLA_HEADS * LANES
    row = lambda w: pl.BlockSpec((tm, w), lambda i: (i, 0))
    full = lambda a: pl.BlockSpec(a.shape, lambda i: (0,) * a.ndim)
    outs = [(hm, BF16), (hm, BF16), (MLA_HEADS * MLA_V, BF16), (DSA_HEADS * DSA_HEAD_DIM, BF16),
            (LANES, BF16), (LANES, BF16), (IDX_HEADS * IDX_DIM, BF16), (LANES, BF16), (IDX_HEADS, F32)]
    return pl.pallas_call(
        _in_proj_kernel,
        grid=(n // tm,),
        in_specs=[row(d), row(1)] + [full(a) for a in (g0, b0, w1, qg, wq, wqr, kg, wk, wv, invf, sgn)],
        out_specs=[row(w) for w, _ in outs],
        out_shape=[jax.ShapeDtypeStruct((n, w), dt) for w, dt in outs],
        compiler_params=pltpu.CompilerParams(dimension_semantics=("parallel",),
                                             vmem_limit_bytes=VMEM_LIMIT),
        name="in_proj",
    )(x2, pos2, g0, b0, w1, qg, wq, wqr, kg, wk, wv, invf, sgn)


def _admissible(ck, tq, k0, q0):
    s = k0 + lax.broadcasted_iota(I32, (ck, tq), 0)
    t = q0 + lax.broadcasted_iota(I32, (ck, tq), 1)
    return (s >> _CHUNK_SHIFT) <= (t >> _CHUNK_SHIFT)


def _softmax_step(s, vt, m_ref, acc_ref, h):
    sb = s.astype(BF16)
    m_prev = m_ref[h]
    m_new = jnp.maximum(m_prev, jnp.max(sb, axis=0, keepdims=True).astype(F32))
    alpha = jnp.exp2(m_prev - m_new)
    p = jnp.exp2(sb - m_new.astype(BF16))
    acc_ref[h] = alpha * acc_ref[h] + _dot(vt, p)
    m_ref[h] = m_new


def _scores_first(s_ref, nh, produce, consume):
    for h in range(nh):
        s_ref[h] = produce(h)
    for h in range(nh):
        consume(h, s_ref[h])


def _finish(o_ref, acc_ref, nh, dv):
    ot = jnp.concatenate([acc_ref[h, :dv, :] / acc_ref[h, dv:dv + 1, :] for h in range(nh)], axis=0)
    o_ref[...] = ot.T.astype(BF16)


def _value_slabs(v, bsz, seq, blk, nh, dv):
    nc = seq // blk
    vt = v.reshape(bsz, nc, blk, nh, dv).transpose(0, 1, 3, 4, 2)
    ones = jnp.ones((bsz, nc, nh, 1, blk), v.dtype)
    zeros = jnp.zeros((bsz, nc, nh, _VROWS - dv - 1, blk), v.dtype)
    return jnp.concatenate([vt, ones, zeros], axis=3).reshape(bsz, nc, nh * _VROWS, blk)


def _mla_kernel(q_ref, k_ref, vt_ref, o_ref, s_ref, m_ref, acc_ref, *, tq, ck):
    j = pl.program_id(1)
    q0 = j * tq
    m_ref[...] = jnp.full(m_ref.shape, NEG, F32)
    acc_ref[...] = jnp.zeros(acc_ref.shape, F32)
    nfull = (q0 + CHUNK) // ck

    def chunk(c, masked):
        k0 = pl.multiple_of(c * ck, ck)
        adm = _admissible(ck, tq, k0, q0) if masked else None

        def score(h):
            return _dot_nt(k_ref[pl.ds(k0, ck), h * LANES:(h + 1) * LANES], q_ref[:, h * LANES:(h + 1) * LANES])

        def use(h, s):
            if masked:
                s = jnp.where(adm, s, NEG)
            _softmax_step(s, vt_ref[c, h * _VROWS:(h + 1) * _VROWS, :], m_ref, acc_ref, h)

        _scores_first(s_ref, MLA_HEADS, score, use)

    def full_body(c, carry):
        chunk(c, False)
        return carry

    lax.fori_loop(0, nfull, full_body, 0)
    chunk(nfull, True)
    _finish(o_ref, acc_ref, MLA_HEADS, MLA_V)


def _mla_attn(qm, km, vmt, tq, ck):
    b, t, hm = qm.shape
    dv = MLA_HEADS * MLA_V
    assert ck % tq == 0 and tq > CHUNK and tq % CHUNK == 0 and t % ck == 0
    return pl.pallas_call(
        functools.partial(_mla_kernel, tq=tq, ck=ck),
        grid=(b, t // tq),
        in_specs=[pl.BlockSpec((None, tq, hm), lambda i, j: (i, j, 0)),
                  pl.BlockSpec((None, t, hm), lambda i, j: (i, 0, 0)),
                  pl.BlockSpec((None, t // ck, MLA_HEADS * _VROWS, ck), lambda i, j: (i, 0, 0, 0))],
        out_specs=pl.BlockSpec((None, tq, dv), lambda i, j: (i, j, 0)),
        out_shape=jax.ShapeDtypeStruct((b, t, dv), BF16),
        scratch_shapes=[pltpu.VMEM((MLA_HEADS, ck, tq), F32), pltpu.VMEM((MLA_HEADS, 1, tq), F32),
                        pltpu.VMEM((MLA_HEADS, _VROWS, tq), F32)],
        compiler_params=pltpu.CompilerParams(dimension_semantics=("parallel", "arbitrary"),
                                             vmem_limit_bytes=VMEM_LIMIT),
        name="mla_attn",
    )(qm, km, vmt)


def _dsa_kernel(qb_ref, qi_ref, wt_ref, kk_ref, vt_ref, ki_ref, bt_ref, o_ref,
                qs_ref, qim_ref, keys_ref, khi_ref, klo_ref, kls_ref, s_ref, thr_ref, cut_ref, m_ref, acc_ref,
                *, tq, topk):
    nh = DSA_HEADS
    ck = tq
    seq = kk_ref.shape[0]
    j = pl.program_id(1)
    q0 = j * tq
    nck = j + 1
    lane = lax.broadcasted_iota(I32, (tq, LANES), 1)

    for h in range(nh):
        pair = qb_ref[:, (h // 2) * LANES:(h // 2 + 1) * LANES].astype(F32)
        in_head = (lane >= (h % 2) * DSA_HEAD_DIM) & (lane < (h % 2 + 1) * DSA_HEAD_DIM)
        qs_ref[h * tq:(h + 1) * tq, :] = jnp.where(in_head, pair, 0.0).astype(BF16)
        grp = qi_ref[:, (h // 4) * LANES:(h // 4 + 1) * LANES].astype(F32)
        in_head = (lane >= (h % 4) * IDX_DIM) & (lane < (h % 4 + 1) * IDX_DIM)
        qim_ref[h * tq:(h + 1) * tq, :] = jnp.where(in_head, grp, 0.0).astype(BF16)

    w = wt_ref[...]

    def idx_chunk(c, masked):
        k0 = pl.multiple_of(c * ck, ck)
        kc = ki_ref[pl.ds(k0, ck), :]
        total = [jnp.zeros((ck, tq), F32)]

        def add(h, r):
            total[0] = total[0] + w[h:h + 1, :] * jnp.maximum(r, 0.0)

        _scores_first(s_ref, nh, lambda h: _dot_nt(kc, qim_ref[h * tq:(h + 1) * tq, :]), add)
        score = total[0]
        if masked:
            score = jnp.where(_admissible(ck, tq, k0, q0), score, NEG)
        score = jnp.where(score == 0.0, 0.0, score)
        bits = lax.bitcast_convert_type(score, I32)
        key = bits ^ ((bits >> 31) & 0x7FFFFFFF)
        keys_ref[c] = key
        khi_ref[c] = (key >> 16).astype(I16)
        klo_ref[c] = ((key & 0xFFFF) - 0x8000).astype(I16)

    def idx_body(c, carry):
        idx_chunk(c, False)
        return carry

    lax.fori_loop(0, j, idx_body, 0)
    idx_chunk(j, True)

    thr_ref[...] = jnp.full(thr_ref.shape, _KEY_NEG + 1, I32)
    cut_ref[...] = jnp.full(cut_ref.shape, seq, I32)

    def count_keys(pred):
        def body(c, acc):
            part = jnp.where(pred(c), 1.0, 0.0).reshape(ck // 8, 8, tq)
            while part.shape[0] > 1:
                half = part.shape[0] // 2
                part = part[:half] + part[half:]
            return acc + part[0]
        acc = lax.fori_loop(0, nck, body, jnp.zeros((8, tq), F32))
        return jnp.sum(acc, axis=0, keepdims=True)

    def count16(ref, cand):
        rows = 2 * _SUBLANES

        def body(c, acc):
            part = jnp.where(ref[c] >= cand, jnp.int16(1), jnp.int16(0)).reshape(ck // rows, rows, tq)
            while part.shape[0] > 1:
                half = part.shape[0] // 2
                part = part[:half] + part[half:]
            return acc + part[0]
        acc = lax.fori_loop(0, nck, body, jnp.zeros((rows, tq), I16))
        return jnp.sum(acc.astype(F32), axis=0, keepdims=True)

    def greedy16(ref, need):
        def step(i, carry):
            v, cge, cgt = carry
            cand = v + lax.shift_left(jnp.int32(1), 15 - i)
            cnt = count16(ref, cand.astype(I16))
            ok = cnt >= need
            return jnp.where(ok, cand, v), jnp.where(ok, cnt, cge), jnp.where(ok, cgt, cnt)

        v0 = jnp.full((1, tq), _I16_MIN, I32)
        cge0 = jnp.full((1, tq), 1.0, F32) * (nck * ck).astype(F32)
        return lax.fori_loop(0, 16, step, (v0, cge0, jnp.zeros((1, tq), F32)))

    @pl.when(q0 + tq > topk)
    def _search():
        hi, cge_hi, cgt_hi = greedy16(khi_ref, float(topk))
        hi16 = hi.astype(I16)

        def mask_body(c, carry):
            kls_ref[c] = jnp.where(khi_ref[c] == hi16, klo_ref[c], jnp.int16(_I16_MIN))
            return carry

        lax.fori_loop(0, nck, mask_body, 0)
        lo, cge_lo, _ = greedy16(kls_ref, float(topk) - cgt_hi)
        thr = hi * 65536 + (lo + 0x8000)
        thr_ref[...] = thr
        cge = cgt_hi + jnp.where(lo == _I16_MIN, cge_hi - cgt_hi, cge_lo)
        excess = cge - float(topk)

        @pl.when(jnp.max(excess) > 0.0)
        def _ties():
            nbits = max(1, int(seq - 1).bit_length())

            def tstep(i, cut):
                cand = cut + lax.shift_left(jnp.int32(1), nbits - 1 - i)

                def pred(c):
                    idx = c * ck + lax.broadcasted_iota(I32, (ck, tq), 0)
                    return (keys_ref[c] == thr) & (idx >= cand)

                return jnp.where(count_keys(pred) >= excess, cand, cut)

            cut = lax.fori_loop(0, nbits, tstep, jnp.zeros((1, tq), I32))
            cut_ref[...] = jnp.where(excess > 0.0, cut, seq)

    m_ref[...] = jnp.full(m_ref.shape, NEG, F32)
    acc_ref[...] = jnp.zeros(acc_ref.shape, F32)
    thr = thr_ref[...]
    cut = cut_ref[...]

    def att_chunk(c, bias_slab, own):
        k0 = pl.multiple_of(c * ck, ck)
        kc = kk_ref[pl.ds(k0, ck), :]
        key = keys_ref[c]
        idx = k0 + lax.broadcasted_iota(I32, (ck, tq), 0)
        sel = (key > thr) | ((key == thr) & (idx < cut))
        if own:
            sel = sel & _admissible(ck, tq, k0, q0)
        vt = vt_ref[c]

        def use(h, s):
            if bias_slab is not None:
                s = s + bt_ref[bias_slab, h]
            _softmax_step(jnp.where(sel, s, NEG), vt, m_ref, acc_ref, h)

        _scores_first(s_ref, nh, lambda h: _dot_nt(kc, qs_ref[h * tq:(h + 1) * tq, :]), use)

    def far_body(c, carry):
        att_chunk(c, None, False)
        return carry

    lax.fori_loop(0, j - 1, far_body, 0)

    @pl.when(j > 0)
    def _prev():
        att_chunk(j - 1, 0, False)

    att_chunk(j, 1, True)
    _finish(o_ref, acc_ref, nh, DSA_HEAD_DIM)


def _dsa_attn(qb, qi, wt, kk, vt, ki, bt, tq, topk):
    b, t, _ = qb.shape
    ck = tq
    assert tq == bt.shape[2] and tq % LANES == 0 and t % tq == 0
    nh = DSA_HEADS
    blk = lambda a: pl.BlockSpec((None, tq, a.shape[2]), lambda i, j: (i, j, 0))
    seqb = lambda a: pl.BlockSpec((None, t, a.shape[2]), lambda i, j: (i, 0, 0))
    return pl.pallas_call(
        functools.partial(_dsa_kernel, tq=tq, topk=topk),
        grid=(b, t // tq),
        in_specs=[blk(qb), blk(qi), pl.BlockSpec((None, nh, tq), lambda i, j: (i, 0, j)), seqb(kk),
                  pl.BlockSpec((None, t // ck, _VROWS, ck), lambda i, j: (i, 0, 0, 0)), seqb(ki),
                  pl.BlockSpec(bt.shape, lambda i, j: (0, 0, 0, 0))],
        out_specs=blk(qb),
        out_shape=jax.ShapeDtypeStruct(qb.shape, BF16),
        scratch_shapes=[pltpu.VMEM((nh * tq, LANES), BF16), pltpu.VMEM((nh * tq, LANES), BF16),
                        pltpu.VMEM((t // ck, ck, tq), I32), pltpu.VMEM((t // ck, ck, tq), I16),
                        pltpu.VMEM((t // ck, ck, tq), I16), pltpu.VMEM((t // ck, ck, tq), I16),
                        pltpu.VMEM((nh, ck, tq), F32),
                        pltpu.VMEM((1, tq), I32),
                        pltpu.VMEM((1, tq), I32), pltpu.VMEM((nh, 1, tq), F32),
                        pltpu.VMEM((nh, _VROWS, tq), F32)],
        compiler_params=pltpu.CompilerParams(dimension_semantics=("parallel", "arbitrary"),
                                             vmem_limit_bytes=VMEM_LIMIT),
        name="dsa_attn",
    )(qb, qi, wt, kk, vt, ki, bt)


_R_GRP = 0
_R_EXP = N_GROUPS


def _post_kernel(x_ref, oa_ref, ob_ref, g0_ref, b0_ref, wg_ref, bg_ref, wua_ref, wub_ref, wo_ref,
                 g1_ref, b1_ref, wr_ref, br_ref, hx_ref, gsel_ref, *, alpha):
    d = x_ref.shape[1]
    xn = _ln(x_ref[...], g0_ref[...], b0_ref[...])
    z = _dot(xn.astype(BF16), wg_ref[...]) + bg_ref[...]
    gates = 1.0 / (1.0 + jnp.exp(-z))
    mix = gates[:, :d] * _dot(oa_ref[...], wua_ref[...]) + gates[:, d:] * _dot(ob_ref[...], wub_ref[...])
    mixed = _dot(mix.astype(BF16), wo_ref[...])
    h = _ln(alpha * xn + mixed, g1_ref[...], b1_ref[...])
    hx_ref[:, :d] = h

    logits = _dot(h.astype(BF16), wr_ref[...]) + br_ref[...]
    lane = lax.broadcasted_iota(I32, logits.shape, 1).astype(F32)
    gmask = lane < N_GROUPS
    gl = jnp.where(gmask, logits, -jnp.inf)
    gmax = jnp.max(gl, axis=1, keepdims=True)
    g_sel = jnp.min(jnp.where(gl == gmax, lane, float(LANES)), axis=1, keepdims=True)
    g_w = 1.0 / jnp.sum(jnp.where(gmask, jnp.exp(gl - gmax), 0.0), axis=1, keepdims=True)
    e_lo = _R_EXP + g_sel * EXPERTS_PER_GROUP
    el = jnp.where((lane >= e_lo) & (lane < e_lo + EXPERTS_PER_GROUP), logits, -jnp.inf)
    m1 = jnp.max(el, axis=1, keepdims=True)
    i1 = jnp.min(jnp.where(el == m1, lane, float(LANES)), axis=1, keepdims=True)
    el2 = jnp.where(lane == i1, -jnp.inf, el)
    m2 = jnp.max(el2, axis=1, keepdims=True)
    i2 = jnp.min(jnp.where(el2 == m2, lane, float(LANES)), axis=1, keepdims=True)
    e21 = jnp.exp(m2 - m1)
    w1 = g_w / (1.0 + e21)
    w2 = g_w * e21 / (1.0 + e21)
    hx_ref[:, d:] = jnp.where(lane == i1, w1, 0.0) + jnp.where(lane == i2, w2, 0.0)
    gsel_ref[...] = g_sel.astype(I32)


def _post_attn(x2, oa, ob, g0, b0, wg, bg, wua, wub, wo, g1, b1, wr, br, alpha, tm):
    n, d = x2.shape
    row = lambda w: pl.BlockSpec((tm, w), lambda i: (i, 0))
    full = lambda a: pl.BlockSpec(a.shape, lambda i: (0,) * a.ndim)
    return pl.pallas_call(
        functools.partial(_post_kernel, alpha=alpha),
        grid=(n // tm,),
        in_specs=[row(d), row(oa.shape[1]), row(ob.shape[1])]
                 + [full(a) for a in (g0, b0, wg, bg, wua, wub, wo, g1, b1, wr, br)],
        out_specs=[row(d + LANES), row(1)],
        out_shape=[jax.ShapeDtypeStruct((n, d + LANES), F32), jax.ShapeDtypeStruct((n, 1), I32)],
        compiler_params=pltpu.CompilerParams(dimension_semantics=("parallel",),
                                             vmem_limit_bytes=VMEM_LIMIT),
        name="post_attn",
    )(x2, oa, ob, g0, b0, wg, bg, wua, wub, wo, g1, b1, wr, br)


def _moe_kernel(tg_ref, nv_ref, src_ref, srcn_ref, hx_hbm, weg_ref, weu_ref, wed_ref, g2_ref, b2_ref,
                out_hbm, gbuf, obuf, gsem, ssem, *, alpha, tm, d):
    k = pl.program_id(0)
    nk = pl.num_programs(0)
    slot = k % 2
    nrun = tm // LANES

    def gather_copy(tok, r, s):
        return pltpu.make_async_copy(hx_hbm.at[pl.ds(tok, 1)], gbuf.at[s, pl.ds(r, 1)], gsem.at[s])

    def scatter_copy(tok, r, s):
        return pltpu.make_async_copy(obuf.at[s, pl.ds(r, 1)], out_hbm.at[pl.ds(tok, 1)], ssem.at[s])

    def start_gather(tile_src_ref, s):
        for g in range(nrun):
            for c in range(LANES):
                gather_copy(tile_src_ref[g, c], g * LANES + c, s).start()

    def start_scatter(n_rows, s):
        for g in range(nrun):
            @pl.when(n_rows >= (g + 1) * LANES)
            def _run():
                for c in range(LANES):
                    scatter_copy(src_ref[g, c], g * LANES + c, s).start()

        def row(r, carry):
            scatter_copy(src_ref[r // LANES, r % LANES], r, s).start()
            return carry

        lax.fori_loop(n_rows // LANES * LANES, n_rows, row, 0)

    def by_slot(fn, s_of_slot):
        for par in (0, 1):
            pl.when(slot == par)(functools.partial(fn, s_of_slot(par)))

    def wait_scatter(n_rows, s):
        def waiter(rows):
            def body(i, carry):
                pltpu.make_async_copy(obuf.at[s, pl.ds(0, rows)], out_hbm.at[pl.ds(0, rows)], ssem.at[s]).wait()
                return carry
            return body

        lax.fori_loop(0, n_rows // LANES, waiter(LANES), 0)
        lax.fori_loop(0, n_rows % LANES // _SUBLANES, waiter(_SUBLANES), 0)
        lax.fori_loop(0, n_rows % _SUBLANES, waiter(1), 0)

    @pl.when(k == 0)
    def _first():
        start_gather(src_ref, 0)

    pltpu.make_async_copy(hx_hbm.at[pl.ds(0, tm)], gbuf.at[slot], gsem.at[slot]).wait()

    @pl.when(k + 1 < nk)
    def _prefetch():
        by_slot(lambda s: start_gather(srcn_ref, s), lambda par: 1 - par)

    @pl.when(k >= 2)
    def _free_obuf():
        wait_scatter(nv_ref[k - 2], slot)

    x = gbuf[slot]
    h = x[:, :d]
    comb = x[:, d:]
    hb = h.astype(BF16)
    lane = lax.broadcasted_iota(I32, comb.shape, 1)
    e0 = _R_EXP + tg_ref[k] * EXPERTS_PER_GROUP
    acc = jnp.zeros((tm, d), F32)
    for e in range(EXPERTS_PER_GROUP):
        a = _dot(hb, weg_ref[e])
        u = _dot(hb, weu_ref[e])
        cw = jnp.sum(jnp.where(lane == e0 + e, comb, 0.0), axis=1, keepdims=True)
        hid = (a / (1.0 + jnp.exp(-a))) * u * cw
        acc = acc + _dot(hid.astype(BF16), wed_ref[e])
    obuf[slot] = _ln(alpha * h + acc, g2_ref[...], b2_ref[...])

    nv = nv_ref[k]
    by_slot(lambda s: start_scatter(nv, s), lambda par: par)

    @pl.when(k == nk - 1)
    def _drain():
        wait_scatter(nv, slot)

        @pl.when(k >= 1)
        def _():
            wait_scatter(nv_ref[k - 1], 1 - slot)


def _moe(hx, tile_group, n_valid, src, weg, weu, wed, g2, b2, alpha, tm, n):
    d = hx.shape[1] - LANES
    nk = src.shape[0]
    ne, _, f = weg.shape
    gsz = EXPERTS_PER_GROUP
    grid_spec = pltpu.PrefetchScalarGridSpec(
        num_scalar_prefetch=2,
        grid=(nk,),
        in_specs=[pl.BlockSpec((None, tm // LANES, LANES), lambda k, tg, nv: (k, 0, 0),
                               memory_space=pltpu.SMEM),
                  pl.BlockSpec((None, tm // LANES, LANES),
                               lambda k, tg, nv: (jnp.minimum(k + 1, nk - 1), 0, 0), memory_space=pltpu.SMEM),
                  pl.BlockSpec(memory_space=pl.ANY),
                  pl.BlockSpec((gsz, d, f), lambda k, tg, nv: (tg[k], 0, 0)),
                  pl.BlockSpec((gsz, d, f), lambda k, tg, nv: (tg[k], 0, 0)),
                  pl.BlockSpec((gsz, f, d), lambda k, tg, nv: (tg[k], 0, 0)),
                  pl.BlockSpec(g2.shape, lambda k, tg, nv: (0, 0)),
                  pl.BlockSpec(b2.shape, lambda k, tg, nv: (0, 0))],
        out_specs=pl.BlockSpec(memory_space=pl.ANY),
        scratch_shapes=[pltpu.VMEM((2, tm, d + LANES), F32), pltpu.VMEM((2, tm, d), F32),
                        pltpu.SemaphoreType.DMA((2,)), pltpu.SemaphoreType.DMA((2,))])
    return pl.pallas_call(
        functools.partial(_moe_kernel, alpha=alpha, tm=tm, d=d),
        grid_spec=grid_spec,
        out_shape=jax.ShapeDtypeStruct((n, d), F32),
        compiler_params=pltpu.CompilerParams(dimension_semantics=("arbitrary",),
                                             vmem_limit_bytes=VMEM_LIMIT),
        name="moe",
    )(tile_group, n_valid, src, src, hx, weg, weu, wed, g2, b2)


def _group_tiles(gsel, tm):
    n = gsel.shape[0]
    n_tiles = n // tm + N_GROUPS
    onehot = (gsel[:, None] == jnp.arange(N_GROUPS, dtype=I32)[None, :]).astype(I32)
    csum = jnp.cumsum(onehot, axis=0)
    rank = jnp.sum(csum * onehot, axis=1) - 1
    counts = csum[-1]
    padded = (counts + tm - 1) // tm * tm
    ends = jnp.cumsum(padded)
    offs = ends - padded
    dest = offs[gsel] + rank
    src = jnp.zeros((n_tiles * tm,), I32).at[dest].set(jnp.arange(n, dtype=I32))
    tile_start = jnp.arange(n_tiles, dtype=I32) * tm
    tile_group = jnp.minimum(jnp.sum((tile_start[:, None] >= ends[None, :]).astype(I32), axis=1), N_GROUPS - 1)
    n_valid = jnp.clip(offs[tile_group] + counts[tile_group] - tile_start, 0, tm)
    return tile_group.astype(I32), n_valid.astype(I32), src.reshape(n_tiles, tm // LANES, LANES)


def _layout_weights(w_in, w_uq, w_uk, w_uv):
    d = w_in.shape[0]
    sizes = (MLA_Q_RANK, MLA_KV_RANK, MLA_ROPE, DSA_HEADS * DSA_HEAD_DIM, DSA_HEAD_DIM, DSA_HEAD_DIM,
             IDX_HEADS * IDX_DIM, IDX_DIM, IDX_HEADS)
    offs = np.cumsum((0,) + sizes)
    c_q, c_kv, k_rope, q_b, k_b, v_b, q_idx, k_idx, w_idx = (
        w_in[:, int(offs[i]):int(offs[i + 1])] for i in range(len(sizes)))
    half = MLA_ROPE // 2
    z = lambda w: jnp.zeros((d, w), w_in.dtype)
    pad = LANES - MLA_NOPE - MLA_ROPE
    grp_a = jnp.concatenate([w_idx, z(MLA_NOPE - IDX_HEADS), k_rope, z(pad)], axis=1)
    grp_b = jnp.concatenate([z(MLA_NOPE), k_rope[:, half:], k_rope[:, :half], z(pad)], axis=1)
    w1 = jnp.concatenate([c_q, c_kv, grp_a, grp_b, q_b * DSA_HEAD_DIM ** -0.5, k_b, k_b, v_b, v_b, q_idx,
                          k_idx, k_idx, k_idx, k_idx], axis=1)
    assert w1.shape[1] == _C_END

    dk = MLA_NOPE + MLA_ROPE
    rq = w_uq.shape[0]
    zq = lambda w: jnp.zeros((rq, w), w_uq.dtype)
    wq_parts, wqr_parts, wk_parts = [], [], []
    for h in range(MLA_HEADS):
        nope = w_uq[:, h * dk:h * dk + MLA_NOPE]
        rope = w_uq[:, h * dk + MLA_NOPE:(h + 1) * dk]
        wq_parts += [nope, rope, zq(pad)]
        wqr_parts += [zq(MLA_NOPE), rope[:, half:], rope[:, :half], zq(pad)]
        wk_parts += [w_uk[:, h * MLA_NOPE:(h + 1) * MLA_NOPE],
                     jnp.zeros((w_uk.shape[0], LANES - MLA_NOPE), w_uk.dtype)]
    wq = jnp.concatenate(wq_parts, axis=1)
    wqr = jnp.concatenate(wqr_parts, axis=1)
    wk = jnp.concatenate(wk_parts, axis=1)

    inv = ROPE_THETA ** (-np.arange(half, dtype=np.float32) / half)
    invf = np.zeros((1, LANES), np.float32)
    invf[0, MLA_NOPE:MLA_NOPE + half] = inv
    invf[0, MLA_NOPE + half:MLA_NOPE + MLA_ROPE] = inv
    sgn = np.zeros((1, LANES), np.float32)
    sgn[0, MLA_NOPE:MLA_NOPE + half] = -1.0
    sgn[0, MLA_NOPE + half:MLA_NOPE + MLA_ROPE] = 1.0
    return (w1.astype(BF16), wq.astype(BF16), wqr.astype(BF16), wk.astype(BF16), w_uv.astype(BF16),
            jnp.asarray(invf), jnp.asarray(sgn))


def kernel(x, positions, ln0_g, ln0_b, w_in, q_norm_g, w_uq, kv_norm_g, w_uk, w_uv, rel_bias, w_up_a, w_up_b,
           w_gate, b_gate, w_o, ln1_g, ln1_b, w_grp, b_grp, w_rt, b_rt, w_exp_gate, w_exp_up, w_exp_down,
           ln2_g, ln2_b):
    bsz, seq, d = x.shape
    depth = w_in.shape[0]
    assert depth == 1
    alpha = (2.0 * depth) ** 0.25
    n = bsz * seq
    topk = min(IDX_TOPK_MAX, seq // 4)
    r2 = lambda v: v.reshape(1, -1)

    x2 = x.reshape(n, d)
    pos2 = positions.reshape(n, 1)
    w1, wq, wqr, wk, wv, invf, sgn = _layout_weights(w_in[0], w_uq[0], w_uk[0], w_uv[0])

    blk = 256
    bt = _bias_tiles(rel_bias, blk)
    qm, km, vm, qb, kk, vv, qi, ki, wi = _in_proj(
        x2, pos2, r2(ln0_g), r2(ln0_b), w1, r2(q_norm_g[0]), wq, wqr, r2(kv_norm_g[0]), wk, wv, invf, sgn,
        tm=512)
    b3 = lambda a: a.reshape(bsz, seq, a.shape[1])
    o_a = _mla_attn(b3(qm), b3(km), _value_slabs(vm, bsz, seq, blk, MLA_HEADS, MLA_V), tq=blk, ck=blk)
    o_b = _dsa_attn(b3(qb), b3(qi), b3(wi).transpose(0, 2, 1), b3(kk),
                    _value_slabs(vv[:, :DSA_HEAD_DIM], bsz, seq, blk, 1, DSA_HEAD_DIM), b3(ki), bt,
                    tq=blk, topk=topk)

    wr = jnp.concatenate([w_grp[0], w_rt[0], jnp.zeros((d, LANES - N_GROUPS - N_EXPERTS), F32)], axis=1)
    br = jnp.concatenate([b_grp[0], b_rt[0], jnp.zeros((LANES - N_GROUPS - N_EXPERTS,), F32)])
    hx, gsel = _post_attn(
        x2, o_a.reshape(n, -1), o_b.reshape(n, -1), r2(ln0_g), r2(ln0_b), w_gate[0].astype(BF16), r2(b_gate[0]),
        w_up_a[0].astype(BF16), w_up_b[0].astype(BF16), w_o[0].astype(BF16), r2(ln1_g[0]), r2(ln1_b[0]),
        wr.astype(BF16), r2(br), alpha, tm=512)
    tm_moe = 512
    tile_group, n_valid, src = _group_tiles(gsel[:, 0], tm_moe)
    out = _moe(hx, tile_group, n_valid, src, w_exp_gate[0].astype(BF16), w_exp_up[0].astype(BF16),
               w_exp_down[0].astype(BF16), r2(ln2_g[0]), r2(ln2_b[0]), alpha, tm_moe, n)
    return out.reshape(bsz, seq, d)
```

```python
import functools
import math

import numpy as np
import jax
import jax.numpy as jnp
from jax import lax
from jax.experimental import pallas as pl
from jax.experimental.pallas import tpu as pltpu

F32 = jnp.float32
BF16 = jnp.bfloat16
I32 = jnp.int32
I16 = jnp.int16

CHUNK = 64
MLA_HEADS = 8
MLA_Q_RANK = 256
MLA_KV_RANK = 128
MLA_NOPE = 64
MLA_ROPE = 32
MLA_V = 64
ROPE_THETA = 10000.0
DSA_HEADS = 8
DSA_HEAD_DIM = 64
IDX_HEADS = 8
IDX_DIM = 32
IDX_TOPK_MAX = 256
REL_BUCKETS = 32
REL_MAX_DIST = 128
N_GROUPS = 4
EXPERTS_PER_GROUP = 8
N_EXPERTS = N_GROUPS * EXPERTS_PER_GROUP
EXPERT_HIDDEN = 256
LN_EPS = 1e-5
RMS_EPS = 1e-6
NEG = -1e30

LANES = 128
_SUBLANES = 8
_VROWS = 80
VMEM_LIMIT = 48 * 1024 * 1024

_KEY_NEG = int(np.array(NEG, np.float32).view(np.int32)) ^ 0x7FFFFFFF
_I16_MIN = -(2 ** 15)
_LOG2E = math.log2(math.e)
_CHUNK_SHIFT = CHUNK.bit_length() - 1
assert 1 << _CHUNK_SHIFT == CHUNK


def _ln(x, g, b):
    mu = jnp.mean(x, axis=-1, keepdims=True)
    xc = x - mu
    var = jnp.mean(xc * xc, axis=-1, keepdims=True)
    return xc * lax.rsqrt(var + LN_EPS) * g + b


def _rms(x, g):
    return x * lax.rsqrt(jnp.mean(x * x, axis=-1, keepdims=True) + RMS_EPS) * g


def _dot(a, b):
    return jnp.dot(a, b, preferred_element_type=F32)


def _dot_nt(a, b):
    return lax.dot_general(a, b, (((1,), (1,)), ((), ())), preferred_element_type=F32)


def _bias_kernel(rb_ref, o_ref):
    nb = REL_BUCKETS // 2
    max_exact = nb // 2
    blk = o_ref.shape[2]
    s = lax.broadcasted_iota(I32, (blk, blk), 0)
    t = lax.broadcasted_iota(I32, (blk, blk), 1)
    for which, off in ((0, -blk), (1, 0)):
        rel = s - t + off
        ret = jnp.where(rel > 0, nb, 0)
        n = jnp.abs(rel)
        large = max_exact + (jnp.log(jnp.maximum(n, 1).astype(F32) / max_exact)
                             / math.log(REL_MAX_DIST / max_exact) * (nb - max_exact)).astype(I32)
        large = jnp.minimum(large, nb - 1)
        bucket = ret + jnp.where(n < max_exact, n, large)
        for h in range(DSA_HEADS):
            acc = jnp.zeros((blk, blk), F32)
            for b in range(REL_BUCKETS):
                acc = jnp.where(bucket == b, rb_ref[b, h], acc)
            o_ref[which, h] = (acc - rb_ref[nb - 1, h]) * _LOG2E


def _bias_tiles(rel_bias, blk):
    assert blk >= REL_MAX_DIST
    return pl.pallas_call(
        _bias_kernel,
        out_shape=jax.ShapeDtypeStruct((2, DSA_HEADS, blk, blk), F32),
        in_specs=[pl.BlockSpec(memory_space=pltpu.SMEM)],
        out_specs=pl.BlockSpec(memory_space=pltpu.VMEM),
        name="bias_tiles",
    )(rel_bias)


_C_Q, _C_KV, _C_RA, _C_RB, _C_QB, _C_KK, _C_VV, _C_QI, _C_KI, _C_END = (
    0, 256, 384, 512, 640, 1152, 1280, 1408, 1664, 1792)


def _value_rows(vt, dv):
    t = vt.shape[1]
    return jnp.concatenate([vt, jnp.ones((1, t), F32), jnp.zeros((_VROWS - dv - 1, t), F32)], axis=0)


def _in_proj_kernel(x_ref, pos_ref, g0_ref, b0_ref, w1_ref, qg_ref, wq_ref, wqr_ref, kg_ref, wk_ref,
                    wvt_ref, one_ref, invf_ref, sgn_ref, cb_ref, sb_ref,
                    qm_ref, km_ref, vmt_ref, qb_ref, kk_ref, vvt_ref, qi_ref, ki_ref, wt_ref, *, blk):
    tm = x_ref.shape[0]
    xn = _ln(x_ref[...], g0_ref[...], b0_ref[...])
    proj = _dot(xn.astype(BF16), w1_ref[...])
    qb_ref[...] = (proj[:, _C_QB:_C_KK] * _LOG2E).astype(BF16)
    kk_ref[...] = proj[:, _C_KK:_C_VV].astype(BF16)
    qi_ref[...] = proj[:, _C_QI:_C_KI].astype(BF16)
    ki_ref[...] = proj[:, _C_KI:_C_END].astype(BF16)
    ga = proj[:, _C_RA:_C_RB]
    gb = proj[:, _C_RB:_C_QB]
    wt_ref[...] = ga.T[0:IDX_HEADS, :] * (IDX_HEADS ** -0.5 * IDX_DIM ** -0.5)
    vvt = _value_rows(proj[:, _C_VV:_C_QI].T[0:DSA_HEAD_DIM, :], DSA_HEAD_DIM).astype(BF16)
    for c in range(tm // blk):
        vvt_ref[c] = vvt[:, c * blk:(c + 1) * blk]

    groups = tm // CHUNK
    a = (pos_ref[0:1, :] >> _CHUNK_SHIFT) + lax.broadcasted_iota(I32, (groups, 1), 0)
    ang = (a * CHUNK).astype(F32) * invf_ref[...]
    ca, sa = jnp.cos(ang), jnp.sin(ang)
    cb, sb = cb_ref[...], sb_ref[...]
    cosv = jnp.concatenate([ca[g:g + 1] * cb - sa[g:g + 1] * sb for g in range(groups)], axis=0)
    sinv = jnp.concatenate([sa[g:g + 1] * cb + ca[g:g + 1] * sb for g in range(groups)], axis=0) * sgn_ref[...]
    lane = lax.broadcasted_iota(I32, cosv.shape, 1)
    rope_lane = (lane >= MLA_NOPE) & (lane < MLA_NOPE + MLA_ROPE)
    kr = jnp.where(rope_lane, ga * cosv + gb * sinv, 0.0)

    scale = (MLA_NOPE + MLA_ROPE) ** -0.5 * _LOG2E
    cos8 = jnp.concatenate([cosv * scale] * MLA_HEADS, axis=1)
    sin8 = jnp.concatenate([sinv * scale] * MLA_HEADS, axis=1)
    cqn = _rms(proj[:, _C_Q:_C_KV], qg_ref[...]).astype(BF16)
    q = _dot(cqn, wq_ref[...]) * cos8 + _dot(cqn, wqr_ref[...]) * sin8
    qm_ref[...] = q.astype(BF16)

    ckn = _rms(proj[:, _C_KV:_C_RA], kg_ref[...]).astype(BF16)
    k = _dot(ckn, wk_ref[...]) + jnp.concatenate([kr] * MLA_HEADS, axis=1)
    km_ref[...] = k.astype(BF16)
    vmt = (_dot_nt(wvt_ref[...], ckn) + one_ref[...]).astype(BF16)
    for c in range(tm // blk):
        vmt_ref[c] = vmt[:, c * blk:(c + 1) * blk]


def _in_proj(x2, pos2, g0, b0, w1, qg, wq, wqr, kg, wk, wvt, one, invf, sgn, cb, sb, tm, blk):
    n, d = x2.shape
    assert tm % blk == 0 and blk % CHUNK == 0
    hm = MLA_HEADS * LANES
    row = lambda w: pl.BlockSpec((tm, w), lambda i: (i, 0))
    full = lambda a: pl.BlockSpec(a.shape, lambda i: (0,) * a.ndim)
    slab = lambda r: pl.BlockSpec((tm // blk, r, blk), lambda i: (i, 0, 0))
    tok = lambda w, dt: (jax.ShapeDtypeStruct((n, w), dt), row(w))
    slb = lambda r: (jax.ShapeDtypeStruct((n // blk, r, blk), BF16), slab(r))
    outs = [tok(hm, BF16), tok(hm, BF16), slb(MLA_HEADS * _VROWS), tok(DSA_HEADS * DSA_HEAD_DIM, BF16),
            tok(LANES, BF16), slb(_VROWS), tok(IDX_HEADS * IDX_DIM, BF16), tok(LANES, BF16),
            (jax.ShapeDtypeStruct((IDX_HEADS, n), F32), pl.BlockSpec((IDX_HEADS, tm), lambda i: (0, i)))]
    consts = (g0, b0, w1, qg, wq, wqr, kg, wk, wvt, one, invf, sgn, cb, sb)
    return pl.pallas_call(
        functools.partial(_in_proj_kernel, blk=blk),
        grid=(n // tm,),
        in_specs=[row(d), row(1)] + [full(a) for a in consts],
        out_specs=[s for _, s in outs],
        out_shape=[s for s, _ in outs],
        compiler_params=pltpu.CompilerParams(dimension_semantics=("parallel",),
                                             vmem_limit_bytes=VMEM_LIMIT),
        name="in_proj",
    )(x2, pos2, *consts)


def _admissible(ck, tq, k0, q0):
    s = k0 + lax.broadcasted_iota(I32, (ck, tq), 0)
    t = q0 + lax.broadcasted_iota(I32, (ck, tq), 1)
    return (s >> _CHUNK_SHIFT) <= (t >> _CHUNK_SHIFT)


def _softmax_step(s, vt, m_ref, acc_ref, h):
    sb = s.astype(BF16)
    m_prev = m_ref[h]
    m_new = jnp.maximum(m_prev, jnp.max(sb, axis=0, keepdims=True).astype(F32))
    alpha = jnp.exp2(m_prev - m_new)
    p = jnp.exp2(sb - m_new.astype(BF16))
    acc_ref[h] = alpha * acc_ref[h] + _dot(vt, p)
    m_ref[h] = m_new


def _scores_first(s_ref, nh, produce, consume):
    for h in range(nh):
        s_ref[h] = produce(h)
    for h in range(nh):
        consume(h, s_ref[h])


def _finish(o_ref, acc_ref, nh, dv):
    ot = jnp.concatenate([acc_ref[h, :dv, :] / acc_ref[h, dv:dv + 1, :] for h in range(nh)], axis=0)
    o_ref[...] = ot.T.astype(BF16)


def _mla_kernel(q_ref, k_ref, vt_ref, o_ref, s_ref, m_ref, acc_ref, *, tq, ck):
    j = pl.program_id(1)
    q0 = j * tq
    m_ref[...] = jnp.full(m_ref.shape, NEG, F32)
    acc_ref[...] = jnp.zeros(acc_ref.shape, F32)
    nfull = (q0 + CHUNK) // ck

    def chunk(c, masked):
        k0 = pl.multiple_of(c * ck, ck)
        adm = _admissible(ck, tq, k0, q0) if masked else None

        def score(h):
            return _dot_nt(k_ref[pl.ds(k0, ck), h * LANES:(h + 1) * LANES], q_ref[:, h * LANES:(h + 1) * LANES])

        def use(h, s):
            if masked:
                s = jnp.where(adm, s, NEG)
            _softmax_step(s, vt_ref[c, h * _VROWS:(h + 1) * _VROWS, :], m_ref, acc_ref, h)

        _scores_first(s_ref, MLA_HEADS, score, use)

    def full_body(c, carry):
        chunk(c, False)
        return carry

    lax.fori_loop(0, nfull, full_body, 0)
    chunk(nfull, True)
    _finish(o_ref, acc_ref, MLA_HEADS, MLA_V)


def _mla_attn(qm, km, vmt, tq, ck):
    b, t, hm = qm.shape
    dv = MLA_HEADS * MLA_V
    assert ck % tq == 0 and tq > CHUNK and tq % CHUNK == 0 and t % ck == 0
    return pl.pallas_call(
        functools.partial(_mla_kernel, tq=tq, ck=ck),
        grid=(b, t // tq),
        in_specs=[pl.BlockSpec((None, tq, hm), lambda i, j: (i, j, 0)),
                  pl.BlockSpec((None, t, hm), lambda i, j: (i, 0, 0)),
                  pl.BlockSpec((None, t // ck, MLA_HEADS * _VROWS, ck), lambda i, j: (i, 0, 0, 0))],
        out_specs=pl.BlockSpec((None, tq, dv), lambda i, j: (i, j, 0)),
        out_shape=jax.ShapeDtypeStruct((b, t, dv), BF16),
        scratch_shapes=[pltpu.VMEM((MLA_HEADS, ck, tq), F32), pltpu.VMEM((MLA_HEADS, 1, tq), F32),
                        pltpu.VMEM((MLA_HEADS, _VROWS, tq), F32)],
        compiler_params=pltpu.CompilerParams(dimension_semantics=("parallel", "arbitrary"),
                                             vmem_limit_bytes=VMEM_LIMIT),
        name="mla_attn",
    )(qm, km, vmt)


def _dsa_kernel(qb_ref, qi_ref, wt_ref, kk_ref, vt_ref, ki_ref, bt_ref, o_ref,
                qs_ref, qim_ref, keys_ref, khi_ref, klo_ref, kls_ref, s_ref, thr_ref, cut_ref, m_ref, acc_ref,
                *, tq, topk):
    nh = DSA_HEADS
    ck = tq
    seq = kk_ref.shape[0]
    j = pl.program_id(1)
    q0 = j * tq
    nck = j + 1
    lane = lax.broadcasted_iota(I32, (tq, LANES), 1)

    for h in range(nh):
        pair = qb_ref[:, (h // 2) * LANES:(h // 2 + 1) * LANES].astype(F32)
        in_head = (lane >= (h % 2) * DSA_HEAD_DIM) & (lane < (h % 2 + 1) * DSA_HEAD_DIM)
        qs_ref[h * tq:(h + 1) * tq, :] = jnp.where(in_head, pair, 0.0).astype(BF16)
        grp = qi_ref[:, (h // 4) * LANES:(h // 4 + 1) * LANES].astype(F32)
        in_head = (lane >= (h % 4) * IDX_DIM) & (lane < (h % 4 + 1) * IDX_DIM)
        qim_ref[h * tq:(h + 1) * tq, :] = jnp.where(in_head, grp, 0.0).astype(BF16)

    w = wt_ref[...]

    def idx_chunk(c, masked):
        k0 = pl.multiple_of(c * ck, ck)
        kc = ki_ref[pl.ds(k0, ck), :]
        total = [jnp.zeros((ck, tq), F32)]

        def add(h, r):
            total[0] = total[0] + w[h:h + 1, :] * jnp.maximum(r, 0.0)

        _scores_first(s_ref, nh, lambda h: _dot_nt(kc, qim_ref[h * tq:(h + 1) * tq, :]), add)
        score = total[0]
        if masked:
            score = jnp.where(_admissible(ck, tq, k0, q0), score, NEG)
        score = jnp.where(score == 0.0, 0.0, score)
        bits = lax.bitcast_convert_type(score, I32)
        key = bits ^ ((bits >> 31) & 0x7FFFFFFF)
        keys_ref[c] = key
        khi_ref[c] = (key >> 16).astype(I16)
        klo_ref[c] = ((key & 0xFFFF) - 0x8000).astype(I16)

    def idx_body(c, carry):
        idx_chunk(c, False)
        return carry

    lax.fori_loop(0, j, idx_body, 0)
    idx_chunk(j, True)

    thr_ref[...] = jnp.full(thr_ref.shape, _KEY_NEG + 1, I32)
    cut_ref[...] = jnp.full(cut_ref.shape, seq, I32)

    def count_keys(pred):
        def body(c, acc):
            part = jnp.where(pred(c), 1.0, 0.0).reshape(ck // 8, 8, tq)
            while part.shape[0] > 1:
                half = part.shape[0] // 2
                part = part[:half] + part[half:]
            return acc + part[0]
        acc = lax.fori_loop(0, nck, body, jnp.zeros((8, tq), F32))
        return jnp.sum(acc, axis=0, keepdims=True)

    def count16(ref, cand):
        rows = 2 * _SUBLANES

        def hits(c):
            part = jnp.where(ref[c] >= cand, jnp.int16(1), jnp.int16(0)).reshape(ck // rows, rows, tq)
            while part.shape[0] > 1:
                half = part.shape[0] // 2
                part = part[:half] + part[half:]
            return part[0]

        acc = lax.fori_loop(0, nck // 2, lambda i, a: a + hits(2 * i) + hits(2 * i + 1),
                            jnp.zeros((rows, tq), I16))
        acc = lax.fori_loop(nck // 2 * 2, nck, lambda c, a: a + hits(c), acc)
        return jnp.sum(acc.astype(F32), axis=0, keepdims=True)

    def greedy16(ref, need):
        def step(i, carry):
            v, cge, cgt = carry
            cand = v + lax.shift_left(jnp.int32(1), 15 - i)
            cnt = count16(ref, cand.astype(I16))
            ok = cnt >= need
            return jnp.where(ok, cand, v), jnp.where(ok, cnt, cge), jnp.where(ok, cgt, cnt)

        v0 = jnp.full((1, tq), _I16_MIN, I32)
        cge0 = jnp.full((1, tq), 1.0, F32) * (nck * ck).astype(F32)
        return lax.fori_loop(0, 16, step, (v0, cge0, jnp.zeros((1, tq), F32)))

    @pl.when(q0 + tq > topk)
    def _search():
        hi, cge_hi, cgt_hi = greedy16(khi_ref, float(topk))
        hi16 = hi.astype(I16)

        def mask_body(c, carry):
            kls_ref[c] = jnp.where(khi_ref[c] == hi16, klo_ref[c], jnp.int16(_I16_MIN))
            return carry

        lax.fori_loop(0, nck, mask_body, 0)
        lo, cge_lo, _ = greedy16(kls_ref, float(topk) - cgt_hi)
        thr = hi * 65536 + (lo + 0x8000)
        thr_ref[...] = thr
        cge = cgt_hi + jnp.where(lo == _I16_MIN, cge_hi - cgt_hi, cge_lo)
        excess = cge - float(topk)

        @pl.when(jnp.max(excess) > 0.0)
        def _ties():
            nbits = max(1, int(seq - 1).bit_length())

            def tstep(i, cut):
                cand = cut + lax.shift_left(jnp.int32(1), nbits - 1 - i)

                def pred(c):
                    idx = c * ck + lax.broadcasted_iota(I32, (ck, tq), 0)
                    return (keys_ref[c] == thr) & (idx >= cand)

                return jnp.where(count_keys(pred) >= excess, cand, cut)

            cut = lax.fori_loop(0, nbits, tstep, jnp.zeros((1, tq), I32))
            cut_ref[...] = jnp.where(excess > 0.0, cut, seq)

    m_ref[...] = jnp.full(m_ref.shape, NEG, F32)
    acc_ref[...] = jnp.zeros(acc_ref.shape, F32)
    thr = thr_ref[...]
    cut = cut_ref[...]

    def att_chunk(c, bias_slab, own):
        k0 = pl.multiple_of(c * ck, ck)
        kc = kk_ref[pl.ds(k0, ck), :]
        key = keys_ref[c]
        idx = k0 + lax.broadcasted_iota(I32, (ck, tq), 0)
        sel = (key > thr) | ((key == thr) & (idx < cut))
        if own:
            sel = sel & _admissible(ck, tq, k0, q0)
        vt = vt_ref[c]

        def use(h, s):
            if bias_slab is not None:
                s = s + bt_ref[bias_slab, h]
            _softmax_step(jnp.where(sel, s, NEG), vt, m_ref, acc_ref, h)

        _scores_first(s_ref, nh, lambda h: _dot_nt(kc, qs_ref[h * tq:(h + 1) * tq, :]), use)

    def far_body(c, carry):
        att_chunk(c, None, False)
        return carry

    lax.fori_loop(0, j - 1, far_body, 0)

    @pl.when(j > 0)
    def _prev():
        att_chunk(j - 1, 0, False)

    att_chunk(j, 1, True)
    _finish(o_ref, acc_ref, nh, DSA_HEAD_DIM)


def _dsa_attn(qb, qi, wt, kk, vt, ki, bt, tq, topk):
    b, t, _ = qb.shape
    ck = tq
    assert tq == bt.shape[2] and tq % LANES == 0 and t % tq == 0
    nh = DSA_HEADS
    blk = lambda a: pl.BlockSpec((None, tq, a.shape[2]), lambda i, j: (i, j, 0))
    seqb = lambda a: pl.BlockSpec((None, t, a.shape[2]), lambda i, j: (i, 0, 0))
    return pl.pallas_call(
        functools.partial(_dsa_kernel, tq=tq, topk=topk),
        grid=(b, t // tq),
        in_specs=[blk(qb), blk(qi), pl.BlockSpec((nh, tq), lambda i, j: (0, i * (t // tq) + j)), seqb(kk),
                  pl.BlockSpec((None, t // ck, _VROWS, ck), lambda i, j: (i, 0, 0, 0)), seqb(ki),
                  pl.BlockSpec(bt.shape, lambda i, j: (0, 0, 0, 0))],
        out_specs=blk(qb),
        out_shape=jax.ShapeDtypeStruct(qb.shape, BF16),
        scratch_shapes=[pltpu.VMEM((nh * tq, LANES), BF16), pltpu.VMEM((nh * tq, LANES), BF16),
                        pltpu.VMEM((t // ck, ck, tq), I32), pltpu.VMEM((t // ck, ck, tq), I16),
                        pltpu.VMEM((t // ck, ck, tq), I16), pltpu.VMEM((t // ck, ck, tq), I16),
                        pltpu.VMEM((nh, ck, tq), F32),
                        pltpu.VMEM((1, tq), I32),
                        pltpu.VMEM((1, tq), I32), pltpu.VMEM((nh, 1, tq), F32),
                        pltpu.VMEM((nh, _VROWS, tq), F32)],
        compiler_params=pltpu.CompilerParams(dimension_semantics=("parallel", "arbitrary"),
                                             vmem_limit_bytes=VMEM_LIMIT),
        name="dsa_attn",
    )(qb, qi, wt, kk, vt, ki, bt)


_R_GRP = 0
_R_EXP = N_GROUPS


def _post_kernel(x_ref, oa_ref, ob_ref, g0_ref, b0_ref, wg_ref, bg_ref, wua_ref, wub_ref, wo_ref,
                 g1_ref, b1_ref, wr_ref, br_ref, hx_ref, gsel_ref, *, alpha):
    d = x_ref.shape[1]
    xn = _ln(x_ref[...], g0_ref[...], b0_ref[...])
    z = _dot(xn.astype(BF16), wg_ref[...]) + bg_ref[...]
    gates = 1.0 / (1.0 + jnp.exp(-z))
    mix = gates[:, :d] * _dot(oa_ref[...], wua_ref[...]) + gates[:, d:] * _dot(ob_ref[...], wub_ref[...])
    mixed = _dot(mix.astype(BF16), wo_ref[...])
    h = _ln(alpha * xn + mixed, g1_ref[...], b1_ref[...])
    hx_ref[:, :d] = h

    logits = _dot(h.astype(BF16), wr_ref[...]) + br_ref[...]
    lane = lax.broadcasted_iota(I32, logits.shape, 1).astype(F32)
    gmask = lane < N_GROUPS
    gl = jnp.where(gmask, logits, -jnp.inf)
    gmax = jnp.max(gl, axis=1, keepdims=True)
    g_sel = jnp.min(jnp.where(gl == gmax, lane, float(LANES)), axis=1, keepdims=True)
    g_w = 1.0 / jnp.sum(jnp.where(gmask, jnp.exp(gl - gmax), 0.0), axis=1, keepdims=True)
    e_lo = _R_EXP + g_sel * EXPERTS_PER_GROUP
    el = jnp.where((lane >= e_lo) & (lane < e_lo + EXPERTS_PER_GROUP), logits, -jnp.inf)
    m1 = jnp.max(el, axis=1, keepdims=True)
    i1 = jnp.min(jnp.where(el == m1, lane, float(LANES)), axis=1, keepdims=True)
    el2 = jnp.where(lane == i1, -jnp.inf, el)
    m2 = jnp.max(el2, axis=1, keepdims=True)
    i2 = jnp.min(jnp.where(el2 == m2, lane, float(LANES)), axis=1, keepdims=True)
    e21 = jnp.exp(m2 - m1)
    w1 = g_w / (1.0 + e21)
    w2 = g_w * e21 / (1.0 + e21)
    hx_ref[:, d:] = jnp.where(lane == i1, w1, 0.0) + jnp.where(lane == i2, w2, 0.0)
    gsel_ref[...] = g_sel.astype(I32)


def _post_attn(x2, oa, ob, g0, b0, wg, bg, wua, wub, wo, g1, b1, wr, br, alpha, tm):
    n, d = x2.shape
    row = lambda w: pl.BlockSpec((tm, w), lambda i: (i, 0))
    full = lambda a: pl.BlockSpec(a.shape, lambda i: (0,) * a.ndim)
    return pl.pallas_call(
        functools.partial(_post_kernel, alpha=alpha),
        grid=(n // tm,),
        in_specs=[row(d), row(oa.shape[1]), row(ob.shape[1])]
                 + [full(a) for a in (g0, b0, wg, bg, wua, wub, wo, g1, b1, wr, br)],
        out_specs=[row(d + LANES), row(1)],
        out_shape=[jax.ShapeDtypeStruct((n, d + LANES), F32), jax.ShapeDtypeStruct((n, 1), I32)],
        compiler_params=pltpu.CompilerParams(dimension_semantics=("parallel",),
                                             vmem_limit_bytes=VMEM_LIMIT),
        name="post_attn",
    )(x2, oa, ob, g0, b0, wg, bg, wua, wub, wo, g1, b1, wr, br)


def _moe_kernel(tg_ref, nv_ref, src_ref, srcn_ref, hx_hbm, weg_ref, weu_ref, wed_ref, g2_ref, b2_ref,
                out_hbm, gbuf, obuf, gsem, ssem, *, alpha, tm, d):
    k = pl.program_id(0)
    nk = pl.num_programs(0)
    slot = k % 2
    nrun = tm // LANES

    def gather_copy(tok, r, s):
        return pltpu.make_async_copy(hx_hbm.at[pl.ds(tok, 1)], gbuf.at[s, pl.ds(r, 1)], gsem.at[s])

    def scatter_copy(tok, r, s):
        return pltpu.make_async_copy(obuf.at[s, pl.ds(r, 1)], out_hbm.at[pl.ds(tok, 1)], ssem.at[s])

    def start_gather(tile_src_ref, s):
        for g in range(nrun):
            for c in range(LANES):
                gather_copy(tile_src_ref[g, c], g * LANES + c, s).start()

    def start_scatter(n_rows, s):
        for g in range(nrun):
            @pl.when(n_rows >= (g + 1) * LANES)
            def _run():
                for c in range(LANES):
                    scatter_copy(src_ref[g, c], g * LANES + c, s).start()

        def row(r, carry):
            scatter_copy(src_ref[r // LANES, r % LANES], r, s).start()
            return carry

        lax.fori_loop(n_rows // LANES * LANES, n_rows, row, 0)

    def by_slot(fn, s_of_slot):
        for par in (0, 1):
            pl.when(slot == par)(functools.partial(fn, s_of_slot(par)))

    def wait_scatter(n_rows, s):
        def waiter(rows):
            def body(i, carry):
                pltpu.make_async_copy(obuf.at[s, pl.ds(0, rows)], out_hbm.at[pl.ds(0, rows)], ssem.at[s]).wait()
                return carry
            return body

        lax.fori_loop(0, n_rows // LANES, waiter(LANES), 0)
        lax.fori_loop(0, n_rows % LANES // _SUBLANES, waiter(_SUBLANES), 0)
        lax.fori_loop(0, n_rows % _SUBLANES, waiter(1), 0)

    @pl.when(k == 0)
    def _first():
        start_gather(src_ref, 0)

    pltpu.make_async_copy(hx_hbm.at[pl.ds(0, tm)], gbuf.at[slot], gsem.at[slot]).wait()

    @pl.when(k + 1 < nk)
    def _prefetch():
        by_slot(lambda s: start_gather(srcn_ref, s), lambda par: 1 - par)

    @pl.when(k >= 2)
    def _free_obuf():
        wait_scatter(nv_ref[k - 2], slot)

    x = gbuf[slot]
    h = x[:, :d]
    comb = x[:, d:]
    hb = h.astype(BF16)
    lane = lax.broadcasted_iota(I32, comb.shape, 1)
    e0 = _R_EXP + tg_ref[k] * EXPERTS_PER_GROUP
    acc = jnp.zeros((tm, d), F32)
    for e in range(EXPERTS_PER_GROUP):
        a = _dot(hb, weg_ref[e])
        u = _dot(hb, weu_ref[e])
        cw = jnp.sum(jnp.where(lane == e0 + e, comb, 0.0), axis=1, keepdims=True)
        hid = (a / (1.0 + jnp.exp(-a))) * u * cw
        acc = acc + _dot(hid.astype(BF16), wed_ref[e])
    obuf[slot] = _ln(alpha * h + acc, g2_ref[...], b2_ref[...])

    nv = nv_ref[k]
    by_slot(lambda s: start_scatter(nv, s), lambda par: par)

    @pl.when(k == nk - 1)
    def _drain():
        wait_scatter(nv, slot)

        @pl.when(k >= 1)
        def _():
            wait_scatter(nv_ref[k - 1], 1 - slot)


def _moe(hx, tile_group, n_valid, src, weg, weu, wed, g2, b2, alpha, tm, n):
    d = hx.shape[1] - LANES
    nk = src.shape[0]
    ne, _, f = weg.shape
    gsz = EXPERTS_PER_GROUP
    grid_spec = pltpu.PrefetchScalarGridSpec(
        num_scalar_prefetch=2,
        grid=(nk,),
        in_specs=[pl.BlockSpec((None, tm // LANES, LANES), lambda k, tg, nv: (k, 0, 0),
                               memory_space=pltpu.SMEM),
                  pl.BlockSpec((None, tm // LANES, LANES),
                               lambda k, tg, nv: (jnp.minimum(k + 1, nk - 1), 0, 0), memory_space=pltpu.SMEM),
                  pl.BlockSpec(memory_space=pl.ANY),
                  pl.BlockSpec((gsz, d, f), lambda k, tg, nv: (tg[k], 0, 0)),
                  pl.BlockSpec((gsz, d, f), lambda k, tg, nv: (tg[k], 0, 0)),
                  pl.BlockSpec((gsz, f, d), lambda k, tg, nv: (tg[k], 0, 0)),
                  pl.BlockSpec(g2.shape, lambda k, tg, nv: (0, 0)),
                  pl.BlockSpec(b2.shape, lambda k, tg, nv: (0, 0))],
        out_specs=pl.BlockSpec(memory_space=pl.ANY),
        scratch_shapes=[pltpu.VMEM((2, tm, d + LANES), F32), pltpu.VMEM((2, tm, d), F32),
                        pltpu.SemaphoreType.DMA((2,)), pltpu.SemaphoreType.DMA((2,))])
    return pl.pallas_call(
        functools.partial(_moe_kernel, alpha=alpha, tm=tm, d=d),
        grid_spec=grid_spec,
        out_shape=jax.ShapeDtypeStruct((n, d), F32),
        compiler_params=pltpu.CompilerParams(dimension_semantics=("arbitrary",),
                                             vmem_limit_bytes=VMEM_LIMIT),
        name="moe",
    )(tile_group, n_valid, src, src, hx, weg, weu, wed, g2, b2)


def _group_tiles(gsel, tm):
    n = gsel.shape[0]
    n_tiles = n // tm + N_GROUPS
    onehot = (gsel[:, None] == jnp.arange(N_GROUPS, dtype=I32)[None, :]).astype(I32)
    csum = jnp.cumsum(onehot, axis=0)
    rank = jnp.sum(csum * onehot, axis=1) - 1
    counts = csum[-1]
    padded = (counts + tm - 1) // tm * tm
    ends = jnp.cumsum(padded)
    offs = ends - padded
    dest = offs[gsel] + rank
    src = jnp.zeros((n_tiles * tm,), I32).at[dest].set(jnp.arange(n, dtype=I32))
    tile_start = jnp.arange(n_tiles, dtype=I32) * tm
    tile_group = jnp.minimum(jnp.sum((tile_start[:, None] >= ends[None, :]).astype(I32), axis=1), N_GROUPS - 1)
    n_valid = jnp.clip(offs[tile_group] + counts[tile_group] - tile_start, 0, tm)
    return tile_group.astype(I32), n_valid.astype(I32), src.reshape(n_tiles, tm // LANES, LANES)


def _layout_weights(w_in, w_uq, w_uk, w_uv):
    d = w_in.shape[0]
    sizes = (MLA_Q_RANK, MLA_KV_RANK, MLA_ROPE, DSA_HEADS * DSA_HEAD_DIM, DSA_HEAD_DIM, DSA_HEAD_DIM,
             IDX_HEADS * IDX_DIM, IDX_DIM, IDX_HEADS)
    offs = np.cumsum((0,) + sizes)
    c_q, c_kv, k_rope, q_b, k_b, v_b, q_idx, k_idx, w_idx = (
        w_in[:, int(offs[i]):int(offs[i + 1])] for i in range(len(sizes)))
    half = MLA_ROPE // 2
    z = lambda w: jnp.zeros((d, w), w_in.dtype)
    pad = LANES - MLA_NOPE - MLA_ROPE
    grp_a = jnp.concatenate([w_idx, z(MLA_NOPE - IDX_HEADS), k_rope, z(pad)], axis=1)
    grp_b = jnp.concatenate([z(MLA_NOPE), k_rope[:, half:], k_rope[:, :half], z(pad)], axis=1)
    w1 = jnp.concatenate([c_q, c_kv, grp_a, grp_b, q_b * DSA_HEAD_DIM ** -0.5, k_b, k_b, v_b, v_b, q_idx,
                          k_idx, k_idx, k_idx, k_idx], axis=1)
    assert w1.shape[1] == _C_END

    dk = MLA_NOPE + MLA_ROPE
    rq = w_uq.shape[0]
    zq = lambda w: jnp.zeros((rq, w), w_uq.dtype)
    wq_parts, wqr_parts, wk_parts = [], [], []
    for h in range(MLA_HEADS):
        nope = w_uq[:, h * dk:h * dk + MLA_NOPE]
        rope = w_uq[:, h * dk + MLA_NOPE:(h + 1) * dk]
        wq_parts += [nope, rope, zq(pad)]
        wqr_parts += [zq(MLA_NOPE), rope[:, half:], rope[:, :half], zq(pad)]
        wk_parts += [w_uk[:, h * MLA_NOPE:(h + 1) * MLA_NOPE],
                     jnp.zeros((w_uk.shape[0], LANES - MLA_NOPE), w_uk.dtype)]
    wq = jnp.concatenate(wq_parts, axis=1)
    wqr = jnp.concatenate(wqr_parts, axis=1)
    wk = jnp.concatenate(wk_parts, axis=1)

    inv = ROPE_THETA ** (-np.arange(half, dtype=np.float32) / half)
    invf = np.zeros((1, LANES), np.float32)
    invf[0, MLA_NOPE:MLA_NOPE + half] = inv
    invf[0, MLA_NOPE + half:MLA_NOPE + MLA_ROPE] = inv
    sgn = np.zeros((1, LANES), np.float32)
    sgn[0, MLA_NOPE:MLA_NOPE + half] = -1.0
    sgn[0, MLA_NOPE + half:MLA_NOPE + MLA_ROPE] = 1.0
    ang_b = np.arange(CHUNK, dtype=np.float64)[:, None] * invf.astype(np.float64)
    cb, sb = np.cos(ang_b).astype(np.float32), np.sin(ang_b).astype(np.float32)

    kvr = w_uv.shape[0]
    wvt = jnp.concatenate(
        [jnp.concatenate([w_uv[:, h * MLA_V:(h + 1) * MLA_V].T, jnp.zeros((_VROWS - MLA_V, kvr), w_uv.dtype)], axis=0)
         for h in range(MLA_HEADS)], axis=0)
    one = np.zeros((MLA_HEADS * _VROWS, 1), np.float32)
    one[MLA_V::_VROWS, 0] = 1.0
    return (w1.astype(BF16), wq.astype(BF16), wqr.astype(BF16), wk.astype(BF16), wvt.astype(BF16),
            jnp.asarray(one), jnp.asarray(invf), jnp.asarray(sgn), jnp.asarray(cb), jnp.asarray(sb))


def kernel(x, positions, ln0_g, ln0_b, w_in, q_norm_g, w_uq, kv_norm_g, w_uk, w_uv, rel_bias, w_up_a, w_up_b,
           w_gate, b_gate, w_o, ln1_g, ln1_b, w_grp, b_grp, w_rt, b_rt, w_exp_gate, w_exp_up, w_exp_down,
           ln2_g, ln2_b):
    bsz, seq, d = x.shape
    depth = w_in.shape[0]
    assert depth == 1
    alpha = (2.0 * depth) ** 0.25
    n = bsz * seq
    topk = min(IDX_TOPK_MAX, seq // 4)
    r2 = lambda v: v.reshape(1, -1)

    x2 = x.reshape(n, d)
    pos2 = positions.reshape(n, 1)
    w1, wq, wqr, wk, wvt, one, invf, sgn, cb, sb = _layout_weights(w_in[0], w_uq[0], w_uk[0], w_uv[0])

    blk = 256
    bt = _bias_tiles(rel_bias, blk)
    qm, km, vmt, qb, kk, vvt, qi, ki, wt = _in_proj(
        x2, pos2, r2(ln0_g), r2(ln0_b), w1, r2(q_norm_g[0]), wq, wqr, r2(kv_norm_g[0]), wk, wvt, one, invf, sgn,
        cb, sb, tm=512, blk=blk)
    b3 = lambda a: a.reshape(bsz, seq, a.shape[1])
    c4 = lambda a: a.reshape(bsz, seq // blk, a.shape[1], blk)
    o_a = _mla_attn(b3(qm), b3(km), c4(vmt), tq=blk, ck=blk)
    o_b = _dsa_attn(b3(qb), b3(qi), wt, b3(kk), c4(vvt), b3(ki), bt, tq=blk, topk=topk)

    wr = jnp.concatenate([w_grp[0], w_rt[0], jnp.zeros((d, LANES - N_GROUPS - N_EXPERTS), F32)], axis=1)
    br = jnp.concatenate([b_grp[0], b_rt[0], jnp.zeros((LANES - N_GROUPS - N_EXPERTS,), F32)])
    hx, gsel = _post_attn(
        x2, o_a.reshape(n, -1), o_b.reshape(n, -1), r2(ln0_g), r2(ln0_b), w_gate[0].astype(BF16), r2(b_gate[0]),
        w_up_a[0].astype(BF16), w_up_b[0].astype(BF16), w_o[0].astype(BF16), r2(ln1_g[0]), r2(ln1_b[0]),
        wr.astype(BF16), r2(br), alpha, tm=512)
    tm_moe = 512
    tile_group, n_valid, src = _group_tiles(gsel[:, 0], tm_moe)
    out = _moe(hx, tile_group, n_valid, src, w_exp_gate[0].astype(BF16), w_exp_up[0].astype(BF16),
               w_exp_down[0].astype(BF16), r2(ln2_g[0]), r2(ln2_b[0]), alpha, tm_moe, n)
    return out.reshape(bsz, seq, d)
```

```python
import functools
import math

import numpy as np
import jax
import jax.numpy as jnp
from jax import lax
from jax.experimental import pallas as pl
from jax.experimental.pallas import tpu as pltpu

F32 = jnp.float32
BF16 = jnp.bfloat16
I32 = jnp.int32
I16 = jnp.int16

CHUNK = 64
MLA_HEADS = 8
MLA_Q_RANK = 256
MLA_KV_RANK = 128
MLA_NOPE = 64
MLA_ROPE = 32
MLA_V = 64
ROPE_THETA = 10000.0
DSA_HEADS = 8
DSA_HEAD_DIM = 64
IDX_HEADS = 8
IDX_DIM = 32
IDX_TOPK_MAX = 256
REL_BUCKETS = 32
REL_MAX_DIST = 128
N_GROUPS = 4
EXPERTS_PER_GROUP = 8
N_EXPERTS = N_GROUPS * EXPERTS_PER_GROUP
EXPERT_HIDDEN = 256
LN_EPS = 1e-5
RMS_EPS = 1e-6
NEG = -1e30

LANES = 128
_SUBLANES = 8
_VROWS = 80
VMEM_LIMIT = 48 * 1024 * 1024

_KEY_NEG = int(np.array(NEG, np.float32).view(np.int32)) ^ 0x7FFFFFFF
_I16_MIN = -(2 ** 15)
_LOG2E = math.log2(math.e)
_CHUNK_SHIFT = CHUNK.bit_length() - 1
assert 1 << _CHUNK_SHIFT == CHUNK


def _ln(x, g, b):
    mu = jnp.mean(x, axis=-1, keepdims=True)
    xc = x - mu
    var = jnp.mean(xc * xc, axis=-1, keepdims=True)
    return xc * lax.rsqrt(var + LN_EPS) * g + b


def _rms(x, g):
    return x * lax.rsqrt(jnp.mean(x * x, axis=-1, keepdims=True) + RMS_EPS) * g


def _dot(a, b):
    return jnp.dot(a, b, preferred_element_type=F32)


def _dot_nt(a, b):
    return lax.dot_general(a, b, (((1,), (1,)), ((), ())), preferred_element_type=F32)


def _bias_kernel(rb_ref, o_ref):
    nb = REL_BUCKETS // 2
    max_exact = nb // 2
    blk = o_ref.shape[2]
    s = lax.broadcasted_iota(I32, (blk, blk), 0)
    t = lax.broadcasted_iota(I32, (blk, blk), 1)
    for which, off in ((0, -blk), (1, 0)):
        rel = s - t + off
        ret = jnp.where(rel > 0, nb, 0)
        n = jnp.abs(rel)
        large = max_exact + (jnp.log(jnp.maximum(n, 1).astype(F32) / max_exact)
                             / math.log(REL_MAX_DIST / max_exact) * (nb - max_exact)).astype(I32)
        large = jnp.minimum(large, nb - 1)
        bucket = ret + jnp.where(n < max_exact, n, large)
        for h in range(DSA_HEADS):
            acc = jnp.zeros((blk, blk), F32)
            for b in range(REL_BUCKETS):
                acc = jnp.where(bucket == b, rb_ref[b, h], acc)
            o_ref[which, h] = (acc - rb_ref[nb - 1, h]) * _LOG2E


def _bias_tiles(rel_bias, blk):
    assert blk >= REL_MAX_DIST
    return pl.pallas_call(
        _bias_kernel,
        out_shape=jax.ShapeDtypeStruct((2, DSA_HEADS, blk, blk), F32),
        in_specs=[pl.BlockSpec(memory_space=pltpu.SMEM)],
        out_specs=pl.BlockSpec(memory_space=pltpu.VMEM),
        name="bias_tiles",
    )(rel_bias)


_C_Q, _C_KV, _C_RA, _C_RB, _C_QB, _C_KK, _C_VV, _C_QI, _C_KI, _C_END = (
    0, 256, 384, 512, 640, 1152, 1280, 1408, 1664, 1792)


def _value_rows(vt, dv):
    t = vt.shape[1]
    return jnp.concatenate([vt, jnp.ones((1, t), F32), jnp.zeros((_VROWS - dv - 1, t), F32)], axis=0)


def _in_proj_kernel(x_ref, pos_ref, g0_ref, b0_ref, w1_ref, qg_ref, wq_ref, wqr_ref, kg_ref, wk_ref,
                    wvt_ref, one_ref, invf_ref, sgn_ref, cb_ref, sb_ref,
                    qm_ref, km_ref, vmt_ref, qb_ref, kk_ref, vvt_ref, qi_ref, ki_ref, wt_ref, *, blk):
    tm = x_ref.shape[0]
    xn = _ln(x_ref[...], g0_ref[...], b0_ref[...])
    proj = _dot(xn.astype(BF16), w1_ref[...])
    qb_ref[...] = (proj[:, _C_QB:_C_KK] * _LOG2E).astype(BF16)
    kk_ref[...] = proj[:, _C_KK:_C_VV].astype(BF16)
    qi_ref[...] = proj[:, _C_QI:_C_KI].astype(BF16)
    ki_ref[...] = proj[:, _C_KI:_C_END].astype(BF16)
    ga = proj[:, _C_RA:_C_RB]
    gb = proj[:, _C_RB:_C_QB]
    wt_ref[...] = ga.T[0:IDX_HEADS, :] * (IDX_HEADS ** -0.5 * IDX_DIM ** -0.5)
    vvt = _value_rows(proj[:, _C_VV:_C_QI].T[0:DSA_HEAD_DIM, :], DSA_HEAD_DIM).astype(BF16)
    for c in range(tm // blk):
        vvt_ref[c] = vvt[:, c * blk:(c + 1) * blk]

    groups = tm // CHUNK
    a = (pos_ref[0:1, :] >> _CHUNK_SHIFT) + lax.broadcasted_iota(I32, (groups, 1), 0)
    ang = (a * CHUNK).astype(F32) * invf_ref[...]
    ca, sa = jnp.cos(ang), jnp.sin(ang)
    cb, sb = cb_ref[...], sb_ref[...]
    cosv = jnp.concatenate([ca[g:g + 1] * cb - sa[g:g + 1] * sb for g in range(groups)], axis=0)
    sinv = jnp.concatenate([sa[g:g + 1] * cb + ca[g:g + 1] * sb for g in range(groups)], axis=0) * sgn_ref[...]
    lane = lax.broadcasted_iota(I32, cosv.shape, 1)
    rope_lane = (lane >= MLA_NOPE) & (lane < MLA_NOPE + MLA_ROPE)
    kr = jnp.where(rope_lane, ga * cosv + gb * sinv, 0.0)

    scale = (MLA_NOPE + MLA_ROPE) ** -0.5 * _LOG2E
    cos8 = jnp.concatenate([cosv * scale] * MLA_HEADS, axis=1)
    sin8 = jnp.concatenate([sinv * scale] * MLA_HEADS, axis=1)
    cqn = _rms(proj[:, _C_Q:_C_KV], qg_ref[...]).astype(BF16)
    q = _dot(cqn, wq_ref[...]) * cos8 + _dot(cqn, wqr_ref[...]) * sin8
    qm_ref[...] = q.astype(BF16)

    ckn = _rms(proj[:, _C_KV:_C_RA], kg_ref[...]).astype(BF16)
    k = _dot(ckn, wk_ref[...]) + jnp.concatenate([kr] * MLA_HEADS, axis=1)
    km_ref[...] = k.astype(BF16)
    vmt = (_dot_nt(wvt_ref[...], ckn) + one_ref[...]).astype(BF16)
    for c in range(tm // blk):
        vmt_ref[c] = vmt[:, c * blk:(c + 1) * blk]


def _in_proj(x2, pos2, g0, b0, w1, qg, wq, wqr, kg, wk, wvt, one, invf, sgn, cb, sb, tm, blk):
    n, d = x2.shape
    assert tm % blk == 0 and blk % CHUNK == 0
    hm = MLA_HEADS * LANES
    row = lambda w: pl.BlockSpec((tm, w), lambda i: (i, 0))
    full = lambda a: pl.BlockSpec(a.shape, lambda i: (0,) * a.ndim)
    slab = lambda r: pl.BlockSpec((tm // blk, r, blk), lambda i: (i, 0, 0))
    tok = lambda w, dt: (jax.ShapeDtypeStruct((n, w), dt), row(w))
    slb = lambda r: (jax.ShapeDtypeStruct((n // blk, r, blk), BF16), slab(r))
    outs = [tok(hm, BF16), tok(hm, BF16), slb(MLA_HEADS * _VROWS), tok(DSA_HEADS * DSA_HEAD_DIM, BF16),
            tok(LANES, BF16), slb(_VROWS), tok(IDX_HEADS * IDX_DIM, BF16), tok(LANES, BF16),
            (jax.ShapeDtypeStruct((IDX_HEADS, n), F32), pl.BlockSpec((IDX_HEADS, tm), lambda i: (0, i)))]
    consts = (g0, b0, w1, qg, wq, wqr, kg, wk, wvt, one, invf, sgn, cb, sb)
    return pl.pallas_call(
        functools.partial(_in_proj_kernel, blk=blk),
        grid=(n // tm,),
        in_specs=[row(d), row(1)] + [full(a) for a in consts],
        out_specs=[s for _, s in outs],
        out_shape=[s for s, _ in outs],
        compiler_params=pltpu.CompilerParams(dimension_semantics=("parallel",),
                                             vmem_limit_bytes=VMEM_LIMIT),
        name="in_proj",
    )(x2, pos2, *consts)


def _admissible(ck, tq, k0, q0):
    s = k0 + lax.broadcasted_iota(I32, (ck, tq), 0)
    t = q0 + lax.broadcasted_iota(I32, (ck, tq), 1)
    return (s >> _CHUNK_SHIFT) <= (t >> _CHUNK_SHIFT)


def _softmax_step(s, vt, m_ref, acc_ref, h):
    sb = s.astype(BF16)
    m_prev = m_ref[h]
    m_new = jnp.maximum(m_prev, jnp.max(sb, axis=0, keepdims=True).astype(F32))
    alpha = jnp.exp2(m_prev - m_new)
    p = jnp.exp2(sb - m_new.astype(BF16))
    acc_ref[h] = alpha * acc_ref[h] + _dot(vt, p)
    m_ref[h] = m_new


def _scores_first(s_ref, nh, produce, consume):
    for h in range(nh):
        s_ref[h] = produce(h)
    for h in range(nh):
        consume(h, s_ref[h])


def _finish(o_ref, acc_ref, nh, dv):
    ot = jnp.concatenate([acc_ref[h, :dv, :] / acc_ref[h, dv:dv + 1, :] for h in range(nh)], axis=0)
    o_ref[...] = ot.T.astype(BF16)


def _mla_kernel(q_ref, k_ref, vt_ref, o_ref, s_ref, m_ref, acc_ref, *, tq, ck):
    j = pl.program_id(1)
    q0 = j * tq
    m_ref[...] = jnp.full(m_ref.shape, NEG, F32)
    acc_ref[...] = jnp.zeros(acc_ref.shape, F32)
    nfull = (q0 + CHUNK) // ck

    def chunk(c, masked):
        k0 = pl.multiple_of(c * ck, ck)
        adm = _admissible(ck, tq, k0, q0) if masked else None

        def score(h):
            return _dot_nt(k_ref[pl.ds(k0, ck), h * LANES:(h + 1) * LANES], q_ref[:, h * LANES:(h + 1) * LANES])

        def use(h, s):
            if masked:
                s = jnp.where(adm, s, NEG)
            _softmax_step(s, vt_ref[c, h * _VROWS:(h + 1) * _VROWS, :], m_ref, acc_ref, h)

        _scores_first(s_ref, MLA_HEADS, score, use)

    def full_body(c, carry):
        chunk(c, False)
        return carry

    lax.fori_loop(0, nfull, full_body, 0)
    chunk(nfull, True)
    _finish(o_ref, acc_ref, MLA_HEADS, MLA_V)


def _mla_attn(qm, km, vmt, tq, ck):
    b, t, hm = qm.shape
    dv = MLA_HEADS * MLA_V
    assert ck % tq == 0 and tq > CHUNK and tq % CHUNK == 0 and t % ck == 0
    return pl.pallas_call(
        functools.partial(_mla_kernel, tq=tq, ck=ck),
        grid=(b, t // tq),
        in_specs=[pl.BlockSpec((None, tq, hm), lambda i, j: (i, j, 0)),
                  pl.BlockSpec((None, t, hm), lambda i, j: (i, 0, 0)),
                  pl.BlockSpec((None, t // ck, MLA_HEADS * _VROWS, ck), lambda i, j: (i, 0, 0, 0))],
        out_specs=pl.BlockSpec((None, tq, dv), lambda i, j: (i, j, 0)),
        out_shape=jax.ShapeDtypeStruct((b, t, dv), BF16),
        scratch_shapes=[pltpu.VMEM((MLA_HEADS, ck, tq), F32), pltpu.VMEM((MLA_HEADS, 1, tq), F32),
                        pltpu.VMEM((MLA_HEADS, _VROWS, tq), F32)],
        compiler_params=pltpu.CompilerParams(dimension_semantics=("parallel", "arbitrary"),
                                             vmem_limit_bytes=VMEM_LIMIT),
        name="mla_attn",
    )(qm, km, vmt)


def _dsa_kernel(qb_ref, qi_ref, wt_ref, kk_ref, vt_ref, ki_ref, bt_ref, o_ref,
                qs_ref, qim_ref, keys_ref, khi_ref, klo_ref, kls_ref, s_ref, thr_ref, cut_ref, m_ref, acc_ref,
                *, tq, topk):
    nh = DSA_HEADS
    ck = tq
    seq = kk_ref.shape[0]
    j = pl.program_id(1)
    q0 = j * tq
    nck = j + 1
    lane = lax.broadcasted_iota(I32, (tq, LANES), 1)

    for h in range(nh):
        pair = qb_ref[:, (h // 2) * LANES:(h // 2 + 1) * LANES].astype(F32)
        in_head = (lane >= (h % 2) * DSA_HEAD_DIM) & (lane < (h % 2 + 1) * DSA_HEAD_DIM)
        qs_ref[h * tq:(h + 1) * tq, :] = jnp.where(in_head, pair, 0.0).astype(BF16)
        grp = qi_ref[:, (h // 4) * LANES:(h // 4 + 1) * LANES].astype(F32)
        in_head = (lane >= (h % 4) * IDX_DIM) & (lane < (h % 4 + 1) * IDX_DIM)
        qim_ref[h * tq:(h + 1) * tq, :] = jnp.where(in_head, grp, 0.0).astype(BF16)

    w = wt_ref[...]

    def idx_chunk(c, masked):
        k0 = pl.multiple_of(c * ck, ck)
        kc = ki_ref[pl.ds(k0, ck), :]
        total = [jnp.zeros((ck, tq), F32)]

        def add(h, r):
            total[0] = total[0] + w[h:h + 1, :] * jnp.maximum(r, 0.0)

        _scores_first(s_ref, nh, lambda h: _dot_nt(kc, qim_ref[h * tq:(h + 1) * tq, :]), add)
        score = total[0]
        if masked:
            score = jnp.where(_admissible(ck, tq, k0, q0), score, NEG)
        score = jnp.where(score == 0.0, 0.0, score)
        bits = lax.bitcast_convert_type(score, I32)
        key = bits ^ ((bits >> 31) & 0x7FFFFFFF)
        keys_ref[c] = key
        khi_ref[c] = (key >> 16).astype(I16)
        klo_ref[c] = ((key & 0xFFFF) - 0x8000).astype(I16)

    def idx_body(c, carry):
        idx_chunk(c, False)
        return carry

    lax.fori_loop(0, j, idx_body, 0)
    idx_chunk(j, True)

    thr_ref[...] = jnp.full(thr_ref.shape, _KEY_NEG + 1, I32)
    cut_ref[...] = jnp.full(cut_ref.shape, seq, I32)

    def count16(ref, cand):
        rows = 2 * _SUBLANES

        def hits(c):
            part = jnp.where(ref[c] >= cand, jnp.int16(1), jnp.int16(0)).reshape(ck // rows, rows, tq)
            while part.shape[0] > 1:
                half = part.shape[0] // 2
                part = part[:half] + part[half:]
            return part[0]

        acc = lax.fori_loop(0, nck // 2, lambda i, a: a + hits(2 * i) + hits(2 * i + 1),
                            jnp.zeros((rows, tq), I16))
        acc = lax.fori_loop(nck // 2 * 2, nck, lambda c, a: a + hits(c), acc)
        return jnp.sum(acc.astype(F32), axis=0, keepdims=True)

    def greedy16(ref, need):
        def step(i, carry):
            v, cge, cgt = carry
            cand = v + lax.shift_left(jnp.int32(1), 15 - i)
            cnt = count16(ref, cand.astype(I16))
            ok = cnt >= need
            return jnp.where(ok, cand, v), jnp.where(ok, cnt, cge), jnp.where(ok, cgt, cnt)

        v0 = jnp.full((1, tq), _I16_MIN, I32)
        cge0 = jnp.full((1, tq), 1.0, F32) * (nck * ck).astype(F32)
        return lax.fori_loop(0, 16, step, (v0, cge0, jnp.zeros((1, tq), F32)))

    @pl.when(q0 + tq > topk)
    def _search():
        hi, cge_hi, cgt_hi = greedy16(khi_ref, float(topk))
        hi16 = hi.astype(I16)

        def mask_body(c, carry):
            kls_ref[c] = jnp.where(khi_ref[c] == hi16, klo_ref[c], jnp.int16(_I16_MIN))
            return carry

        lax.fori_loop(0, nck, mask_body, 0)
        lo, cge_lo, _ = greedy16(kls_ref, float(topk) - cgt_hi)
        thr = hi * 65536 + (lo + 0x8000)
        thr_ref[...] = thr
        cge = cgt_hi + jnp.where(lo == _I16_MIN, cge_hi - cgt_hi, cge_lo)
        excess = cge - float(topk)

        @pl.when(jnp.max(excess) > 0.0)
        def _ties():
            nbits = max(1, int(seq - 1).bit_length())
            lo16 = lo.astype(I16)

            def tie_body(c, carry):
                idx = (c * ck + lax.broadcasted_iota(I32, (ck, tq), 0)).astype(I16)
                tied = (khi_ref[c] == hi16) & (klo_ref[c] == lo16)
                kls_ref[c] = jnp.where(tied, idx, jnp.int16(-1))
                return carry

            lax.fori_loop(0, nck, tie_body, 0)

            def tstep(i, cut):
                cand = cut + lax.shift_left(jnp.int32(1), nbits - 1 - i)
                return jnp.where(count16(kls_ref, cand.astype(I16)) >= excess, cand, cut)

            cut = lax.fori_loop(0, nbits, tstep, jnp.zeros((1, tq), I32))
            cut_ref[...] = jnp.where(excess > 0.0, cut, seq)

    m_ref[...] = jnp.full(m_ref.shape, NEG, F32)
    acc_ref[...] = jnp.zeros(acc_ref.shape, F32)
    thr = thr_ref[...]
    cut = cut_ref[...]

    def att_chunk(c, bias_slab, own):
        k0 = pl.multiple_of(c * ck, ck)
        kc = kk_ref[pl.ds(k0, ck), :]
        key = keys_ref[c]
        idx = k0 + lax.broadcasted_iota(I32, (ck, tq), 0)
        sel = (key > thr) | ((key == thr) & (idx < cut))
        if own:
            sel = sel & _admissible(ck, tq, k0, q0)
        vt = vt_ref[c]

        def use(h, s):
            if bias_slab is not None:
                s = s + bt_ref[bias_slab, h]
            _softmax_step(jnp.where(sel, s, NEG), vt, m_ref, acc_ref, h)

        _scores_first(s_ref, nh, lambda h: _dot_nt(kc, qs_ref[h * tq:(h + 1) * tq, :]), use)

    def far_body(c, carry):
        att_chunk(c, None, False)
        return carry

    lax.fori_loop(0, j - 1, far_body, 0)

    @pl.when(j > 0)
    def _prev():
        att_chunk(j - 1, 0, False)

    att_chunk(j, 1, True)
    _finish(o_ref, acc_ref, nh, DSA_HEAD_DIM)


def _dsa_attn(qb, qi, wt, kk, vt, ki, bt, tq, topk):
    b, t, _ = qb.shape
    ck = tq
    assert tq == bt.shape[2] and tq % LANES == 0 and t % tq == 0 and t < -_I16_MIN
    nh = DSA_HEADS
    blk = lambda a: pl.BlockSpec((None, tq, a.shape[2]), lambda i, j: (i, j, 0))
    seqb = lambda a: pl.BlockSpec((None, t, a.shape[2]), lambda i, j: (i, 0, 0))
    return pl.pallas_call(
        functools.partial(_dsa_kernel, tq=tq, topk=topk),
        grid=(b, t // tq),
        in_specs=[blk(qb), blk(qi), pl.BlockSpec((nh, tq), lambda i, j: (0, i * (t // tq) + j)), seqb(kk),
                  pl.BlockSpec((None, t // ck, _VROWS, ck), lambda i, j: (i, 0, 0, 0)), seqb(ki),
                  pl.BlockSpec(bt.shape, lambda i, j: (0, 0, 0, 0))],
        out_specs=blk(qb),
        out_shape=jax.ShapeDtypeStruct(qb.shape, BF16),
        scratch_shapes=[pltpu.VMEM((nh * tq, LANES), BF16), pltpu.VMEM((nh * tq, LANES), BF16),
                        pltpu.VMEM((t // ck, ck, tq), I32), pltpu.VMEM((t // ck, ck, tq), I16),
                        pltpu.VMEM((t // ck, ck, tq), I16), pltpu.VMEM((t // ck, ck, tq), I16),
                        pltpu.VMEM((nh, ck, tq), F32),
                        pltpu.VMEM((1, tq), I32),
                        pltpu.VMEM((1, tq), I32), pltpu.VMEM((nh, 1, tq), F32),
                        pltpu.VMEM((nh, _VROWS, tq), F32)],
        compiler_params=pltpu.CompilerParams(dimension_semantics=("parallel", "arbitrary"),
                                             vmem_limit_bytes=VMEM_LIMIT),
        name="dsa_attn",
    )(qb, qi, wt, kk, vt, ki, bt)


_R_GRP = 0
_R_EXP = N_GROUPS


def _post_kernel(x_ref, oa_ref, ob_ref, g0_ref, b0_ref, wg_ref, bg_ref, wua_ref, wub_ref, wo_ref,
                 g1_ref, b1_ref, wr_ref, br_ref, hx_ref, gsel_ref, *, alpha):
    d = x_ref.shape[1]
    xn = _ln(x_ref[...], g0_ref[...], b0_ref[...])
    z = _dot(xn.astype(BF16), wg_ref[...]) + bg_ref[...]
    gates = 1.0 / (1.0 + jnp.exp(-z))
    mix = gates[:, :d] * _dot(oa_ref[...], wua_ref[...]) + gates[:, d:] * _dot(ob_ref[...], wub_ref[...])
    mixed = _dot(mix.astype(BF16), wo_ref[...])
    h = _ln(alpha * xn + mixed, g1_ref[...], b1_ref[...])
    hx_ref[:, :d] = h

    logits = _dot(h.astype(BF16), wr_ref[...]) + br_ref[...]
    lane = lax.broadcasted_iota(I32, logits.shape, 1).astype(F32)
    gmask = lane < N_GROUPS
    gl = jnp.where(gmask, logits, -jnp.inf)
    gmax = jnp.max(gl, axis=1, keepdims=True)
    g_sel = jnp.min(jnp.where(gl == gmax, lane, float(LANES)), axis=1, keepdims=True)
    g_w = 1.0 / jnp.sum(jnp.where(gmask, jnp.exp(gl - gmax), 0.0), axis=1, keepdims=True)
    e_lo = _R_EXP + g_sel * EXPERTS_PER_GROUP
    el = jnp.where((lane >= e_lo) & (lane < e_lo + EXPERTS_PER_GROUP), logits, -jnp.inf)
    m1 = jnp.max(el, axis=1, keepdims=True)
    i1 = jnp.min(jnp.where(el == m1, lane, float(LANES)), axis=1, keepdims=True)
    el2 = jnp.where(lane == i1, -jnp.inf, el)
    m2 = jnp.max(el2, axis=1, keepdims=True)
    i2 = jnp.min(jnp.where(el2 == m2, lane, float(LANES)), axis=1, keepdims=True)
    e21 = jnp.exp(m2 - m1)
    w1 = g_w / (1.0 + e21)
    w2 = g_w * e21 / (1.0 + e21)
    hx_ref[:, d:] = jnp.where(lane == i1, w1, 0.0) + jnp.where(lane == i2, w2, 0.0)
    gsel_ref[...] = g_sel.astype(I32)


def _post_attn(x2, oa, ob, g0, b0, wg, bg, wua, wub, wo, g1, b1, wr, br, alpha, tm):
    n, d = x2.shape
    row = lambda w: pl.BlockSpec((tm, w), lambda i: (i, 0))
    full = lambda a: pl.BlockSpec(a.shape, lambda i: (0,) * a.ndim)
    return pl.pallas_call(
        functools.partial(_post_kernel, alpha=alpha),
        grid=(n // tm,),
        in_specs=[row(d), row(oa.shape[1]), row(ob.shape[1])]
                 + [full(a) for a in (g0, b0, wg, bg, wua, wub, wo, g1, b1, wr, br)],
        out_specs=[row(d + LANES), row(1)],
        out_shape=[jax.ShapeDtypeStruct((n, d + LANES), F32), jax.ShapeDtypeStruct((n, 1), I32)],
        compiler_params=pltpu.CompilerParams(dimension_semantics=("parallel",),
                                             vmem_limit_bytes=VMEM_LIMIT),
        name="post_attn",
    )(x2, oa, ob, g0, b0, wg, bg, wua, wub, wo, g1, b1, wr, br)


def _moe_kernel(tg_ref, nv_ref, src_ref, srcn_ref, hx_hbm, weg_ref, weu_ref, wed_ref, g2_ref, b2_ref,
                out_hbm, gbuf, obuf, gsem, ssem, *, alpha, tm, d):
    k = pl.program_id(0)
    nk = pl.num_programs(0)
    slot = k % 2
    nrun = tm // LANES

    def gather_copy(tok, r, s):
        return pltpu.make_async_copy(hx_hbm.at[pl.ds(tok, 1)], gbuf.at[s, pl.ds(r, 1)], gsem.at[s])

    def scatter_copy(tok, r, s):
        return pltpu.make_async_copy(obuf.at[s, pl.ds(r, 1)], out_hbm.at[pl.ds(tok, 1)], ssem.at[s])

    def start_gather(tile_src_ref, s):
        for g in range(nrun):
            for c in range(LANES):
                gather_copy(tile_src_ref[g, c], g * LANES + c, s).start()

    def start_scatter(n_rows, s):
        for g in range(nrun):
            @pl.when(n_rows >= (g + 1) * LANES)
            def _run():
                for c in range(LANES):
                    scatter_copy(src_ref[g, c], g * LANES + c, s).start()

        def row(r, carry):
            scatter_copy(src_ref[r // LANES, r % LANES], r, s).start()
            return carry

        lax.fori_loop(n_rows // LANES * LANES, n_rows, row, 0)

    def by_slot(fn, s_of_slot):
        for par in (0, 1):
            pl.when(slot == par)(functools.partial(fn, s_of_slot(par)))

    def wait_scatter(n_rows, s):
        def waiter(rows):
            def body(i, carry):
                pltpu.make_async_copy(obuf.at[s, pl.ds(0, rows)], out_hbm.at[pl.ds(0, rows)], ssem.at[s]).wait()
                return carry
            return body

        lax.fori_loop(0, n_rows // LANES, waiter(LANES), 0)
        lax.fori_loop(0, n_rows % LANES // _SUBLANES, waiter(_SUBLANES), 0)
        lax.fori_loop(0, n_rows % _SUBLANES, waiter(1), 0)

    @pl.when(k == 0)
    def _first():
        start_gather(src_ref, 0)

    pltpu.make_async_copy(hx_hbm.at[pl.ds(0, tm)], gbuf.at[slot], gsem.at[slot]).wait()

    @pl.when(k + 1 < nk)
    def _prefetch():
        by_slot(lambda s: start_gather(srcn_ref, s), lambda par: 1 - par)

    @pl.when(k >= 2)
    def _free_obuf():
        wait_scatter(nv_ref[k - 2], slot)

    x = gbuf[slot]
    h = x[:, :d]
    comb = x[:, d:]
    hb = h.astype(BF16)
    lane = lax.broadcasted_iota(I32, comb.shape, 1)
    e0 = _R_EXP + tg_ref[k] * EXPERTS_PER_GROUP
    acc = jnp.zeros((tm, d), F32)
    for e in range(EXPERTS_PER_GROUP):
        a = _dot(hb, weg_ref[e])
        u = _dot(hb, weu_ref[e])
        cw = jnp.sum(jnp.where(lane == e0 + e, comb, 0.0), axis=1, keepdims=True)
        hid = (a / (1.0 + jnp.exp(-a))) * u * cw
        acc = acc + _dot(hid.astype(BF16), wed_ref[e])
    obuf[slot] = _ln(alpha * h + acc, g2_ref[...], b2_ref[...])

    nv = nv_ref[k]
    by_slot(lambda s: start_scatter(nv, s), lambda par: par)

    @pl.when(k == nk - 1)
    def _drain():
        wait_scatter(nv, slot)

        @pl.when(k >= 1)
        def _():
            wait_scatter(nv_ref[k - 1], 1 - slot)


def _moe(hx, tile_group, n_valid, src, weg, weu, wed, g2, b2, alpha, tm, n):
    d = hx.shape[1] - LANES
    nk = src.shape[0]
    ne, _, f = weg.shape
    gsz = EXPERTS_PER_GROUP
    grid_spec = pltpu.PrefetchScalarGridSpec(
        num_scalar_prefetch=2,
        grid=(nk,),
        in_specs=[pl.BlockSpec((None, tm // LANES, LANES), lambda k, tg, nv: (k, 0, 0),
                               memory_space=pltpu.SMEM),
                  pl.BlockSpec((None, tm // LANES, LANES),
                               lambda k, tg, nv: (jnp.minimum(k + 1, nk - 1), 0, 0), memory_space=pltpu.SMEM),
                  pl.BlockSpec(memory_space=pl.ANY),
                  pl.BlockSpec((gsz, d, f), lambda k, tg, nv: (tg[k], 0, 0)),
                  pl.BlockSpec((gsz, d, f), lambda k, tg, nv: (tg[k], 0, 0)),
                  pl.BlockSpec((gsz, f, d), lambda k, tg, nv: (tg[k], 0, 0)),
                  pl.BlockSpec(g2.shape, lambda k, tg, nv: (0, 0)),
                  pl.BlockSpec(b2.shape, lambda k, tg, nv: (0, 0))],
        out_specs=pl.BlockSpec(memory_space=pl.ANY),
        scratch_shapes=[pltpu.VMEM((2, tm, d + LANES), F32), pltpu.VMEM((2, tm, d), F32),
                        pltpu.SemaphoreType.DMA((2,)), pltpu.SemaphoreType.DMA((2,))])
    return pl.pallas_call(
        functools.partial(_moe_kernel, alpha=alpha, tm=tm, d=d),
        grid_spec=grid_spec,
        out_shape=jax.ShapeDtypeStruct((n, d), F32),
        compiler_params=pltpu.CompilerParams(dimension_semantics=("arbitrary",),
                                             vmem_limit_bytes=VMEM_LIMIT),
        name="moe",
    )(tile_group, n_valid, src, src, hx, weg, weu, wed, g2, b2)


def _group_tiles(gsel, tm):
    n = gsel.shape[0]
    n_tiles = n // tm + N_GROUPS
    onehot = (gsel[:, None] == jnp.arange(N_GROUPS, dtype=I32)[None, :]).astype(I32)
    csum = jnp.cumsum(onehot, axis=0)
    rank = jnp.sum(csum * onehot, axis=1) - 1
    counts = csum[-1]
    padded = (counts + tm - 1) // tm * tm
    ends = jnp.cumsum(padded)
    offs = ends - padded
    dest = offs[gsel] + rank
    src = jnp.zeros((n_tiles * tm,), I32).at[dest].set(jnp.arange(n, dtype=I32))
    tile_start = jnp.arange(n_tiles, dtype=I32) * tm
    tile_group = jnp.minimum(jnp.sum((tile_start[:, None] >= ends[None, :]).astype(I32), axis=1), N_GROUPS - 1)
    n_valid = jnp.clip(offs[tile_group] + counts[tile_group] - tile_start, 0, tm)
    return tile_group.astype(I32), n_valid.astype(I32), src.reshape(n_tiles, tm // LANES, LANES)


def _layout_weights(w_in, w_uq, w_uk, w_uv):
    d = w_in.shape[0]
    sizes = (MLA_Q_RANK, MLA_KV_RANK, MLA_ROPE, DSA_HEADS * DSA_HEAD_DIM, DSA_HEAD_DIM, DSA_HEAD_DIM,
             IDX_HEADS * IDX_DIM, IDX_DIM, IDX_HEADS)
    offs = np.cumsum((0,) + sizes)
    c_q, c_kv, k_rope, q_b, k_b, v_b, q_idx, k_idx, w_idx = (
        w_in[:, int(offs[i]):int(offs[i + 1])] for i in range(len(sizes)))
    half = MLA_ROPE // 2
    z = lambda w: jnp.zeros((d, w), w_in.dtype)
    pad = LANES - MLA_NOPE - MLA_ROPE
    grp_a = jnp.concatenate([w_idx, z(MLA_NOPE - IDX_HEADS), k_rope, z(pad)], axis=1)
    grp_b = jnp.concatenate([z(MLA_NOPE), k_rope[:, half:], k_rope[:, :half], z(pad)], axis=1)
    w1 = jnp.concatenate([c_q, c_kv, grp_a, grp_b, q_b * DSA_HEAD_DIM ** -0.5, k_b, k_b, v_b, v_b, q_idx,
                          k_idx, k_idx, k_idx, k_idx], axis=1)
    assert w1.shape[1] == _C_END

    dk = MLA_NOPE + MLA_ROPE
    rq = w_uq.shape[0]
    zq = lambda w: jnp.zeros((rq, w), w_uq.dtype)
    wq_parts, wqr_parts, wk_parts = [], [], []
    for h in range(MLA_HEADS):
        nope = w_uq[:, h * dk:h * dk + MLA_NOPE]
        rope = w_uq[:, h * dk + MLA_NOPE:(h + 1) * dk]
        wq_parts += [nope, rope, zq(pad)]
        wqr_parts += [zq(MLA_NOPE), rope[:, half:], rope[:, :half], zq(pad)]
        wk_parts += [w_uk[:, h * MLA_NOPE:(h + 1) * MLA_NOPE],
                     jnp.zeros((w_uk.shape[0], LANES - MLA_NOPE), w_uk.dtype)]
    wq = jnp.concatenate(wq_parts, axis=1)
    wqr = jnp.concatenate(wqr_parts, axis=1)
    wk = jnp.concatenate(wk_parts, axis=1)

    inv = ROPE_THETA ** (-np.arange(half, dtype=np.float32) / half)
    invf = np.zeros((1, LANES), np.float32)
    invf[0, MLA_NOPE:MLA_NOPE + half] = inv
    invf[0, MLA_NOPE + half:MLA_NOPE + MLA_ROPE] = inv
    sgn = np.zeros((1, LANES), np.float32)
    sgn[0, MLA_NOPE:MLA_NOPE + half] = -1.0
    sgn[0, MLA_NOPE + half:MLA_NOPE + MLA_ROPE] = 1.0
    ang_b = np.arange(CHUNK, dtype=np.float64)[:, None] * invf.astype(np.float64)
    cb, sb = np.cos(ang_b).astype(np.float32), np.sin(ang_b).astype(np.float32)

    kvr = w_uv.shape[0]
    wvt = jnp.concatenate(
        [jnp.concatenate([w_uv[:, h * MLA_V:(h + 1) * MLA_V].T, jnp.zeros((_VROWS - MLA_V, kvr), w_uv.dtype)], axis=0)
         for h in range(MLA_HEADS)], axis=0)
    one = np.zeros((MLA_HEADS * _VROWS, 1), np.float32)
    one[MLA_V::_VROWS, 0] = 1.0
    return (w1.astype(BF16), wq.astype(BF16), wqr.astype(BF16), wk.astype(BF16), wvt.astype(BF16),
            jnp.asarray(one), jnp.asarray(invf), jnp.asarray(sgn), jnp.asarray(cb), jnp.asarray(sb))


def kernel(x, positions, ln0_g, ln0_b, w_in, q_norm_g, w_uq, kv_norm_g, w_uk, w_uv, rel_bias, w_up_a, w_up_b,
           w_gate, b_gate, w_o, ln1_g, ln1_b, w_grp, b_grp, w_rt, b_rt, w_exp_gate, w_exp_up, w_exp_down,
           ln2_g, ln2_b):
    bsz, seq, d = x.shape
    depth = w_in.shape[0]
    assert depth == 1
    alpha = (2.0 * depth) ** 0.25
    n = bsz * seq
    topk = min(IDX_TOPK_MAX, seq // 4)
    r2 = lambda v: v.reshape(1, -1)

    x2 = x.reshape(n, d)
    pos2 = positions.reshape(n, 1)
    w1, wq, wqr, wk, wvt, one, invf, sgn, cb, sb = _layout_weights(w_in[0], w_uq[0], w_uk[0], w_uv[0])

    blk = 256
    bt = _bias_tiles(rel_bias, blk)
    qm, km, vmt, qb, kk, vvt, qi, ki, wt = _in_proj(
        x2, pos2, r2(ln0_g), r2(ln0_b), w1, r2(q_norm_g[0]), wq, wqr, r2(kv_norm_g[0]), wk, wvt, one, invf, sgn,
        cb, sb, tm=512, blk=blk)
    b3 = lambda a: a.reshape(bsz, seq, a.shape[1])
    c4 = lambda a: a.reshape(bsz, seq // blk, a.shape[1], blk)
    o_a = _mla_attn(b3(qm), b3(km), c4(vmt), tq=blk, ck=blk)
    o_b = _dsa_attn(b3(qb), b3(qi), wt, b3(kk), c4(vvt), b3(ki), bt, tq=blk, topk=topk)

    wr = jnp.concatenate([w_grp[0], w_rt[0], jnp.zeros((d, LANES - N_GROUPS - N_EXPERTS), F32)], axis=1)
    br = jnp.concatenate([b_grp[0], b_rt[0], jnp.zeros((LANES - N_GROUPS - N_EXPERTS,), F32)])
    hx, gsel = _post_attn(
        x2, o_a.reshape(n, -1), o_b.reshape(n, -1), r2(ln0_g), r2(ln0_b), w_gate[0].astype(BF16), r2(b_gate[0]),
        w_up_a[0].astype(BF16), w_up_b[0].astype(BF16), w_o[0].astype(BF16), r2(ln1_g[0]), r2(ln1_b[0]),
        wr.astype(BF16), r2(br), alpha, tm=512)
    tm_moe = 512
    tile_group, n_valid, src = _group_tiles(gsel[:, 0], tm_moe)
    out = _moe(hx, tile_group, n_valid, src, w_exp_gate[0].astype(BF16), w_exp_up[0].astype(BF16),
               w_exp_down[0].astype(BF16), r2(ln2_g[0]), r2(ln2_b[0]), alpha, tm_moe, n)
    return out.reshape(bsz, seq, d)
```

```python
import functools
import math

import numpy as np
import jax
import jax.numpy as jnp
from jax import lax
from jax.experimental import pallas as pl
from jax.experimental.pallas import tpu as pltpu

F32 = jnp.float32
BF16 = jnp.bfloat16
I32 = jnp.int32
I16 = jnp.int16

CHUNK = 64
MLA_HEADS = 8
MLA_Q_RANK = 256
MLA_KV_RANK = 128
MLA_NOPE = 64
MLA_ROPE = 32
MLA_V = 64
ROPE_THETA = 10000.0
DSA_HEADS = 8
DSA_HEAD_DIM = 64
IDX_HEADS = 8
IDX_DIM = 32
IDX_TOPK_MAX = 256
REL_BUCKETS = 32
REL_MAX_DIST = 128
N_GROUPS = 4
EXPERTS_PER_GROUP = 8
N_EXPERTS = N_GROUPS * EXPERTS_PER_GROUP
EXPERT_HIDDEN = 256
LN_EPS = 1e-5
RMS_EPS = 1e-6
NEG = -1e30

LANES = 128
_SUBLANES = 8
_VROWS = 80
VMEM_LIMIT = 48 * 1024 * 1024

_KEY_NEG = int(np.array(NEG, np.float32).view(np.int32)) ^ 0x7FFFFFFF
_I16_MIN = -(2 ** 15)
_LOG2E = math.log2(math.e)
_CHUNK_SHIFT = CHUNK.bit_length() - 1
assert 1 << _CHUNK_SHIFT == CHUNK


def _ln(x, g, b):
    mu = jnp.mean(x, axis=-1, keepdims=True)
    xc = x - mu
    var = jnp.mean(xc * xc, axis=-1, keepdims=True)
    return xc * lax.rsqrt(var + LN_EPS) * g + b


def _rms(x, g):
    return x * lax.rsqrt(jnp.mean(x * x, axis=-1, keepdims=True) + RMS_EPS) * g


def _dot(a, b):
    return jnp.dot(a, b, preferred_element_type=F32)


def _dot_nt(a, b):
    return lax.dot_general(a, b, (((1,), (1,)), ((), ())), preferred_element_type=F32)


def _bias_kernel(rb_ref, o_ref):
    nb = REL_BUCKETS // 2
    max_exact = nb // 2
    blk = o_ref.shape[2]
    s = lax.broadcasted_iota(I32, (blk, blk), 0)
    t = lax.broadcasted_iota(I32, (blk, blk), 1)
    for which, off in ((0, -blk), (1, 0)):
        rel = s - t + off
        ret = jnp.where(rel > 0, nb, 0)
        n = jnp.abs(rel)
        large = max_exact + (jnp.log(jnp.maximum(n, 1).astype(F32) / max_exact)
                             / math.log(REL_MAX_DIST / max_exact) * (nb - max_exact)).astype(I32)
        large = jnp.minimum(large, nb - 1)
        bucket = ret + jnp.where(n < max_exact, n, large)
        for h in range(DSA_HEADS):
            acc = jnp.zeros((blk, blk), F32)
            for b in range(REL_BUCKETS):
                acc = jnp.where(bucket == b, rb_ref[b, h], acc)
            o_ref[which, h] = (acc - rb_ref[nb - 1, h]) * _LOG2E


def _bias_tiles(rel_bias, blk):
    assert blk >= REL_MAX_DIST
    return pl.pallas_call(
        _bias_kernel,
        out_shape=jax.ShapeDtypeStruct((2, DSA_HEADS, blk, blk), F32),
        in_specs=[pl.BlockSpec(memory_space=pltpu.SMEM)],
        out_specs=pl.BlockSpec(memory_space=pltpu.VMEM),
        name="bias_tiles",
    )(rel_bias)


_C_Q, _C_KV, _C_RA, _C_RB, _C_QB, _C_KK, _C_VV, _C_QI, _C_KI, _C_END = (
    0, 256, 384, 512, 640, 1152, 1280, 1408, 1664, 1792)


def _value_rows(vt, dv):
    t = vt.shape[1]
    return jnp.concatenate([vt, jnp.ones((1, t), F32), jnp.zeros((_VROWS - dv - 1, t), F32)], axis=0)


def _in_proj_kernel(x_ref, pos_ref, g0_ref, b0_ref, w1_ref, qg_ref, wq_ref, wqr_ref, kg_ref, wk_ref,
                    wvt_ref, one_ref, invf_ref, sgn_ref, cb_ref, sb_ref,
                    qm_ref, km_ref, vmt_ref, qb_ref, kk_ref, vvt_ref, qi_ref, ki_ref, wt_ref, *, blk):
    tm = x_ref.shape[0]
    xn = _ln(x_ref[...], g0_ref[...], b0_ref[...])
    proj = _dot(xn.astype(BF16), w1_ref[...])
    qb_ref[...] = (proj[:, _C_QB:_C_KK] * _LOG2E).astype(BF16)
    kk_ref[...] = proj[:, _C_KK:_C_VV].astype(BF16)
    qi_ref[...] = proj[:, _C_QI:_C_KI].astype(BF16)
    ki_ref[...] = proj[:, _C_KI:_C_END].astype(BF16)
    ga = proj[:, _C_RA:_C_RB]
    gb = proj[:, _C_RB:_C_QB]
    wt_ref[...] = ga.T[0:IDX_HEADS, :] * (IDX_HEADS ** -0.5 * IDX_DIM ** -0.5)
    vvt = _value_rows(proj[:, _C_VV:_C_QI].T[0:DSA_HEAD_DIM, :], DSA_HEAD_DIM).astype(BF16)
    for c in range(tm // blk):
        vvt_ref[c] = vvt[:, c * blk:(c + 1) * blk]

    groups = tm // CHUNK
    a = (pos_ref[0:1, :] >> _CHUNK_SHIFT) + lax.broadcasted_iota(I32, (groups, 1), 0)
    ang = (a * CHUNK).astype(F32) * invf_ref[...]
    ca, sa = jnp.cos(ang), jnp.sin(ang)
    cb, sb = cb_ref[...], sb_ref[...]
    cosv = jnp.concatenate([ca[g:g + 1] * cb - sa[g:g + 1] * sb for g in range(groups)], axis=0)
    sinv = jnp.concatenate([sa[g:g + 1] * cb + ca[g:g + 1] * sb for g in range(groups)], axis=0) * sgn_ref[...]
    lane = lax.broadcasted_iota(I32, cosv.shape, 1)
    rope_lane = (lane >= MLA_NOPE) & (lane < MLA_NOPE + MLA_ROPE)
    kr = jnp.where(rope_lane, ga * cosv + gb * sinv, 0.0)

    scale = (MLA_NOPE + MLA_ROPE) ** -0.5 * _LOG2E
    cos8 = jnp.concatenate([cosv * scale] * MLA_HEADS, axis=1)
    sin8 = jnp.concatenate([sinv * scale] * MLA_HEADS, axis=1)
    cqn = _rms(proj[:, _C_Q:_C_KV], qg_ref[...]).astype(BF16)
    q = _dot(cqn, wq_ref[...]) * cos8 + _dot(cqn, wqr_ref[...]) * sin8
    qm_ref[...] = q.astype(BF16)

    ckn = _rms(proj[:, _C_KV:_C_RA], kg_ref[...]).astype(BF16)
    k = _dot(ckn, wk_ref[...]) + jnp.concatenate([kr] * MLA_HEADS, axis=1)
    km_ref[...] = k.astype(BF16)
    vmt = (_dot_nt(wvt_ref[...], ckn) + one_ref[...]).astype(BF16)
    for c in range(tm // blk):
        vmt_ref[c] = vmt[:, c * blk:(c + 1) * blk]


def _in_proj(x2, pos2, g0, b0, w1, qg, wq, wqr, kg, wk, wvt, one, invf, sgn, cb, sb, tm, blk):
    n, d = x2.shape
    assert tm % blk == 0 and blk % CHUNK == 0
    hm = MLA_HEADS * LANES
    row = lambda w: pl.BlockSpec((tm, w), lambda i: (i, 0))
    full = lambda a: pl.BlockSpec(a.shape, lambda i: (0,) * a.ndim)
    slab = lambda r: pl.BlockSpec((tm // blk, r, blk), lambda i: (i, 0, 0))
    tok = lambda w, dt: (jax.ShapeDtypeStruct((n, w), dt), row(w))
    slb = lambda r: (jax.ShapeDtypeStruct((n // blk, r, blk), BF16), slab(r))
    outs = [tok(hm, BF16), tok(hm, BF16), slb(MLA_HEADS * _VROWS), tok(DSA_HEADS * DSA_HEAD_DIM, BF16),
            tok(LANES, BF16), slb(_VROWS), tok(IDX_HEADS * IDX_DIM, BF16), tok(LANES, BF16),
            (jax.ShapeDtypeStruct((IDX_HEADS, n), F32), pl.BlockSpec((IDX_HEADS, tm), lambda i: (0, i)))]
    consts = (g0, b0, w1, qg, wq, wqr, kg, wk, wvt, one, invf, sgn, cb, sb)
    return pl.pallas_call(
        functools.partial(_in_proj_kernel, blk=blk),
        grid=(n // tm,),
        in_specs=[row(d), row(1)] + [full(a) for a in consts],
        out_specs=[s for _, s in outs],
        out_shape=[s for s, _ in outs],
        compiler_params=pltpu.CompilerParams(dimension_semantics=("parallel",),
                                             vmem_limit_bytes=VMEM_LIMIT),
        name="in_proj",
    )(x2, pos2, *consts)


def _admissible(ck, tq, k0, q0):
    s = k0 + lax.broadcasted_iota(I32, (ck, tq), 0)
    t = q0 + lax.broadcasted_iota(I32, (ck, tq), 1)
    return (s >> _CHUNK_SHIFT) <= (t >> _CHUNK_SHIFT)


def _softmax_step(s, vt, m_ref, acc_ref, h):
    sb = s.astype(BF16)
    m_prev = m_ref[h]
    m_new = jnp.maximum(m_prev, jnp.max(sb, axis=0, keepdims=True).astype(F32))
    alpha = jnp.exp2(m_prev - m_new)
    p = jnp.exp2(sb - m_new.astype(BF16))
    acc_ref[h] = alpha * acc_ref[h] + _dot(vt, p)
    m_ref[h] = m_new


def _scores_first(s_ref, nh, produce, consume):
    for h in range(nh):
        s_ref[h] = produce(h)
    for h in range(nh):
        consume(h, s_ref[h])


def _finish(o_ref, acc_ref, nh, dv):
    ot = jnp.concatenate([acc_ref[h, :dv, :] / acc_ref[h, dv:dv + 1, :] for h in range(nh)], axis=0)
    o_ref[...] = ot.T.astype(BF16)


def _mla_kernel(q_ref, k_ref, vt_ref, o_ref, s_ref, m_ref, acc_ref, *, tq, ck):
    j = pl.program_id(1)
    q0 = j * tq
    m_ref[...] = jnp.full(m_ref.shape, NEG, F32)
    acc_ref[...] = jnp.zeros(acc_ref.shape, F32)
    nfull = (q0 + CHUNK) // ck

    def chunk(c, masked):
        k0 = pl.multiple_of(c * ck, ck)
        adm = _admissible(ck, tq, k0, q0) if masked else None

        def score(h):
            return _dot_nt(k_ref[pl.ds(k0, ck), h * LANES:(h + 1) * LANES], q_ref[:, h * LANES:(h + 1) * LANES])

        def use(h, s):
            if masked:
                s = jnp.where(adm, s, NEG)
            _softmax_step(s, vt_ref[c, h * _VROWS:(h + 1) * _VROWS, :], m_ref, acc_ref, h)

        _scores_first(s_ref, MLA_HEADS, score, use)

    def full_body(c, carry):
        chunk(c, False)
        return carry

    lax.fori_loop(0, nfull, full_body, 0)
    chunk(nfull, True)
    _finish(o_ref, acc_ref, MLA_HEADS, MLA_V)


def _mla_attn(qm, km, vmt, tq, ck):
    b, t, hm = qm.shape
    dv = MLA_HEADS * MLA_V
    assert ck % tq == 0 and tq > CHUNK and tq % CHUNK == 0 and t % ck == 0
    return pl.pallas_call(
        functools.partial(_mla_kernel, tq=tq, ck=ck),
        grid=(b, t // tq),
        in_specs=[pl.BlockSpec((None, tq, hm), lambda i, j: (i, j, 0)),
                  pl.BlockSpec((None, t, hm), lambda i, j: (i, 0, 0)),
                  pl.BlockSpec((None, t // ck, MLA_HEADS * _VROWS, ck), lambda i, j: (i, 0, 0, 0))],
        out_specs=pl.BlockSpec((None, tq, dv), lambda i, j: (i, j, 0)),
        out_shape=jax.ShapeDtypeStruct((b, t, dv), BF16),
        scratch_shapes=[pltpu.VMEM((MLA_HEADS, ck, tq), F32), pltpu.VMEM((MLA_HEADS, 1, tq), F32),
                        pltpu.VMEM((MLA_HEADS, _VROWS, tq), F32)],
        compiler_params=pltpu.CompilerParams(dimension_semantics=("parallel", "arbitrary"),
                                             vmem_limit_bytes=VMEM_LIMIT),
        name="mla_attn",
    )(qm, km, vmt)


def _dsa_kernel(qb_ref, qi_ref, wt_ref, kk_ref, vt_ref, ki_ref, bt_ref, o_ref,
                qs_ref, qim_ref, keys_ref, khi_ref, klo_ref, kls_ref, s_ref, thr_ref, cut_ref, m_ref, acc_ref,
                *, tq, topk):
    nh = DSA_HEADS
    ck = tq
    seq = kk_ref.shape[0]
    j = pl.program_id(1)
    q0 = j * tq
    nck = j + 1
    lane = lax.broadcasted_iota(I32, (tq, LANES), 1)

    for h in range(nh):
        pair = qb_ref[:, (h // 2) * LANES:(h // 2 + 1) * LANES].astype(F32)
        in_head = (lane >= (h % 2) * DSA_HEAD_DIM) & (lane < (h % 2 + 1) * DSA_HEAD_DIM)
        qs_ref[h * tq:(h + 1) * tq, :] = jnp.where(in_head, pair, 0.0).astype(BF16)
        grp = qi_ref[:, (h // 4) * LANES:(h // 4 + 1) * LANES].astype(F32)
        in_head = (lane >= (h % 4) * IDX_DIM) & (lane < (h % 4 + 1) * IDX_DIM)
        qim_ref[h * tq:(h + 1) * tq, :] = jnp.where(in_head, grp, 0.0).astype(BF16)

    w = wt_ref[...]

    def idx_chunk(c, masked):
        k0 = pl.multiple_of(c * ck, ck)
        kc = ki_ref[pl.ds(k0, ck), :]
        total = [jnp.zeros((ck, tq), F32)]

        def add(h, r):
            total[0] = total[0] + w[h:h + 1, :] * jnp.maximum(r, 0.0)

        _scores_first(s_ref, nh, lambda h: _dot_nt(kc, qim_ref[h * tq:(h + 1) * tq, :]), add)
        score = total[0]
        if masked:
            score = jnp.where(_admissible(ck, tq, k0, q0), score, NEG)
        score = jnp.where(score == 0.0, 0.0, score)
        bits = lax.bitcast_convert_type(score, I32)
        key = bits ^ ((bits >> 31) & 0x7FFFFFFF)
        keys_ref[c] = key
        khi_ref[c] = (key >> 16).astype(I16)
        klo_ref[c] = ((key & 0xFFFF) - 0x8000).astype(I16)

    def idx_body(c, carry):
        idx_chunk(c, False)
        return carry

    lax.fori_loop(0, j, idx_body, 0)
    idx_chunk(j, True)

    thr_ref[...] = jnp.full(thr_ref.shape, _KEY_NEG + 1, I32)
    cut_ref[...] = jnp.full(cut_ref.shape, seq, I32)

    def count16(ref, cand):
        rows = 2 * _SUBLANES

        def hits(c):
            part = jnp.where(ref[c] >= cand, jnp.int16(1), jnp.int16(0)).reshape(ck // rows, rows, tq)
            while part.shape[0] > 1:
                half = part.shape[0] // 2
                part = part[:half] + part[half:]
            return part[0]

        acc = lax.fori_loop(0, nck // 2, lambda i, a: a + hits(2 * i) + hits(2 * i + 1),
                            jnp.zeros((rows, tq), I16))
        acc = lax.fori_loop(nck // 2 * 2, nck, lambda c, a: a + hits(c), acc)
        return jnp.sum(acc.astype(F32), axis=0, keepdims=True)

    def greedy16(ref, need):
        def step(i, carry):
            v, cge, cgt = carry
            cand = v + lax.shift_left(jnp.int32(1), 15 - i)
            cnt = count16(ref, cand.astype(I16))
            ok = cnt >= need
            return jnp.where(ok, cand, v), jnp.where(ok, cnt, cge), jnp.where(ok, cgt, cnt)

        v0 = jnp.full((1, tq), _I16_MIN, I32)
        cge0 = jnp.full((1, tq), 1.0, F32) * (nck * ck).astype(F32)
        return lax.fori_loop(0, 16, step, (v0, cge0, jnp.zeros((1, tq), F32)))

    @pl.when(q0 + tq > topk)
    def _search():
        hi, cge_hi, cgt_hi = greedy16(khi_ref, float(topk))
        hi16 = hi.astype(I16)

        def mask_body(c, carry):
            kls_ref[c] = jnp.where(khi_ref[c] == hi16, klo_ref[c], jnp.int16(_I16_MIN))
            return carry

        lax.fori_loop(0, nck, mask_body, 0)
        lo, cge_lo, _ = greedy16(kls_ref, float(topk) - cgt_hi)
        thr = hi * 65536 + (lo + 0x8000)
        thr_ref[...] = thr
        cge = cgt_hi + jnp.where(lo == _I16_MIN, cge_hi - cgt_hi, cge_lo)
        excess = cge - float(topk)

        @pl.when(jnp.max(excess) > 0.0)
        def _ties():
            nbits = max(1, int(seq - 1).bit_length())
            lo16 = lo.astype(I16)

            def tie_body(c, carry):
                idx = (c * ck + lax.broadcasted_iota(I32, (ck, tq), 0)).astype(I16)
                tied = (khi_ref[c] == hi16) & (klo_ref[c] == lo16)
                kls_ref[c] = jnp.where(tied, idx, jnp.int16(-1))
                return carry

            lax.fori_loop(0, nck, tie_body, 0)

            def tstep(i, cut):
                cand = cut + lax.shift_left(jnp.int32(1), nbits - 1 - i)
                return jnp.where(count16(kls_ref, cand.astype(I16)) >= excess, cand, cut)

            cut = lax.fori_loop(0, nbits, tstep, jnp.zeros((1, tq), I32))
            cut_ref[...] = jnp.where(excess > 0.0, cut, seq)

    m_ref[...] = jnp.full(m_ref.shape, NEG, F32)
    acc_ref[...] = jnp.zeros(acc_ref.shape, F32)
    thr = thr_ref[...]
    cut = cut_ref[...]

    def att_chunk(c, bias_slab, own):
        k0 = pl.multiple_of(c * ck, ck)
        kc = kk_ref[pl.ds(k0, ck), :]
        key = keys_ref[c]
        idx = k0 + lax.broadcasted_iota(I32, (ck, tq), 0)
        sel = (key > thr) | ((key == thr) & (idx < cut))
        if own:
            sel = sel & _admissible(ck, tq, k0, q0)
        vt = vt_ref[c]

        def use(h, s):
            if bias_slab is not None:
                s = s + bt_ref[bias_slab, h]
            _softmax_step(jnp.where(sel, s, NEG), vt, m_ref, acc_ref, h)

        _scores_first(s_ref, nh, lambda h: _dot_nt(kc, qs_ref[h * tq:(h + 1) * tq, :]), use)

    def far_body(c, carry):
        att_chunk(c, None, False)
        return carry

    lax.fori_loop(0, j - 1, far_body, 0)

    @pl.when(j > 0)
    def _prev():
        att_chunk(j - 1, 0, False)

    att_chunk(j, 1, True)
    _finish(o_ref, acc_ref, nh, DSA_HEAD_DIM)


def _dsa_attn(qb, qi, wt, kk, vt, ki, bt, tq, topk):
    b, t, _ = qb.shape
    ck = tq
    assert tq == bt.shape[2] and tq % LANES == 0 and t % tq == 0 and t < -_I16_MIN
    nh = DSA_HEADS
    blk = lambda a: pl.BlockSpec((None, tq, a.shape[2]), lambda i, j: (i, j, 0))
    seqb = lambda a: pl.BlockSpec((None, t, a.shape[2]), lambda i, j: (i, 0, 0))
    return pl.pallas_call(
        functools.partial(_dsa_kernel, tq=tq, topk=topk),
        grid=(b, t // tq),
        in_specs=[blk(qb), blk(qi), pl.BlockSpec((nh, tq), lambda i, j: (0, i * (t // tq) + j)), seqb(kk),
                  pl.BlockSpec((None, t // ck, _VROWS, ck), lambda i, j: (i, 0, 0, 0)), seqb(ki),
                  pl.BlockSpec(bt.shape, lambda i, j: (0, 0, 0, 0))],
        out_specs=blk(qb),
        out_shape=jax.ShapeDtypeStruct(qb.shape, BF16),
        scratch_shapes=[pltpu.VMEM((nh * tq, LANES), BF16), pltpu.VMEM((nh * tq, LANES), BF16),
                        pltpu.VMEM((t // ck, ck, tq), I32), pltpu.VMEM((t // ck, ck, tq), I16),
                        pltpu.VMEM((t // ck, ck, tq), I16), pltpu.VMEM((t // ck, ck, tq), I16),
                        pltpu.VMEM((nh, ck, tq), F32),
                        pltpu.VMEM((1, tq), I32),
                        pltpu.VMEM((1, tq), I32), pltpu.VMEM((nh, 1, tq), F32),
                        pltpu.VMEM((nh, _VROWS, tq), F32)],
        compiler_params=pltpu.CompilerParams(dimension_semantics=("parallel", "arbitrary"),
                                             vmem_limit_bytes=VMEM_LIMIT),
        name="dsa_attn",
    )(qb, qi, wt, kk, vt, ki, bt)


_R_GRP = 0
_R_EXP = N_GROUPS


def _post_kernel(x_ref, oa_ref, ob_ref, g0_ref, b0_ref, wg_ref, bg_ref, wua_ref, wub_ref, wo_ref,
                 g1_ref, b1_ref, wr_ref, br_ref, hx_ref, gsel_ref, *, alpha):
    d = x_ref.shape[1]
    xn = _ln(x_ref[...], g0_ref[...], b0_ref[...])
    z = _dot(xn.astype(BF16), wg_ref[...]) + bg_ref[...]
    gates = 1.0 / (1.0 + jnp.exp(-z))
    mix = gates[:, :d] * _dot(oa_ref[...], wua_ref[...]) + gates[:, d:] * _dot(ob_ref[...], wub_ref[...])
    mixed = _dot(mix.astype(BF16), wo_ref[...])
    h = _ln(alpha * xn + mixed, g1_ref[...], b1_ref[...])
    hx_ref[:, :d] = h

    logits = _dot(h.astype(BF16), wr_ref[...]) + br_ref[...]
    lane = lax.broadcasted_iota(I32, logits.shape, 1).astype(F32)
    gmask = lane < N_GROUPS
    gl = jnp.where(gmask, logits, -jnp.inf)
    gmax = jnp.max(gl, axis=1, keepdims=True)
    g_sel = jnp.min(jnp.where(gl == gmax, lane, float(LANES)), axis=1, keepdims=True)
    g_w = 1.0 / jnp.sum(jnp.where(gmask, jnp.exp(gl - gmax), 0.0), axis=1, keepdims=True)
    e_lo = _R_EXP + g_sel * EXPERTS_PER_GROUP
    el = jnp.where((lane >= e_lo) & (lane < e_lo + EXPERTS_PER_GROUP), logits, -jnp.inf)
    m1 = jnp.max(el, axis=1, keepdims=True)
    i1 = jnp.min(jnp.where(el == m1, lane, float(LANES)), axis=1, keepdims=True)
    el2 = jnp.where(lane == i1, -jnp.inf, el)
    m2 = jnp.max(el2, axis=1, keepdims=True)
    i2 = jnp.min(jnp.where(el2 == m2, lane, float(LANES)), axis=1, keepdims=True)
    e21 = jnp.exp(m2 - m1)
    w1 = g_w / (1.0 + e21)
    w2 = g_w * e21 / (1.0 + e21)
    hx_ref[:, d:] = jnp.where(lane == i1, w1, 0.0) + jnp.where(lane == i2, w2, 0.0)
    gsel_ref[...] = g_sel.astype(I32)


def _post_attn(x2, oa, ob, g0, b0, wg, bg, wua, wub, wo, g1, b1, wr, br, alpha, tm):
    n, d = x2.shape
    row = lambda w: pl.BlockSpec((tm, w), lambda i: (i, 0))
    full = lambda a: pl.BlockSpec(a.shape, lambda i: (0,) * a.ndim)
    return pl.pallas_call(
        functools.partial(_post_kernel, alpha=alpha),
        grid=(n // tm,),
        in_specs=[row(d), row(oa.shape[1]), row(ob.shape[1])]
                 + [full(a) for a in (g0, b0, wg, bg, wua, wub, wo, g1, b1, wr, br)],
        out_specs=[row(d + LANES), row(1)],
        out_shape=[jax.ShapeDtypeStruct((n, d + LANES), F32), jax.ShapeDtypeStruct((n, 1), I32)],
        compiler_params=pltpu.CompilerParams(dimension_semantics=("parallel",),
                                             vmem_limit_bytes=VMEM_LIMIT),
        name="post_attn",
    )(x2, oa, ob, g0, b0, wg, bg, wua, wub, wo, g1, b1, wr, br)


def _moe_kernel(tg_ref, nv_ref, src_ref, src1_ref, src2_ref, hx_hbm, weg_ref, weu_ref, wed_ref, g2_ref, b2_ref,
                out_hbm, gbuf, obuf, gsem, ssem, *, alpha, tm, d):
    k = pl.program_id(0)
    nk = pl.num_programs(0)
    slot = k % 2
    gslot = k % 3
    nrun = tm // LANES

    def gather_copy(tok, r, s):
        return pltpu.make_async_copy(hx_hbm.at[pl.ds(tok, 1)], gbuf.at[s, pl.ds(r, 1)], gsem.at[s])

    def scatter_copy(tok, r, s):
        return pltpu.make_async_copy(obuf.at[s, pl.ds(r, 1)], out_hbm.at[pl.ds(tok, 1)], ssem.at[s])

    def start_gather(tile_src_ref, s):
        for g in range(nrun):
            for c in range(LANES):
                gather_copy(tile_src_ref[g, c], g * LANES + c, s).start()

    def start_scatter(n_rows, s):
        for g in range(nrun):
            @pl.when(n_rows >= (g + 1) * LANES)
            def _run():
                for c in range(LANES):
                    scatter_copy(src_ref[g, c], g * LANES + c, s).start()

        def row(r, carry):
            scatter_copy(src_ref[r // LANES, r % LANES], r, s).start()
            return carry

        lax.fori_loop(n_rows // LANES * LANES, n_rows, row, 0)

    def by_slot(fn, s_of_slot):
        for par in (0, 1):
            pl.when(slot == par)(functools.partial(fn, s_of_slot(par)))

    def wait_scatter(n_rows, s):
        def waiter(rows):
            def body(i, carry):
                pltpu.make_async_copy(obuf.at[s, pl.ds(0, rows)], out_hbm.at[pl.ds(0, rows)], ssem.at[s]).wait()
                return carry
            return body

        lax.fori_loop(0, n_rows // LANES, waiter(LANES), 0)
        lax.fori_loop(0, n_rows % LANES // _SUBLANES, waiter(_SUBLANES), 0)
        lax.fori_loop(0, n_rows % _SUBLANES, waiter(1), 0)

    def wait_gather(s):
        pltpu.make_async_copy(hx_hbm.at[pl.ds(0, tm)], gbuf.at[s], gsem.at[s]).wait()

    @pl.when(k == 0)
    def _first():
        def rows(tile_src_ref, s):
            def body(r, carry):
                gather_copy(tile_src_ref[r // LANES, r % LANES], r, s).start()
                return carry
            lax.fori_loop(0, tm, body, 0)
        rows(src_ref, 0)
        rows(src1_ref, 1)

    wait_gather(gslot)

    @pl.when(k >= 2)
    def _free_obuf():
        wait_scatter(nv_ref[k - 2], slot)

    x = gbuf[gslot]
    h = x[:, :d]
    comb = x[:, d:]
    hb = h.astype(BF16)
    lane = lax.broadcasted_iota(I32, comb.shape, 1)
    e0 = _R_EXP + tg_ref[k] * EXPERTS_PER_GROUP
    acc = jnp.zeros((tm, d), F32)
    for e in range(EXPERTS_PER_GROUP):
        a = _dot(hb, weg_ref[e])
        u = _dot(hb, weu_ref[e])
        cw = jnp.sum(jnp.where(lane == e0 + e, comb, 0.0), axis=1, keepdims=True)
        hid = (a / (1.0 + jnp.exp(-a))) * u * cw
        acc = acc + _dot(hid.astype(BF16), wed_ref[e])
    obuf[slot] = _ln(alpha * h + acc, g2_ref[...], b2_ref[...])
    start_gather(src2_ref, (k + 2) % 3)

    nv = nv_ref[k]
    by_slot(lambda s: start_scatter(nv, s), lambda par: par)

    @pl.when(k == nk - 1)
    def _drain():
        wait_scatter(nv, slot)

        wait_scatter(nv_ref[k - 1], 1 - slot)
        wait_gather((k + 1) % 3)
        wait_gather((k + 2) % 3)


def _moe(hx, tile_group, n_valid, src, weg, weu, wed, g2, b2, alpha, tm, n):
    d = hx.shape[1] - LANES
    nk = src.shape[0]
    ne, _, f = weg.shape
    gsz = EXPERTS_PER_GROUP
    grid_spec = pltpu.PrefetchScalarGridSpec(
        num_scalar_prefetch=2,
        grid=(nk,),
        in_specs=[pl.BlockSpec((None, tm // LANES, LANES), lambda k, tg, nv: (k, 0, 0),
                               memory_space=pltpu.SMEM),
                  pl.BlockSpec((None, tm // LANES, LANES),
                               lambda k, tg, nv: (jnp.minimum(k + 1, nk - 1), 0, 0), memory_space=pltpu.SMEM),
                  pl.BlockSpec((None, tm // LANES, LANES),
                               lambda k, tg, nv: (jnp.minimum(k + 2, nk - 1), 0, 0), memory_space=pltpu.SMEM),
                  pl.BlockSpec(memory_space=pl.ANY),
                  pl.BlockSpec((gsz, d, f), lambda k, tg, nv: (tg[k], 0, 0)),
                  pl.BlockSpec((gsz, d, f), lambda k, tg, nv: (tg[k], 0, 0)),
                  pl.BlockSpec((gsz, f, d), lambda k, tg, nv: (tg[k], 0, 0)),
                  pl.BlockSpec(g2.shape, lambda k, tg, nv: (0, 0)),
                  pl.BlockSpec(b2.shape, lambda k, tg, nv: (0, 0))],
        out_specs=pl.BlockSpec(memory_space=pl.ANY),
        scratch_shapes=[pltpu.VMEM((3, tm, d + LANES), F32), pltpu.VMEM((2, tm, d), F32),
                        pltpu.SemaphoreType.DMA((3,)), pltpu.SemaphoreType.DMA((2,))])
    assert nk >= 3
    return pl.pallas_call(
        functools.partial(_moe_kernel, alpha=alpha, tm=tm, d=d),
        grid_spec=grid_spec,
        out_shape=jax.ShapeDtypeStruct((n, d), F32),
        compiler_params=pltpu.CompilerParams(dimension_semantics=("arbitrary",),
                                             vmem_limit_bytes=VMEM_LIMIT),
        name="moe",
    )(tile_group, n_valid, src, src, src, hx, weg, weu, wed, g2, b2)


def _group_tiles(gsel, tm):
    n = gsel.shape[0]
    n_tiles = n // tm + N_GROUPS
    onehot = (gsel[:, None] == jnp.arange(N_GROUPS, dtype=I32)[None, :]).astype(I32)
    csum = jnp.cumsum(onehot, axis=0)
    rank = jnp.sum(csum * onehot, axis=1) - 1
    counts = csum[-1]
    padded = (counts + tm - 1) // tm * tm
    ends = jnp.cumsum(padded)
    offs = ends - padded
    dest = offs[gsel] + rank
    src = jnp.zeros((n_tiles * tm,), I32).at[dest].set(jnp.arange(n, dtype=I32))
    tile_start = jnp.arange(n_tiles, dtype=I32) * tm
    tile_group = jnp.minimum(jnp.sum((tile_start[:, None] >= ends[None, :]).astype(I32), axis=1), N_GROUPS - 1)
    n_valid = jnp.clip(offs[tile_group] + counts[tile_group] - tile_start, 0, tm)
    return tile_group.astype(I32), n_valid.astype(I32), src.reshape(n_tiles, tm // LANES, LANES)


def _layout_weights(w_in, w_uq, w_uk, w_uv):
    d = w_in.shape[0]
    sizes = (MLA_Q_RANK, MLA_KV_RANK, MLA_ROPE, DSA_HEADS * DSA_HEAD_DIM, DSA_HEAD_DIM, DSA_HEAD_DIM,
             IDX_HEADS * IDX_DIM, IDX_DIM, IDX_HEADS)
    offs = np.cumsum((0,) + sizes)
    c_q, c_kv, k_rope, q_b, k_b, v_b, q_idx, k_idx, w_idx = (
        w_in[:, int(offs[i]):int(offs[i + 1])] for i in range(len(sizes)))
    half = MLA_ROPE // 2
    z = lambda w: jnp.zeros((d, w), w_in.dtype)
    pad = LANES - MLA_NOPE - MLA_ROPE
    grp_a = jnp.concatenate([w_idx, z(MLA_NOPE - IDX_HEADS), k_rope, z(pad)], axis=1)
    grp_b = jnp.concatenate([z(MLA_NOPE), k_rope[:, half:], k_rope[:, :half], z(pad)], axis=1)
    w1 = jnp.concatenate([c_q, c_kv, grp_a, grp_b, q_b * DSA_HEAD_DIM ** -0.5, k_b, k_b, v_b, v_b, q_idx,
                          k_idx, k_idx, k_idx, k_idx], axis=1)
    assert w1.shape[1] == _C_END

    dk = MLA_NOPE + MLA_ROPE
    rq = w_uq.shape[0]
    zq = lambda w: jnp.zeros((rq, w), w_uq.dtype)
    wq_parts, wqr_parts, wk_parts = [], [], []
    for h in range(MLA_HEADS):
        nope = w_uq[:, h * dk:h * dk + MLA_NOPE]
        rope = w_uq[:, h * dk + MLA_NOPE:(h + 1) * dk]
        wq_parts += [nope, rope, zq(pad)]
        wqr_parts += [zq(MLA_NOPE), rope[:, half:], rope[:, :half], zq(pad)]
        wk_parts += [w_uk[:, h * MLA_NOPE:(h + 1) * MLA_NOPE],
                     jnp.zeros((w_uk.shape[0], LANES - MLA_NOPE), w_uk.dtype)]
    wq = jnp.concatenate(wq_parts, axis=1)
    wqr = jnp.concatenate(wqr_parts, axis=1)
    wk = jnp.concatenate(wk_parts, axis=1)

    inv = ROPE_THETA ** (-np.arange(half, dtype=np.float32) / half)
    invf = np.zeros((1, LANES), np.float32)
    invf[0, MLA_NOPE:MLA_NOPE + half] = inv
    invf[0, MLA_NOPE + half:MLA_NOPE + MLA_ROPE] = inv
    sgn = np.zeros((1, LANES), np.float32)
    sgn[0, MLA_NOPE:MLA_NOPE + half] = -1.0
    sgn[0, MLA_NOPE + half:MLA_NOPE + MLA_ROPE] = 1.0
    ang_b = np.arange(CHUNK, dtype=np.float64)[:, None] * invf.astype(np.float64)
    cb, sb = np.cos(ang_b).astype(np.float32), np.sin(ang_b).astype(np.float32)

    kvr = w_uv.shape[0]
    wvt = jnp.concatenate(
        [jnp.concatenate([w_uv[:, h * MLA_V:(h + 1) * MLA_V].T, jnp.zeros((_VROWS - MLA_V, kvr), w_uv.dtype)], axis=0)
         for h in range(MLA_HEADS)], axis=0)
    one = np.zeros((MLA_HEADS * _VROWS, 1), np.float32)
    one[MLA_V::_VROWS, 0] = 1.0
    return (w1.astype(BF16), wq.astype(BF16), wqr.astype(BF16), wk.astype(BF16), wvt.astype(BF16),
            jnp.asarray(one), jnp.asarray(invf), jnp.asarray(sgn), jnp.asarray(cb), jnp.asarray(sb))


def kernel(x, positions, ln0_g, ln0_b, w_in, q_norm_g, w_uq, kv_norm_g, w_uk, w_uv, rel_bias, w_up_a, w_up_b,
           w_gate, b_gate, w_o, ln1_g, ln1_b, w_grp, b_grp, w_rt, b_rt, w_exp_gate, w_exp_up, w_exp_down,
           ln2_g, ln2_b):
    bsz, seq, d = x.shape
    depth = w_in.shape[0]
    assert depth == 1
    alpha = (2.0 * depth) ** 0.25
    n = bsz * seq
    topk = min(IDX_TOPK_MAX, seq // 4)
    r2 = lambda v: v.reshape(1, -1)

    x2 = x.reshape(n, d)
    pos2 = positions.reshape(n, 1)
    w1, wq, wqr, wk, wvt, one, invf, sgn, cb, sb = _layout_weights(w_in[0], w_uq[0], w_uk[0], w_uv[0])

    blk = 256
    bt = _bias_tiles(rel_bias, blk)
    qm, km, vmt, qb, kk, vvt, qi, ki, wt = _in_proj(
        x2, pos2, r2(ln0_g), r2(ln0_b), w1, r2(q_norm_g[0]), wq, wqr, r2(kv_norm_g[0]), wk, wvt, one, invf, sgn,
        cb, sb, tm=512, blk=blk)
    b3 = lambda a: a.reshape(bsz, seq, a.shape[1])
    c4 = lambda a: a.reshape(bsz, seq // blk, a.shape[1], blk)
    o_a = _mla_attn(b3(qm), b3(km), c4(vmt), tq=blk, ck=blk)
    o_b = _dsa_attn(b3(qb), b3(qi), wt, b3(kk), c4(vvt), b3(ki), bt, tq=blk, topk=topk)

    wr = jnp.concatenate([w_grp[0], w_rt[0], jnp.zeros((d, LANES - N_GROUPS - N_EXPERTS), F32)], axis=1)
    br = jnp.concatenate([b_grp[0], b_rt[0], jnp.zeros((LANES - N_GROUPS - N_EXPERTS,), F32)])
    hx, gsel = _post_attn(
        x2, o_a.reshape(n, -1), o_b.reshape(n, -1), r2(ln0_g), r2(ln0_b), w_gate[0].astype(BF16), r2(b_gate[0]),
        w_up_a[0].astype(BF16), w_up_b[0].astype(BF16), w_o[0].astype(BF16), r2(ln1_g[0]), r2(ln1_b[0]),
        wr.astype(BF16), r2(br), alpha, tm=512)
    tm_moe = 512
    tile_group, n_valid, src = _group_tiles(gsel[:, 0], tm_moe)
    out = _moe(hx, tile_group, n_valid, src, w_exp_gate[0].astype(BF16), w_exp_up[0].astype(BF16),
               w_exp_down[0].astype(BF16), r2(ln2_g[0]), r2(ln2_b[0]), alpha, tm_moe, n)
    return out.reshape(bsz, seq, d)
```

```python
import functools
import math

import numpy as np
import jax
import jax.numpy as jnp
from jax import lax
from jax.experimental import pallas as pl
from jax.experimental.pallas import tpu as pltpu

F32 = jnp.float32
BF16 = jnp.bfloat16
I32 = jnp.int32
I16 = jnp.int16

CHUNK = 64
MLA_HEADS = 8
MLA_Q_RANK = 256
MLA_KV_RANK = 128
MLA_NOPE = 64
MLA_ROPE = 32
MLA_V = 64
ROPE_THETA = 10000.0
DSA_HEADS = 8
DSA_HEAD_DIM = 64
IDX_HEADS = 8
IDX_DIM = 32
IDX_TOPK_MAX = 256
REL_BUCKETS = 32
REL_MAX_DIST = 128
N_GROUPS = 4
EXPERTS_PER_GROUP = 8
N_EXPERTS = N_GROUPS * EXPERTS_PER_GROUP
EXPERT_HIDDEN = 256
LN_EPS = 1e-5
RMS_EPS = 1e-6
NEG = -1e30

LANES = 128
_SUBLANES = 8
_VROWS = 80
VMEM_LIMIT = 48 * 1024 * 1024

_KEY_NEG = int(np.array(NEG, np.float32).view(np.int32)) ^ 0x7FFFFFFF
_I16_MIN = -(2 ** 15)
_LOG2E = math.log2(math.e)
_CHUNK_SHIFT = CHUNK.bit_length() - 1
assert 1 << _CHUNK_SHIFT == CHUNK


def _ln(x, g, b):
    mu = jnp.mean(x, axis=-1, keepdims=True)
    xc = x - mu
    var = jnp.mean(xc * xc, axis=-1, keepdims=True)
    return xc * lax.rsqrt(var + LN_EPS) * g + b


def _rms(x, g):
    return x * lax.rsqrt(jnp.mean(x * x, axis=-1, keepdims=True) + RMS_EPS) * g


def _dot(a, b):
    return jnp.dot(a, b, preferred_element_type=F32)


def _dot_nt(a, b):
    return lax.dot_general(a, b, (((1,), (1,)), ((), ())), preferred_element_type=F32)


def _bias_kernel(rb_ref, o_ref):
    nb = REL_BUCKETS // 2
    max_exact = nb // 2
    blk = o_ref.shape[2]
    s = lax.broadcasted_iota(I32, (blk, blk), 0)
    t = lax.broadcasted_iota(I32, (blk, blk), 1)
    for which, off in ((0, -blk), (1, 0)):
        rel = s - t + off
        ret = jnp.where(rel > 0, nb, 0)
        n = jnp.abs(rel)
        large = max_exact + (jnp.log(jnp.maximum(n, 1).astype(F32) / max_exact)
                             / math.log(REL_MAX_DIST / max_exact) * (nb - max_exact)).astype(I32)
        large = jnp.minimum(large, nb - 1)
        bucket = ret + jnp.where(n < max_exact, n, large)
        for h in range(DSA_HEADS):
            acc = jnp.zeros((blk, blk), F32)
            for b in range(REL_BUCKETS):
                acc = jnp.where(bucket == b, rb_ref[b, h], acc)
            o_ref[which, h] = (acc - rb_ref[nb - 1, h]) * _LOG2E


def _bias_tiles(rel_bias, blk):
    assert blk >= REL_MAX_DIST
    return pl.pallas_call(
        _bias_kernel,
        out_shape=jax.ShapeDtypeStruct((2, DSA_HEADS, blk, blk), F32),
        in_specs=[pl.BlockSpec(memory_space=pltpu.SMEM)],
        out_specs=pl.BlockSpec(memory_space=pltpu.VMEM),
        name="bias_tiles",
    )(rel_bias)


_C_Q, _C_KV, _C_RA, _C_RB, _C_QB, _C_KK, _C_VV, _C_QI, _C_KI, _C_END = (
    0, 256, 384, 512, 640, 1152, 1280, 1408, 1664, 1792)


def _value_rows(vt, dv):
    t = vt.shape[1]
    return jnp.concatenate([vt, jnp.ones((1, t), F32), jnp.zeros((_VROWS - dv - 1, t), F32)], axis=0)


def _in_proj_kernel(x_ref, pos_ref, g0_ref, b0_ref, w1_ref, qg_ref, wq_ref, wqr_ref, kg_ref, wk_ref,
                    wvt_ref, one_ref, invf_ref, sgn_ref, cb_ref, sb_ref,
                    qm_ref, km_ref, vmt_ref, qb_ref, kk_ref, vvt_ref, qi_ref, ki_ref, wt_ref, *, blk):
    tm = x_ref.shape[0]
    xn = _ln(x_ref[...], g0_ref[...], b0_ref[...])
    proj = _dot(xn.astype(BF16), w1_ref[...])
    qb_ref[...] = (proj[:, _C_QB:_C_KK] * _LOG2E).astype(BF16)
    kk_ref[...] = proj[:, _C_KK:_C_VV].astype(BF16)
    qi_ref[...] = proj[:, _C_QI:_C_KI].astype(BF16)
    ki_ref[...] = proj[:, _C_KI:_C_END].astype(BF16)
    ga = proj[:, _C_RA:_C_RB]
    gb = proj[:, _C_RB:_C_QB]
    wt_ref[...] = ga.T[0:IDX_HEADS, :] * (IDX_HEADS ** -0.5 * IDX_DIM ** -0.5)
    vvt = _value_rows(proj[:, _C_VV:_C_QI].T[0:DSA_HEAD_DIM, :], DSA_HEAD_DIM).astype(BF16)
    for c in range(tm // blk):
        vvt_ref[c] = vvt[:, c * blk:(c + 1) * blk]

    groups = tm // CHUNK
    a = (pos_ref[0:1, :] >> _CHUNK_SHIFT) + lax.broadcasted_iota(I32, (groups, 1), 0)
    ang = (a * CHUNK).astype(F32) * invf_ref[...]
    ca, sa = jnp.cos(ang), jnp.sin(ang)
    cb, sb = cb_ref[...], sb_ref[...]
    cosv = jnp.concatenate([ca[g:g + 1] * cb - sa[g:g + 1] * sb for g in range(groups)], axis=0)
    sinv = jnp.concatenate([sa[g:g + 1] * cb + ca[g:g + 1] * sb for g in range(groups)], axis=0) * sgn_ref[...]
    lane = lax.broadcasted_iota(I32, cosv.shape, 1)
    rope_lane = (lane >= MLA_NOPE) & (lane < MLA_NOPE + MLA_ROPE)
    kr = jnp.where(rope_lane, ga * cosv + gb * sinv, 0.0)

    scale = (MLA_NOPE + MLA_ROPE) ** -0.5 * _LOG2E
    cos8 = jnp.concatenate([cosv * scale] * MLA_HEADS, axis=1)
    sin8 = jnp.concatenate([sinv * scale] * MLA_HEADS, axis=1)
    cqn = _rms(proj[:, _C_Q:_C_KV], qg_ref[...]).astype(BF16)
    q = _dot(cqn, wq_ref[...]) * cos8 + _dot(cqn, wqr_ref[...]) * sin8
    qm_ref[...] = q.astype(BF16)

    ckn = _rms(proj[:, _C_KV:_C_RA], kg_ref[...]).astype(BF16)
    k = _dot(ckn, wk_ref[...]) + jnp.concatenate([kr] * MLA_HEADS, axis=1)
    km_ref[...] = k.astype(BF16)
    vmt = (_dot_nt(wvt_ref[...], ckn) + one_ref[...]).astype(BF16)
    for c in range(tm // blk):
        vmt_ref[c] = vmt[:, c * blk:(c + 1) * blk]


def _in_proj(x2, pos2, g0, b0, w1, qg, wq, wqr, kg, wk, wvt, one, invf, sgn, cb, sb, tm, blk):
    n, d = x2.shape
    assert tm % blk == 0 and blk % CHUNK == 0
    hm = MLA_HEADS * LANES
    row = lambda w: pl.BlockSpec((tm, w), lambda i: (i, 0))
    full = lambda a: pl.BlockSpec(a.shape, lambda i: (0,) * a.ndim)
    slab = lambda r: pl.BlockSpec((tm // blk, r, blk), lambda i: (i, 0, 0))
    tok = lambda w, dt: (jax.ShapeDtypeStruct((n, w), dt), row(w))
    slb = lambda r: (jax.ShapeDtypeStruct((n // blk, r, blk), BF16), slab(r))
    outs = [tok(hm, BF16), tok(hm, BF16), slb(MLA_HEADS * _VROWS), tok(DSA_HEADS * DSA_HEAD_DIM, BF16),
            tok(LANES, BF16), slb(_VROWS), tok(IDX_HEADS * IDX_DIM, BF16), tok(LANES, BF16),
            (jax.ShapeDtypeStruct((IDX_HEADS, n), F32), pl.BlockSpec((IDX_HEADS, tm), lambda i: (0, i)))]
    consts = (g0, b0, w1, qg, wq, wqr, kg, wk, wvt, one, invf, sgn, cb, sb)
    return pl.pallas_call(
        functools.partial(_in_proj_kernel, blk=blk),
        grid=(n // tm,),
        in_specs=[row(d), row(1)] + [full(a) for a in consts],
        out_specs=[s for _, s in outs],
        out_shape=[s for s, _ in outs],
        compiler_params=pltpu.CompilerParams(dimension_semantics=("parallel",),
                                             vmem_limit_bytes=VMEM_LIMIT),
        name="in_proj",
    )(x2, pos2, *consts)


def _admissible(ck, tq, k0, q0):
    s = k0 + lax.broadcasted_iota(I32, (ck, tq), 0)
    t = q0 + lax.broadcasted_iota(I32, (ck, tq), 1)
    return (s >> _CHUNK_SHIFT) <= (t >> _CHUNK_SHIFT)


def _softmax_step(s, vt, m_ref, acc_ref, h):
    sb = s.astype(BF16)
    m_prev = m_ref[h]
    m_new = jnp.maximum(m_prev, jnp.max(sb, axis=0, keepdims=True).astype(F32))
    alpha = jnp.exp2(m_prev - m_new)
    p = jnp.exp2(sb - m_new.astype(BF16))
    acc_ref[h] = alpha * acc_ref[h] + _dot(vt, p)
    m_ref[h] = m_new


_STEP_WIDTHS = (4, 2, 1)


def _scores_first(s_ref, nh, produce, consume):
    for h in range(nh):
        s_ref[h] = produce(h)
    for h in range(nh):
        consume(h, s_ref[h])


def _finish(o_ref, acc_ref, nh, dv):
    ot = jnp.concatenate([acc_ref[h, :dv, :] / acc_ref[h, dv:dv + 1, :] for h in range(nh)], axis=0)
    o_ref[...] = ot.T.astype(BF16)


def _mla_kernel(q_ref, k_ref, vt_ref, o_ref, s_ref, m_ref, acc_ref, *, tq, ck):
    j = pl.program_id(1)
    q0 = j * tq
    m_ref[...] = jnp.full(m_ref.shape, NEG, F32)
    acc_ref[...] = jnp.zeros(acc_ref.shape, F32)
    nfull = (q0 + CHUNK) // ck

    def chunk(c, width, masked):
        k0 = pl.multiple_of(c * ck, ck)
        rows = width * ck
        adm = _admissible(rows, tq, k0, q0) if masked else None

        def score(h):
            return _dot_nt(k_ref[pl.ds(k0, rows), h * LANES:(h + 1) * LANES], q_ref[:, h * LANES:(h + 1) * LANES])

        def use(h, s):
            if masked:
                s = jnp.where(adm, s, NEG)
            vt = jnp.concatenate([vt_ref[c + i, h * _VROWS:(h + 1) * _VROWS, :] for i in range(width)], axis=1)
            _softmax_step(s, vt, m_ref, acc_ref, h)

        _scores_first(s_ref.at[:, 0:rows], MLA_HEADS, score, use)

    done = 0
    for width in _STEP_WIDTHS:
        steps = (nfull - done) // width

        def body(i, carry, width=width, done=done):
            chunk(done + i * width, width, False)
            return carry

        lax.fori_loop(0, steps, body, 0)
        done = done + steps * width
    chunk(nfull, 1, True)
    _finish(o_ref, acc_ref, MLA_HEADS, MLA_V)


def _mla_attn(qm, km, vmt, tq, ck):
    b, t, hm = qm.shape
    dv = MLA_HEADS * MLA_V
    assert ck % tq == 0 and tq > CHUNK and tq % CHUNK == 0 and t % ck == 0
    return pl.pallas_call(
        functools.partial(_mla_kernel, tq=tq, ck=ck),
        grid=(b, t // tq),
        in_specs=[pl.BlockSpec((None, tq, hm), lambda i, j: (i, j, 0)),
                  pl.BlockSpec((None, t, hm), lambda i, j: (i, 0, 0)),
                  pl.BlockSpec((None, t // ck, MLA_HEADS * _VROWS, ck), lambda i, j: (i, 0, 0, 0))],
        out_specs=pl.BlockSpec((None, tq, dv), lambda i, j: (i, j, 0)),
        out_shape=jax.ShapeDtypeStruct((b, t, dv), BF16),
        scratch_shapes=[pltpu.VMEM((MLA_HEADS, _STEP_WIDTHS[0] * ck, tq), F32),
                        pltpu.VMEM((MLA_HEADS, 1, tq), F32),
                        pltpu.VMEM((MLA_HEADS, _VROWS, tq), F32)],
        compiler_params=pltpu.CompilerParams(dimension_semantics=("parallel", "arbitrary"),
                                             vmem_limit_bytes=VMEM_LIMIT),
        name="mla_attn",
    )(qm, km, vmt)


def _dsa_kernel(qb_ref, qi_ref, wt_ref, kk_ref, vt_ref, ki_ref, bt_ref, o_ref,
                qs_ref, qim_ref, keys_ref, khi_ref, klo_ref, kls_ref, s_ref, thr_ref, cut_ref, m_ref, acc_ref,
                *, tq, topk):
    nh = DSA_HEADS
    ck = tq
    seq = kk_ref.shape[0]
    j = pl.program_id(1)
    q0 = j * tq
    nck = j + 1
    lane = lax.broadcasted_iota(I32, (tq, LANES), 1)

    for h in range(nh):
        pair = qb_ref[:, (h // 2) * LANES:(h // 2 + 1) * LANES].astype(F32)
        in_head = (lane >= (h % 2) * DSA_HEAD_DIM) & (lane < (h % 2 + 1) * DSA_HEAD_DIM)
        qs_ref[h * tq:(h + 1) * tq, :] = jnp.where(in_head, pair, 0.0).astype(BF16)
        grp = qi_ref[:, (h // 4) * LANES:(h // 4 + 1) * LANES].astype(F32)
        in_head = (lane >= (h % 4) * IDX_DIM) & (lane < (h % 4 + 1) * IDX_DIM)
        qim_ref[h * tq:(h + 1) * tq, :] = jnp.where(in_head, grp, 0.0).astype(BF16)

    w = wt_ref[...]

    def widest_first(n_chunks, fn):
        done = 0
        for width in _STEP_WIDTHS:
            steps = (n_chunks - done) // width

            def body(i, carry, width=width, done=done):
                fn(done + i * width, width)
                return carry

            lax.fori_loop(0, steps, body, 0)
            done = done + steps * width

    def idx_chunk(c, width, masked):
        k0 = pl.multiple_of(c * ck, ck)
        rows = width * ck
        kc = ki_ref[pl.ds(k0, rows), :]
        total = [jnp.zeros((rows, tq), F32)]

        def add(h, r):
            total[0] = total[0] + w[h:h + 1, :] * jnp.maximum(r, 0.0)

        _scores_first(s_ref.at[:, 0:rows], nh, lambda h: _dot_nt(kc, qim_ref[h * tq:(h + 1) * tq, :]), add)
        score = total[0]
        if masked:
            score = jnp.where(_admissible(rows, tq, k0, q0), score, NEG)
        score = jnp.where(score == 0.0, 0.0, score)
        bits = lax.bitcast_convert_type(score, I32)
        key = bits ^ ((bits >> 31) & 0x7FFFFFFF)
        for i in range(width):
            ki = key[i * ck:(i + 1) * ck]
            keys_ref[c + i] = ki
            khi_ref[c + i] = (ki >> 16).astype(I16)
            klo_ref[c + i] = ((ki & 0xFFFF) - 0x8000).astype(I16)

    widest_first(j, lambda c, width: idx_chunk(c, width, False))
    idx_chunk(j, 1, True)

    thr_ref[...] = jnp.full(thr_ref.shape, _KEY_NEG + 1, I32)
    cut_ref[...] = jnp.full(cut_ref.shape, seq, I32)

    def count16(ref, cand):
        rows = 2 * _SUBLANES

        def hits(c):
            part = jnp.where(ref[c] >= cand, jnp.int16(1), jnp.int16(0)).reshape(ck // rows, rows, tq)
            while part.shape[0] > 1:
                half = part.shape[0] // 2
                part = part[:half] + part[half:]
            return part[0]

        acc = lax.fori_loop(0, nck // 2, lambda i, a: a + hits(2 * i) + hits(2 * i + 1),
                            jnp.zeros((rows, tq), I16))
        acc = lax.fori_loop(nck // 2 * 2, nck, lambda c, a: a + hits(c), acc)
        return jnp.sum(acc.astype(F32), axis=0, keepdims=True)

    def greedy16(ref, need):
        def step(i, carry):
            v, cge, cgt = carry
            cand = v + lax.shift_left(jnp.int32(1), 15 - i)
            cnt = count16(ref, cand.astype(I16))
            ok = cnt >= need
            return jnp.where(ok, cand, v), jnp.where(ok, cnt, cge), jnp.where(ok, cgt, cnt)

        v0 = jnp.full((1, tq), _I16_MIN, I32)
        cge0 = jnp.full((1, tq), 1.0, F32) * (nck * ck).astype(F32)
        return lax.fori_loop(0, 16, step, (v0, cge0, jnp.zeros((1, tq), F32)))

    @pl.when(q0 + tq > topk)
    def _search():
        hi, cge_hi, cgt_hi = greedy16(khi_ref, float(topk))
        hi16 = hi.astype(I16)

        def mask_body(c, carry):
            kls_ref[c] = jnp.where(khi_ref[c] == hi16, klo_ref[c], jnp.int16(_I16_MIN))
            return carry

        lax.fori_loop(0, nck, mask_body, 0)
        lo, cge_lo, _ = greedy16(kls_ref, float(topk) - cgt_hi)
        thr = hi * 65536 + (lo + 0x8000)
        thr_ref[...] = thr
        cge = cgt_hi + jnp.where(lo == _I16_MIN, cge_hi - cgt_hi, cge_lo)
        excess = cge - float(topk)

        @pl.when(jnp.max(excess) > 0.0)
        def _ties():
            nbits = max(1, int(seq - 1).bit_length())
            lo16 = lo.astype(I16)

            def tie_body(c, carry):
                idx = (c * ck + lax.broadcasted_iota(I32, (ck, tq), 0)).astype(I16)
                tied = (khi_ref[c] == hi16) & (klo_ref[c] == lo16)
                kls_ref[c] = jnp.where(tied, idx, jnp.int16(-1))
                return carry

            lax.fori_loop(0, nck, tie_body, 0)

            def tstep(i, cut):
                cand = cut + lax.shift_left(jnp.int32(1), nbits - 1 - i)
                return jnp.where(count16(kls_ref, cand.astype(I16)) >= excess, cand, cut)

            cut = lax.fori_loop(0, nbits, tstep, jnp.zeros((1, tq), I32))
            cut_ref[...] = jnp.where(excess > 0.0, cut, seq)

    m_ref[...] = jnp.full(m_ref.shape, NEG, F32)
    acc_ref[...] = jnp.zeros(acc_ref.shape, F32)
    thr = thr_ref[...]
    cut = cut_ref[...]

    def att_chunk(c, width, bias_slab, own):
        k0 = pl.multiple_of(c * ck, ck)
        rows = width * ck
        kc = kk_ref[pl.ds(k0, rows), :]
        key = jnp.concatenate([keys_ref[c + i] for i in range(width)], axis=0)
        idx = k0 + lax.broadcasted_iota(I32, (rows, tq), 0)
        sel = (key > thr) | ((key == thr) & (idx < cut))
        if own:
            sel = sel & _admissible(rows, tq, k0, q0)
        vt = jnp.concatenate([vt_ref[c + i] for i in range(width)], axis=1)

        def use(h, s):
            if bias_slab is not None:
                s = s + bt_ref[bias_slab, h]
            _softmax_step(jnp.where(sel, s, NEG), vt, m_ref, acc_ref, h)

        _scores_first(s_ref.at[:, 0:rows], nh, lambda h: _dot_nt(kc, qs_ref[h * tq:(h + 1) * tq, :]), use)

    widest_first(jnp.maximum(j - 1, 0), lambda c, width: att_chunk(c, width, None, False))

    @pl.when(j > 0)
    def _prev():
        att_chunk(j - 1, 1, 0, False)

    att_chunk(j, 1, 1, True)
    _finish(o_ref, acc_ref, nh, DSA_HEAD_DIM)


def _dsa_attn(qb, qi, wt, kk, vt, ki, bt, tq, topk):
    b, t, _ = qb.shape
    ck = tq
    assert tq == bt.shape[2] and tq % LANES == 0 and t % tq == 0 and t < -_I16_MIN
    nh = DSA_HEADS
    blk = lambda a: pl.BlockSpec((None, tq, a.shape[2]), lambda i, j: (i, j, 0))
    seqb = lambda a: pl.BlockSpec((None, t, a.shape[2]), lambda i, j: (i, 0, 0))
    return pl.pallas_call(
        functools.partial(_dsa_kernel, tq=tq, topk=topk),
        grid=(b, t // tq),
        in_specs=[blk(qb), blk(qi), pl.BlockSpec((nh, tq), lambda i, j: (0, i * (t // tq) + j)), seqb(kk),
                  pl.BlockSpec((None, t // ck, _VROWS, ck), lambda i, j: (i, 0, 0, 0)), seqb(ki),
                  pl.BlockSpec(bt.shape, lambda i, j: (0, 0, 0, 0))],
        out_specs=blk(qb),
        out_shape=jax.ShapeDtypeStruct(qb.shape, BF16),
        scratch_shapes=[pltpu.VMEM((nh * tq, LANES), BF16), pltpu.VMEM((nh * tq, LANES), BF16),
                        pltpu.VMEM((t // ck, ck, tq), I32), pltpu.VMEM((t // ck, ck, tq), I16),
                        pltpu.VMEM((t // ck, ck, tq), I16), pltpu.VMEM((t // ck, ck, tq), I16),
                        pltpu.VMEM((nh, _STEP_WIDTHS[0] * ck, tq), F32),
                        pltpu.VMEM((1, tq), I32),
                        pltpu.VMEM((1, tq), I32), pltpu.VMEM((nh, 1, tq), F32),
                        pltpu.VMEM((nh, _VROWS, tq), F32)],
        compiler_params=pltpu.CompilerParams(dimension_semantics=("parallel", "arbitrary"),
                                             vmem_limit_bytes=VMEM_LIMIT),
        name="dsa_attn",
    )(qb, qi, wt, kk, vt, ki, bt)


_R_GRP = 0
_R_EXP = N_GROUPS


def _post_kernel(x_ref, oa_ref, ob_ref, g0_ref, b0_ref, wg_ref, bg_ref, wua_ref, wub_ref, wo_ref,
                 g1_ref, b1_ref, wr_ref, br_ref, hx_ref, gsel_ref, *, alpha):
    d = x_ref.shape[1]
    xn = _ln(x_ref[...], g0_ref[...], b0_ref[...])
    z = _dot(xn.astype(BF16), wg_ref[...]) + bg_ref[...]
    gates = 1.0 / (1.0 + jnp.exp(-z))
    mix = gates[:, :d] * _dot(oa_ref[...], wua_ref[...]) + gates[:, d:] * _dot(ob_ref[...], wub_ref[...])
    mixed = _dot(mix.astype(BF16), wo_ref[...])
    h = _ln(alpha * xn + mixed, g1_ref[...], b1_ref[...])
    hx_ref[:, :d] = h

    logits = _dot(h.astype(BF16), wr_ref[...]) + br_ref[...]
    lane = lax.broadcasted_iota(I32, logits.shape, 1).astype(F32)
    gmask = lane < N_GROUPS
    gl = jnp.where(gmask, logits, -jnp.inf)
    gmax = jnp.max(gl, axis=1, keepdims=True)
    g_sel = jnp.min(jnp.where(gl == gmax, lane, float(LANES)), axis=1, keepdims=True)
    g_w = 1.0 / jnp.sum(jnp.where(gmask, jnp.exp(gl - gmax), 0.0), axis=1, keepdims=True)
    e_lo = _R_EXP + g_sel * EXPERTS_PER_GROUP
    el = jnp.where((lane >= e_lo) & (lane < e_lo + EXPERTS_PER_GROUP), logits, -jnp.inf)
    m1 = jnp.max(el, axis=1, keepdims=True)
    i1 = jnp.min(jnp.where(el == m1, lane, float(LANES)), axis=1, keepdims=True)
    el2 = jnp.where(lane == i1, -jnp.inf, el)
    m2 = jnp.max(el2, axis=1, keepdims=True)
    i2 = jnp.min(jnp.where(el2 == m2, lane, float(LANES)), axis=1, keepdims=True)
    e21 = jnp.exp(m2 - m1)
    w1 = g_w / (1.0 + e21)
    w2 = g_w * e21 / (1.0 + e21)
    hx_ref[:, d:] = jnp.where(lane == i1, w1, 0.0) + jnp.where(lane == i2, w2, 0.0)
    gsel_ref[...] = g_sel.astype(I32)


def _post_attn(x2, oa, ob, g0, b0, wg, bg, wua, wub, wo, g1, b1, wr, br, alpha, tm):
    n, d = x2.shape
    row = lambda w: pl.BlockSpec((tm, w), lambda i: (i, 0))
    full = lambda a: pl.BlockSpec(a.shape, lambda i: (0,) * a.ndim)
    return pl.pallas_call(
        functools.partial(_post_kernel, alpha=alpha),
        grid=(n // tm,),
        in_specs=[row(d), row(oa.shape[1]), row(ob.shape[1])]
                 + [full(a) for a in (g0, b0, wg, bg, wua, wub, wo, g1, b1, wr, br)],
        out_specs=[row(d + LANES), row(1)],
        out_shape=[jax.ShapeDtypeStruct((n, d + LANES), F32), jax.ShapeDtypeStruct((n, 1), I32)],
        compiler_params=pltpu.CompilerParams(dimension_semantics=("parallel",),
                                             vmem_limit_bytes=VMEM_LIMIT),
        name="post_attn",
    )(x2, oa, ob, g0, b0, wg, bg, wua, wub, wo, g1, b1, wr, br)


def _moe_kernel(tg_ref, nv_ref, src_ref, srcn_ref, hx_hbm, weg_ref, weu_ref, wed_ref, g2_ref, b2_ref,
                out_hbm, gbuf, obuf, gsem, ssem, *, alpha, tm, d):
    k = pl.program_id(0)
    nk = pl.num_programs(0)
    slot = k % 2
    nrun = tm // LANES

    def gather_copy(tok, r, s):
        return pltpu.make_async_copy(hx_hbm.at[pl.ds(tok, 1)], gbuf.at[s, pl.ds(r, 1)], gsem.at[s])

    def scatter_copy(tok, r, s):
        return pltpu.make_async_copy(obuf.at[s, pl.ds(r, 1)], out_hbm.at[pl.ds(tok, 1)], ssem.at[s])

    def start_gather(tile_src_ref, s):
        for g in range(nrun):
            for c in range(LANES):
                gather_copy(tile_src_ref[g, c], g * LANES + c, s).start()

    def start_scatter(n_rows, s):
        for g in range(nrun):
            @pl.when(n_rows >= (g + 1) * LANES)
            def _run():
                for c in range(LANES):
                    scatter_copy(src_ref[g, c], g * LANES + c, s).start()

        def row(r, carry):
            scatter_copy(src_ref[r // LANES, r % LANES], r, s).start()
            return carry

        lax.fori_loop(n_rows // LANES * LANES, n_rows, row, 0)

    def by_slot(fn, s_of_slot):
        for par in (0, 1):
            pl.when(slot == par)(functools.partial(fn, s_of_slot(par)))

    def wait_scatter(n_rows, s):
        def waiter(rows):
            def body(i, carry):
                pltpu.make_async_copy(obuf.at[s, pl.ds(0, rows)], out_hbm.at[pl.ds(0, rows)], ssem.at[s]).wait()
                return carry
            return body

        lax.fori_loop(0, n_rows // LANES, waiter(LANES), 0)
        lax.fori_loop(0, n_rows % LANES // _SUBLANES, waiter(_SUBLANES), 0)
        lax.fori_loop(0, n_rows % _SUBLANES, waiter(1), 0)

    @pl.when(k == 0)
    def _first():
        start_gather(src_ref, 0)

    pltpu.make_async_copy(hx_hbm.at[pl.ds(0, tm)], gbuf.at[slot], gsem.at[slot]).wait()

    @pl.when(k + 1 < nk)
    def _prefetch():
        by_slot(lambda s: start_gather(srcn_ref, s), lambda par: 1 - par)

    @pl.when(k >= 2)
    def _free_obuf():
        wait_scatter(nv_ref[k - 2], slot)

    x = gbuf[slot]
    h = x[:, :d]
    comb = x[:, d:]
    hb = h.astype(BF16)
    lane = lax.broadcasted_iota(I32, comb.shape, 1)
    e0 = _R_EXP + tg_ref[k] * EXPERTS_PER_GROUP
    acc = jnp.zeros((tm, d), F32)
    for e in range(EXPERTS_PER_GROUP):
        a = _dot(hb, weg_ref[e])
        u = _dot(hb, weu_ref[e])
        cw = jnp.sum(jnp.where(lane == e0 + e, comb, 0.0), axis=1, keepdims=True)
        hid = (a / (1.0 + jnp.exp(-a))) * u * cw
        acc = acc + _dot(hid.astype(BF16), wed_ref[e])
    obuf[slot] = _ln(alpha * h + acc, g2_ref[...], b2_ref[...])

    nv = nv_ref[k]
    by_slot(lambda s: start_scatter(nv, s), lambda par: par)

    @pl.when(k == nk - 1)
    def _drain():
        wait_scatter(nv, slot)

        @pl.when(k >= 1)
        def _():
            wait_scatter(nv_ref[k - 1], 1 - slot)


def _moe(hx, tile_group, n_valid, src, weg, weu, wed, g2, b2, alpha, tm, n):
    d = hx.shape[1] - LANES
    nk = src.shape[0]
    ne, _, f = weg.shape
    gsz = EXPERTS_PER_GROUP
    grid_spec = pltpu.PrefetchScalarGridSpec(
        num_scalar_prefetch=2,
        grid=(nk,),
        in_specs=[pl.BlockSpec((None, tm // LANES, LANES), lambda k, tg, nv: (k, 0, 0),
                               memory_space=pltpu.SMEM),
                  pl.BlockSpec((None, tm // LANES, LANES),
                               lambda k, tg, nv: (jnp.minimum(k + 1, nk - 1), 0, 0), memory_space=pltpu.SMEM),
                  pl.BlockSpec(memory_space=pl.ANY),
                  pl.BlockSpec((gsz, d, f), lambda k, tg, nv: (tg[k], 0, 0)),
                  pl.BlockSpec((gsz, d, f), lambda k, tg, nv: (tg[k], 0, 0)),
                  pl.BlockSpec((gsz, f, d), lambda k, tg, nv: (tg[k], 0, 0)),
                  pl.BlockSpec(g2.shape, lambda k, tg, nv: (0, 0)),
                  pl.BlockSpec(b2.shape, lambda k, tg, nv: (0, 0))],
        out_specs=pl.BlockSpec(memory_space=pl.ANY),
        scratch_shapes=[pltpu.VMEM((2, tm, d + LANES), F32), pltpu.VMEM((2, tm, d), F32),
                        pltpu.SemaphoreType.DMA((2,)), pltpu.SemaphoreType.DMA((2,))])
    return pl.pallas_call(
        functools.partial(_moe_kernel, alpha=alpha, tm=tm, d=d),
        grid_spec=grid_spec,
        out_shape=jax.ShapeDtypeStruct((n, d), F32),
        compiler_params=pltpu.CompilerParams(dimension_semantics=("arbitrary",),
                                             vmem_limit_bytes=VMEM_LIMIT),
        name="moe",
    )(tile_group, n_valid, src, src, hx, weg, weu, wed, g2, b2)


def _group_tiles(gsel, tm):
    n = gsel.shape[0]
    n_tiles = n // tm + N_GROUPS
    onehot = (gsel[:, None] == jnp.arange(N_GROUPS, dtype=I32)[None, :]).astype(I32)
    csum = jnp.cumsum(onehot, axis=0)
    rank = jnp.sum(csum * onehot, axis=1) - 1
    counts = csum[-1]
    padded = (counts + tm - 1) // tm * tm
    ends = jnp.cumsum(padded)
    offs = ends - padded
    dest = offs[gsel] + rank
    src = jnp.zeros((n_tiles * tm,), I32).at[dest].set(jnp.arange(n, dtype=I32))
    tile_start = jnp.arange(n_tiles, dtype=I32) * tm
    tile_group = jnp.minimum(jnp.sum((tile_start[:, None] >= ends[None, :]).astype(I32), axis=1), N_GROUPS - 1)
    n_valid = jnp.clip(offs[tile_group] + counts[tile_group] - tile_start, 0, tm)
    return tile_group.astype(I32), n_valid.astype(I32), src.reshape(n_tiles, tm // LANES, LANES)


def _layout_weights(w_in, w_uq, w_uk, w_uv):
    d = w_in.shape[0]
    sizes = (MLA_Q_RANK, MLA_KV_RANK, MLA_ROPE, DSA_HEADS * DSA_HEAD_DIM, DSA_HEAD_DIM, DSA_HEAD_DIM,
             IDX_HEADS * IDX_DIM, IDX_DIM, IDX_HEADS)
    offs = np.cumsum((0,) + sizes)
    c_q, c_kv, k_rope, q_b, k_b, v_b, q_idx, k_idx, w_idx = (
        w_in[:, int(offs[i]):int(offs[i + 1])] for i in range(len(sizes)))
    half = MLA_ROPE // 2
    z = lambda w: jnp.zeros((d, w), w_in.dtype)
    pad = LANES - MLA_NOPE - MLA_ROPE
    grp_a = jnp.concatenate([w_idx, z(MLA_NOPE - IDX_HEADS), k_rope, z(pad)], axis=1)
    grp_b = jnp.concatenate([z(MLA_NOPE), k_rope[:, half:], k_rope[:, :half], z(pad)], axis=1)
    w1 = jnp.concatenate([c_q, c_kv, grp_a, grp_b, q_b * DSA_HEAD_DIM ** -0.5, k_b, k_b, v_b, v_b, q_idx,
                          k_idx, k_idx, k_idx, k_idx], axis=1)
    assert w1.shape[1] == _C_END

    dk = MLA_NOPE + MLA_ROPE
    rq = w_uq.shape[0]
    zq = lambda w: jnp.zeros((rq, w), w_uq.dtype)
    wq_parts, wqr_parts, wk_parts = [], [], []
    for h in range(MLA_HEADS):
        nope = w_uq[:, h * dk:h * dk + MLA_NOPE]
        rope = w_uq[:, h * dk + MLA_NOPE:(h + 1) * dk]
        wq_parts += [nope, rope, zq(pad)]
        wqr_parts += [zq(MLA_NOPE), rope[:, half:], rope[:, :half], zq(pad)]
        wk_parts += [w_uk[:, h * MLA_NOPE:(h + 1) * MLA_NOPE],
                     jnp.zeros((w_uk.shape[0], LANES - MLA_NOPE), w_uk.dtype)]
    wq = jnp.concatenate(wq_parts, axis=1)
    wqr = jnp.concatenate(wqr_parts, axis=1)
    wk = jnp.concatenate(wk_parts, axis=1)

    inv = ROPE_THETA ** (-np.arange(half, dtype=np.float32) / half)
    invf = np.zeros((1, LANES), np.float32)
    invf[0, MLA_NOPE:MLA_NOPE + half] = inv
    invf[0, MLA_NOPE + half:MLA_NOPE + MLA_ROPE] = inv
    sgn = np.zeros((1, LANES), np.float32)
    sgn[0, MLA_NOPE:MLA_NOPE + half] = -1.0
    sgn[0, MLA_NOPE + half:MLA_NOPE + MLA_ROPE] = 1.0
    ang_b = np.arange(CHUNK, dtype=np.float64)[:, None] * invf.astype(np.float64)
    cb, sb = np.cos(ang_b).astype(np.float32), np.sin(ang_b).astype(np.float32)

    kvr = w_uv.shape[0]
    wvt = jnp.concatenate(
        [jnp.concatenate([w_uv[:, h * MLA_V:(h + 1) * MLA_V].T, jnp.zeros((_VROWS - MLA_V, kvr), w_uv.dtype)], axis=0)
         for h in range(MLA_HEADS)], axis=0)
    one = np.zeros((MLA_HEADS * _VROWS, 1), np.float32)
    one[MLA_V::_VROWS, 0] = 1.0
    return (w1.astype(BF16), wq.astype(BF16), wqr.astype(BF16), wk.astype(BF16), wvt.astype(BF16),
            jnp.asarray(one), jnp.asarray(invf), jnp.asarray(sgn), jnp.asarray(cb), jnp.asarray(sb))


def kernel(x, positions, ln0_g, ln0_b, w_in, q_norm_g, w_uq, kv_norm_g, w_uk, w_uv, rel_bias, w_up_a, w_up_b,
           w_gate, b_gate, w_o, ln1_g, ln1_b, w_grp, b_grp, w_rt, b_rt, w_exp_gate, w_exp_up, w_exp_down,
           ln2_g, ln2_b):
    bsz, seq, d = x.shape
    depth = w_in.shape[0]
    assert depth == 1
    alpha = (2.0 * depth) ** 0.25
    n = bsz * seq
    topk = min(IDX_TOPK_MAX, seq // 4)
    r2 = lambda v: v.reshape(1, -1)

    x2 = x.reshape(n, d)
    pos2 = positions.reshape(n, 1)
    w1, wq, wqr, wk, wvt, one, invf, sgn, cb, sb = _layout_weights(w_in[0], w_uq[0], w_uk[0], w_uv[0])

    blk = 256
    bt = _bias_tiles(rel_bias, blk)
    qm, km, vmt, qb, kk, vvt, qi, ki, wt = _in_proj(
        x2, pos2, r2(ln0_g), r2(ln0_b), w1, r2(q_norm_g[0]), wq, wqr, r2(kv_norm_g[0]), wk, wvt, one, invf, sgn,
        cb, sb, tm=512, blk=blk)
    b3 = lambda a: a.reshape(bsz, seq, a.shape[1])
    c4 = lambda a: a.reshape(bsz, seq // blk, a.shape[1], blk)
    o_a = _mla_attn(b3(qm), b3(km), c4(vmt), tq=blk, ck=blk)
    o_b = _dsa_attn(b3(qb), b3(qi), wt, b3(kk), c4(vvt), b3(ki), bt, tq=blk, topk=topk)

    wr = jnp.concatenate([w_grp[0], w_rt[0], jnp.zeros((d, LANES - N_GROUPS - N_EXPERTS), F32)], axis=1)
    br = jnp.concatenate([b_grp[0], b_rt[0], jnp.zeros((LANES - N_GROUPS - N_EXPERTS,), F32)])
    hx, gsel = _post_attn(
        x2, o_a.reshape(n, -1), o_b.reshape(n, -1), r2(ln0_g), r2(ln0_b), w_gate[0].astype(BF16), r2(b_gate[0]),
        w_up_a[0].astype(BF16), w_up_b[0].astype(BF16), w_o[0].astype(BF16), r2(ln1_g[0]), r2(ln1_b[0]),
        wr.astype(BF16), r2(br), alpha, tm=512)
    tm_moe = 512
    tile_group, n_valid, src = _group_tiles(gsel[:, 0], tm_moe)
    out = _moe(hx, tile_group, n_valid, src, w_exp_gate[0].astype(BF16), w_exp_up[0].astype(BF16),
               w_exp_down[0].astype(BF16), r2(ln2_g[0]), r2(ln2_b[0]), alpha, tm_moe, n)
    return out.reshape(bsz, seq, d)
```

```python
import functools
import math

import numpy as np
import jax
import jax.numpy as jnp
from jax import lax
from jax.experimental import pallas as pl
from jax.experimental.pallas import tpu as pltpu

F32 = jnp.float32
BF16 = jnp.bfloat16
I32 = jnp.int32
I16 = jnp.int16

CHUNK = 64
MLA_HEADS = 8
MLA_Q_RANK = 256
MLA_KV_RANK = 128
MLA_NOPE = 64
MLA_ROPE = 32
MLA_V = 64
ROPE_THETA = 10000.0
DSA_HEADS = 8
DSA_HEAD_DIM = 64
IDX_HEADS = 8
IDX_DIM = 32
IDX_TOPK_MAX = 256
REL_BUCKETS = 32
REL_MAX_DIST = 128
N_GROUPS = 4
EXPERTS_PER_GROUP = 8
N_EXPERTS = N_GROUPS * EXPERTS_PER_GROUP
EXPERT_HIDDEN = 256
LN_EPS = 1e-5
RMS_EPS = 1e-6
NEG = -1e30

LANES = 128
_SUBLANES = 8
_VROWS = 80
VMEM_LIMIT = 48 * 1024 * 1024

_KEY_NEG = int(np.array(NEG, np.float32).view(np.int32)) ^ 0x7FFFFFFF
_I16_MIN = -(2 ** 15)
_LOG2E = math.log2(math.e)
_CHUNK_SHIFT = CHUNK.bit_length() - 1
assert 1 << _CHUNK_SHIFT == CHUNK


def _ln(x, g, b):
    mu = jnp.mean(x, axis=-1, keepdims=True)
    xc = x - mu
    var = jnp.mean(xc * xc, axis=-1, keepdims=True)
    return xc * lax.rsqrt(var + LN_EPS) * g + b


def _rms(x, g):
    return x * lax.rsqrt(jnp.mean(x * x, axis=-1, keepdims=True) + RMS_EPS) * g


def _dot(a, b):
    return jnp.dot(a, b, preferred_element_type=F32)


def _dot_nt(a, b):
    return lax.dot_general(a, b, (((1,), (1,)), ((), ())), preferred_element_type=F32)


def _bias_kernel(rb_ref, o_ref):
    nb = REL_BUCKETS // 2
    max_exact = nb // 2
    blk = o_ref.shape[2]
    s = lax.broadcasted_iota(I32, (blk, blk), 0)
    t = lax.broadcasted_iota(I32, (blk, blk), 1)
    for which, off in ((0, -blk), (1, 0)):
        rel = s - t + off
        ret = jnp.where(rel > 0, nb, 0)
        n = jnp.abs(rel)
        large = max_exact + (jnp.log(jnp.maximum(n, 1).astype(F32) / max_exact)
                             / math.log(REL_MAX_DIST / max_exact) * (nb - max_exact)).astype(I32)
        large = jnp.minimum(large, nb - 1)
        bucket = ret + jnp.where(n < max_exact, n, large)
        for h in range(DSA_HEADS):
            acc = jnp.zeros((blk, blk), F32)
            for b in range(REL_BUCKETS):
                acc = jnp.where(bucket == b, rb_ref[b, h], acc)
            o_ref[which, h] = (acc - rb_ref[nb - 1, h]) * _LOG2E


def _bias_tiles(rel_bias, blk):
    assert blk >= REL_MAX_DIST
    return pl.pallas_call(
        _bias_kernel,
        out_shape=jax.ShapeDtypeStruct((2, DSA_HEADS, blk, blk), F32),
        in_specs=[pl.BlockSpec(memory_space=pltpu.SMEM)],
        out_specs=pl.BlockSpec(memory_space=pltpu.VMEM),
        name="bias_tiles",
    )(rel_bias)


_C_Q, _C_KV, _C_RA, _C_RB, _C_QB, _C_KK, _C_VV, _C_QI, _C_KI, _C_END = (
    0, 256, 384, 512, 640, 1152, 1280, 1408, 1664, 1792)


def _value_rows(vt, dv):
    t = vt.shape[1]
    return jnp.concatenate([vt, jnp.ones((1, t), F32), jnp.zeros((_VROWS - dv - 1, t), F32)], axis=0)


def _in_proj_kernel(x_ref, pos_ref, g0_ref, b0_ref, w1_ref, qg_ref, wq_ref, wqr_ref, kg_ref, wk_ref,
                    wvt_ref, one_ref, invf_ref, sgn_ref, cb_ref, sb_ref,
                    qm_ref, km_ref, vmt_ref, qb_ref, kk_ref, vvt_ref, qi_ref, ki_ref, wt_ref, *, blk):
    tm = x_ref.shape[0]
    xn = _ln(x_ref[...], g0_ref[...], b0_ref[...])
    proj = _dot(xn.astype(BF16), w1_ref[...])
    qb_ref[...] = (proj[:, _C_QB:_C_KK] * _LOG2E).astype(BF16)
    kk_ref[...] = proj[:, _C_KK:_C_VV].astype(BF16)
    qi_ref[...] = proj[:, _C_QI:_C_KI].astype(BF16)
    ki_ref[...] = proj[:, _C_KI:_C_END].astype(BF16)
    ga = proj[:, _C_RA:_C_RB]
    gb = proj[:, _C_RB:_C_QB]
    wt_ref[...] = ga.T[0:IDX_HEADS, :] * (IDX_HEADS ** -0.5 * IDX_DIM ** -0.5)
    vvt = _value_rows(proj[:, _C_VV:_C_QI].T[0:DSA_HEAD_DIM, :], DSA_HEAD_DIM).astype(BF16)
    for c in range(tm // blk):
        vvt_ref[c] = vvt[:, c * blk:(c + 1) * blk]

    groups = tm // CHUNK
    a = (pos_ref[0:1, :] >> _CHUNK_SHIFT) + lax.broadcasted_iota(I32, (groups, 1), 0)
    ang = (a * CHUNK).astype(F32) * invf_ref[...]
    ca, sa = jnp.cos(ang), jnp.sin(ang)
    cb, sb = cb_ref[...], sb_ref[...]
    cosv = jnp.concatenate([ca[g:g + 1] * cb - sa[g:g + 1] * sb for g in range(groups)], axis=0)
    sinv = jnp.concatenate([sa[g:g + 1] * cb + ca[g:g + 1] * sb for g in range(groups)], axis=0) * sgn_ref[...]
    lane = lax.broadcasted_iota(I32, cosv.shape, 1)
    rope_lane = (lane >= MLA_NOPE) & (lane < MLA_NOPE + MLA_ROPE)
    kr = jnp.where(rope_lane, ga * cosv + gb * sinv, 0.0)

    scale = (MLA_NOPE + MLA_ROPE) ** -0.5 * _LOG2E
    cos8 = jnp.concatenate([cosv * scale] * MLA_HEADS, axis=1)
    sin8 = jnp.concatenate([sinv * scale] * MLA_HEADS, axis=1)
    cqn = _rms(proj[:, _C_Q:_C_KV], qg_ref[...]).astype(BF16)
    q = _dot(cqn, wq_ref[...]) * cos8 + _dot(cqn, wqr_ref[...]) * sin8
    qm_ref[...] = q.astype(BF16)

    ckn = _rms(proj[:, _C_KV:_C_RA], kg_ref[...]).astype(BF16)
    k = _dot(ckn, wk_ref[...]) + jnp.concatenate([kr] * MLA_HEADS, axis=1)
    km_ref[...] = k.astype(BF16)
    vmt = (_dot_nt(wvt_ref[...], ckn) + one_ref[...]).astype(BF16)
    for c in range(tm // blk):
        vmt_ref[c] = vmt[:, c * blk:(c + 1) * blk]


def _in_proj(x2, pos2, g0, b0, w1, qg, wq, wqr, kg, wk, wvt, one, invf, sgn, cb, sb, tm, blk):
    n, d = x2.shape
    assert tm % blk == 0 and blk % CHUNK == 0
    hm = MLA_HEADS * LANES
    row = lambda w: pl.BlockSpec((tm, w), lambda i: (i, 0))
    full = lambda a: pl.BlockSpec(a.shape, lambda i: (0,) * a.ndim)
    slab = lambda r: pl.BlockSpec((tm // blk, r, blk), lambda i: (i, 0, 0))
    tok = lambda w, dt: (jax.ShapeDtypeStruct((n, w), dt), row(w))
    slb = lambda r: (jax.ShapeDtypeStruct((n // blk, r, blk), BF16), slab(r))
    outs = [tok(hm, BF16), tok(hm, BF16), slb(MLA_HEADS * _VROWS), tok(DSA_HEADS * DSA_HEAD_DIM, BF16),
            tok(LANES, BF16), slb(_VROWS), tok(IDX_HEADS * IDX_DIM, BF16), tok(LANES, BF16),
            (jax.ShapeDtypeStruct((IDX_HEADS, n), F32), pl.BlockSpec((IDX_HEADS, tm), lambda i: (0, i)))]
    consts = (g0, b0, w1, qg, wq, wqr, kg, wk, wvt, one, invf, sgn, cb, sb)
    return pl.pallas_call(
        functools.partial(_in_proj_kernel, blk=blk),
        grid=(n // tm,),
        in_specs=[row(d), row(1)] + [full(a) for a in consts],
        out_specs=[s for _, s in outs],
        out_shape=[s for s, _ in outs],
        compiler_params=pltpu.CompilerParams(dimension_semantics=("parallel",),
                                             vmem_limit_bytes=VMEM_LIMIT),
        name="in_proj",
    )(x2, pos2, *consts)


def _admissible(ck, tq, k0, q0):
    s = k0 + lax.broadcasted_iota(I32, (ck, tq), 0)
    t = q0 + lax.broadcasted_iota(I32, (ck, tq), 1)
    return (s >> _CHUNK_SHIFT) <= (t >> _CHUNK_SHIFT)


def _softmax_step(s, vt, m_ref, acc_ref, h):
    sb = s.astype(BF16)
    m_prev = m_ref[h]
    m_new = jnp.maximum(m_prev, jnp.max(sb, axis=0, keepdims=True).astype(F32))
    alpha = jnp.exp2(m_prev - m_new)
    p = jnp.exp2(sb - m_new.astype(BF16))
    acc_ref[h] = alpha * acc_ref[h] + _dot(vt, p)
    m_ref[h] = m_new


_STEP_WIDTHS = (4, 2, 1)


def _scores_first(s_ref, nh, produce, consume):
    for h in range(nh):
        s_ref[h] = produce(h)
    for h in range(nh):
        consume(h, s_ref[h])


def _finish(o_ref, acc_ref, nh, dv):
    ot = jnp.concatenate([acc_ref[h, :dv, :] / acc_ref[h, dv:dv + 1, :] for h in range(nh)], axis=0)
    o_ref[...] = ot.T.astype(BF16)


def _mla_kernel(q_ref, k_ref, vt_ref, o_ref, s_ref, m_ref, acc_ref, *, tq, ck):
    j = pl.program_id(1)
    q0 = j * tq
    m_ref[...] = jnp.full(m_ref.shape, NEG, F32)
    acc_ref[...] = jnp.zeros(acc_ref.shape, F32)
    nfull = (q0 + CHUNK) // ck

    def chunk(c, width, masked):
        k0 = pl.multiple_of(c * ck, ck)
        rows = width * ck
        adm = _admissible(rows, tq, k0, q0) if masked else None

        def score(h):
            return _dot_nt(k_ref[pl.ds(k0, rows), h * LANES:(h + 1) * LANES], q_ref[:, h * LANES:(h + 1) * LANES])

        def use(h, s):
            if masked:
                s = jnp.where(adm, s, NEG)
            vt = jnp.concatenate([vt_ref[c + i, h * _VROWS:(h + 1) * _VROWS, :] for i in range(width)], axis=1)
            _softmax_step(s, vt, m_ref, acc_ref, h)

        _scores_first(s_ref.at[:, 0:rows], MLA_HEADS, score, use)

    done = 0
    for width in _STEP_WIDTHS:
        steps = (nfull - done) // width

        def body(i, carry, width=width, done=done):
            chunk(done + i * width, width, False)
            return carry

        lax.fori_loop(0, steps, body, 0)
        done = done + steps * width
    chunk(nfull, 1, True)
    _finish(o_ref, acc_ref, MLA_HEADS, MLA_V)


def _mla_attn(qm, km, vmt, tq, ck):
    b, t, hm = qm.shape
    dv = MLA_HEADS * MLA_V
    assert ck % tq == 0 and tq > CHUNK and tq % CHUNK == 0 and t % ck == 0
    return pl.pallas_call(
        functools.partial(_mla_kernel, tq=tq, ck=ck),
        grid=(b, t // tq),
        in_specs=[pl.BlockSpec((None, tq, hm), lambda i, j: (i, j, 0)),
                  pl.BlockSpec((None, t, hm), lambda i, j: (i, 0, 0)),
                  pl.BlockSpec((None, t // ck, MLA_HEADS * _VROWS, ck), lambda i, j: (i, 0, 0, 0))],
        out_specs=pl.BlockSpec((None, tq, dv), lambda i, j: (i, j, 0)),
        out_shape=jax.ShapeDtypeStruct((b, t, dv), BF16),
        scratch_shapes=[pltpu.VMEM((MLA_HEADS, _STEP_WIDTHS[0] * ck, tq), F32),
                        pltpu.VMEM((MLA_HEADS, 1, tq), F32),
                        pltpu.VMEM((MLA_HEADS, _VROWS, tq), F32)],
        compiler_params=pltpu.CompilerParams(dimension_semantics=("parallel", "arbitrary"),
                                             vmem_limit_bytes=VMEM_LIMIT),
        name="mla_attn",
    )(qm, km, vmt)


def _dsa_kernel(qb_ref, qi_ref, wt_ref, kk_ref, vt_ref, ki_ref, bt_ref, o_ref,
                qs_ref, qim_ref, keys_ref, khi_ref, klo_ref, kls_ref, s_ref, thr_ref, cut_ref, m_ref, acc_ref,
                *, tq, topk):
    nh = DSA_HEADS
    ck = tq
    seq = kk_ref.shape[0]
    j = pl.program_id(1)
    q0 = j * tq
    nck = j + 1
    lane = lax.broadcasted_iota(I32, (tq, LANES), 1)

    for h in range(nh):
        pair = qb_ref[:, (h // 2) * LANES:(h // 2 + 1) * LANES].astype(F32)
        in_head = (lane >= (h % 2) * DSA_HEAD_DIM) & (lane < (h % 2 + 1) * DSA_HEAD_DIM)
        qs_ref[h * tq:(h + 1) * tq, :] = jnp.where(in_head, pair, 0.0).astype(BF16)
        grp = qi_ref[:, (h // 4) * LANES:(h // 4 + 1) * LANES].astype(F32)
        in_head = (lane >= (h % 4) * IDX_DIM) & (lane < (h % 4 + 1) * IDX_DIM)
        qim_ref[h * tq:(h + 1) * tq, :] = jnp.where(in_head, grp, 0.0).astype(BF16)

    w = wt_ref[...]

    def widest_first(n_chunks, fn):
        done = 0
        for width in _STEP_WIDTHS:
            steps = (n_chunks - done) // width

            def body(i, carry, width=width, done=done):
                fn(done + i * width, width)
                return carry

            lax.fori_loop(0, steps, body, 0)
            done = done + steps * width

    def idx_chunk(c, width, masked):
        k0 = pl.multiple_of(c * ck, ck)
        rows = width * ck
        kc = ki_ref[pl.ds(k0, rows), :]
        total = [jnp.zeros((rows, tq), F32)]

        def add(h, r):
            total[0] = total[0] + w[h:h + 1, :] * jnp.maximum(r, 0.0)

        _scores_first(s_ref.at[:, 0:rows], nh, lambda h: _dot_nt(kc, qim_ref[h * tq:(h + 1) * tq, :]), add)
        score = total[0]
        if masked:
            score = jnp.where(_admissible(rows, tq, k0, q0), score, NEG)
        score = jnp.where(score == 0.0, 0.0, score)
        bits = lax.bitcast_convert_type(score, I32)
        key = bits ^ ((bits >> 31) & 0x7FFFFFFF)
        for i in range(width):
            ki = key[i * ck:(i + 1) * ck]
            keys_ref[c + i] = ki
            khi_ref[c + i] = (ki >> 16).astype(I16)
            klo_ref[c + i] = ((ki & 0xFFFF) - 0x8000).astype(I16)

    widest_first(j, lambda c, width: idx_chunk(c, width, False))
    idx_chunk(j, 1, True)

    thr_ref[...] = jnp.full(thr_ref.shape, _KEY_NEG + 1, I32)
    cut_ref[...] = jnp.full(cut_ref.shape, seq, I32)

    def count16(ref, cand):
        rows = 2 * _SUBLANES

        def hits(c):
            part = jnp.where(ref[c] >= cand, jnp.int16(1), jnp.int16(0)).reshape(ck // rows, rows, tq)
            while part.shape[0] > 1:
                half = part.shape[0] // 2
                part = part[:half] + part[half:]
            return part[0]

        unroll = 4
        acc = lax.fori_loop(0, nck // unroll,
                            lambda i, a: a + sum(hits(unroll * i + u) for u in range(1, unroll)) + hits(unroll * i),
                            jnp.zeros((rows, tq), I16))
        acc = lax.fori_loop(nck // unroll * unroll, nck, lambda c, a: a + hits(c), acc)
        return jnp.sum(acc.astype(F32), axis=0, keepdims=True)

    def greedy16(ref, need):
        def step(i, carry):
            v, cge, cgt = carry
            cand = v + lax.shift_left(jnp.int32(1), 15 - i)
            cnt = count16(ref, cand.astype(I16))
            ok = cnt >= need
            return jnp.where(ok, cand, v), jnp.where(ok, cnt, cge), jnp.where(ok, cgt, cnt)

        v0 = jnp.full((1, tq), _I16_MIN, I32)
        cge0 = jnp.full((1, tq), 1.0, F32) * (nck * ck).astype(F32)
        return lax.fori_loop(0, 16, step, (v0, cge0, jnp.zeros((1, tq), F32)))

    @pl.when(q0 + tq > topk)
    def _search():
        hi, cge_hi, cgt_hi = greedy16(khi_ref, float(topk))
        hi16 = hi.astype(I16)

        def mask_body(c, carry):
            kls_ref[c] = jnp.where(khi_ref[c] == hi16, klo_ref[c], jnp.int16(_I16_MIN))
            return carry

        lax.fori_loop(0, nck, mask_body, 0)
        lo, cge_lo, _ = greedy16(kls_ref, float(topk) - cgt_hi)
        thr = hi * 65536 + (lo + 0x8000)
        thr_ref[...] = thr
        cge = cgt_hi + jnp.where(lo == _I16_MIN, cge_hi - cgt_hi, cge_lo)
        excess = cge - float(topk)

        @pl.when(jnp.max(excess) > 0.0)
        def _ties():
            nbits = max(1, int(seq - 1).bit_length())
            lo16 = lo.astype(I16)

            def tie_body(c, carry):
                idx = (c * ck + lax.broadcasted_iota(I32, (ck, tq), 0)).astype(I16)
                tied = (khi_ref[c] == hi16) & (klo_ref[c] == lo16)
                kls_ref[c] = jnp.where(tied, idx, jnp.int16(-1))
                return carry

            lax.fori_loop(0, nck, tie_body, 0)

            def tstep(i, cut):
                cand = cut + lax.shift_left(jnp.int32(1), nbits - 1 - i)
                return jnp.where(count16(kls_ref, cand.astype(I16)) >= excess, cand, cut)

            cut = lax.fori_loop(0, nbits, tstep, jnp.zeros((1, tq), I32))
            cut_ref[...] = jnp.where(excess > 0.0, cut, seq)

    m_ref[...] = jnp.full(m_ref.shape, NEG, F32)
    acc_ref[...] = jnp.zeros(acc_ref.shape, F32)
    thr = thr_ref[...]
    cut = cut_ref[...]

    def att_chunk(c, width, bias_slab, own):
        k0 = pl.multiple_of(c * ck, ck)
        rows = width * ck
        kc = kk_ref[pl.ds(k0, rows), :]
        key = jnp.concatenate([keys_ref[c + i] for i in range(width)], axis=0)
        idx = k0 + lax.broadcasted_iota(I32, (rows, tq), 0)
        sel = (key > thr) | ((key == thr) & (idx < cut))
        if own:
            sel = sel & _admissible(rows, tq, k0, q0)
        vt = jnp.concatenate([vt_ref[c + i] for i in range(width)], axis=1)

        def use(h, s):
            if bias_slab is not None:
                s = s + bt_ref[bias_slab, h]
            _softmax_step(jnp.where(sel, s, NEG), vt, m_ref, acc_ref, h)

        _scores_first(s_ref.at[:, 0:rows], nh, lambda h: _dot_nt(kc, qs_ref[h * tq:(h + 1) * tq, :]), use)

    widest_first(jnp.maximum(j - 1, 0), lambda c, width: att_chunk(c, width, None, False))

    @pl.when(j > 0)
    def _prev():
        att_chunk(j - 1, 1, 0, False)

    att_chunk(j, 1, 1, True)
    _finish(o_ref, acc_ref, nh, DSA_HEAD_DIM)


def _dsa_attn(qb, qi, wt, kk, vt, ki, bt, tq, topk):
    b, t, _ = qb.shape
    ck = tq
    assert tq == bt.shape[2] and tq % LANES == 0 and t % tq == 0 and t < -_I16_MIN
    nh = DSA_HEADS
    blk = lambda a: pl.BlockSpec((None, tq, a.shape[2]), lambda i, j: (i, j, 0))
    seqb = lambda a: pl.BlockSpec((None, t, a.shape[2]), lambda i, j: (i, 0, 0))
    return pl.pallas_call(
        functools.partial(_dsa_kernel, tq=tq, topk=topk),
        grid=(b, t // tq),
        in_specs=[blk(qb), blk(qi), pl.BlockSpec((nh, tq), lambda i, j: (0, i * (t // tq) + j)), seqb(kk),
                  pl.BlockSpec((None, t // ck, _VROWS, ck), lambda i, j: (i, 0, 0, 0)), seqb(ki),
                  pl.BlockSpec(bt.shape, lambda i, j: (0, 0, 0, 0))],
        out_specs=blk(qb),
        out_shape=jax.ShapeDtypeStruct(qb.shape, BF16),
        scratch_shapes=[pltpu.VMEM((nh * tq, LANES), BF16), pltpu.VMEM((nh * tq, LANES), BF16),
                        pltpu.VMEM((t // ck, ck, tq), I32), pltpu.VMEM((t // ck, ck, tq), I16),
                        pltpu.VMEM((t // ck, ck, tq), I16), pltpu.VMEM((t // ck, ck, tq), I16),
                        pltpu.VMEM((nh, _STEP_WIDTHS[0] * ck, tq), F32),
                        pltpu.VMEM((1, tq), I32),
                        pltpu.VMEM((1, tq), I32), pltpu.VMEM((nh, 1, tq), F32),
                        pltpu.VMEM((nh, _VROWS, tq), F32)],
        compiler_params=pltpu.CompilerParams(dimension_semantics=("parallel", "arbitrary"),
                                             vmem_limit_bytes=VMEM_LIMIT),
        name="dsa_attn",
    )(qb, qi, wt, kk, vt, ki, bt)


_R_EXP = N_GROUPS


def _post_kernel(x_ref, oa_ref, ob_ref, g0_ref, b0_ref, wg_ref, bg_ref, wua_ref, wub_ref, wo_ref,
                 g1_ref, b1_ref, wr_ref, br_ref, hx_ref, gsel_ref, *, alpha):
    d = x_ref.shape[1]
    xn = _ln(x_ref[...], g0_ref[...], b0_ref[...])
    z = _dot(xn.astype(BF16), wg_ref[...]) + bg_ref[...]
    gates = 1.0 / (1.0 + jnp.exp(-z))
    mix = gates[:, :d] * _dot(oa_ref[...], wua_ref[...]) + gates[:, d:] * _dot(ob_ref[...], wub_ref[...])
    mixed = _dot(mix.astype(BF16), wo_ref[...])
    h = _ln(alpha * xn + mixed, g1_ref[...], b1_ref[...])
    hx_ref[:, :d] = h

    logits = _dot(h.astype(BF16), wr_ref[...]) + br_ref[...]
    lane = lax.broadcasted_iota(I32, logits.shape, 1).astype(F32)
    gmask = lane < N_GROUPS
    gl = jnp.where(gmask, logits, -jnp.inf)
    gmax = jnp.max(gl, axis=1, keepdims=True)
    g_sel = jnp.min(jnp.where(gl == gmax, lane, float(LANES)), axis=1, keepdims=True)
    g_w = 1.0 / jnp.sum(jnp.where(gmask, jnp.exp(gl - gmax), 0.0), axis=1, keepdims=True)
    e_lo = _R_EXP + g_sel * EXPERTS_PER_GROUP
    el = jnp.where((lane >= e_lo) & (lane < e_lo + EXPERTS_PER_GROUP), logits, -jnp.inf)
    m1 = jnp.max(el, axis=1, keepdims=True)
    i1 = jnp.min(jnp.where(el == m1, lane, float(LANES)), axis=1, keepdims=True)
    el2 = jnp.where(lane == i1, -jnp.inf, el)
    m2 = jnp.max(el2, axis=1, keepdims=True)
    i2 = jnp.min(jnp.where(el2 == m2, lane, float(LANES)), axis=1, keepdims=True)
    e21 = jnp.exp(m2 - m1)
    w1 = g_w / (1.0 + e21)
    w2 = g_w * e21 / (1.0 + e21)
    hx_ref[:, d:] = jnp.where(lane == i1, w1, 0.0) + jnp.where(lane == i2, w2, 0.0)
    gsel_ref[...] = g_sel.astype(I32)


def _post_attn(x2, oa, ob, g0, b0, wg, bg, wua, wub, wo, g1, b1, wr, br, alpha, tm):
    n, d = x2.shape
    row = lambda w: pl.BlockSpec((tm, w), lambda i: (i, 0))
    full = lambda a: pl.BlockSpec(a.shape, lambda i: (0,) * a.ndim)
    return pl.pallas_call(
        functools.partial(_post_kernel, alpha=alpha),
        grid=(n // tm,),
        in_specs=[row(d), row(oa.shape[1]), row(ob.shape[1])]
                 + [full(a) for a in (g0, b0, wg, bg, wua, wub, wo, g1, b1, wr, br)],
        out_specs=[row(d + LANES), row(1)],
        out_shape=[jax.ShapeDtypeStruct((n, d + LANES), F32), jax.ShapeDtypeStruct((n, 1), I32)],
        compiler_params=pltpu.CompilerParams(dimension_semantics=("parallel",),
                                             vmem_limit_bytes=VMEM_LIMIT),
        name="post_attn",
    )(x2, oa, ob, g0, b0, wg, bg, wua, wub, wo, g1, b1, wr, br)


def _moe_kernel(tg_ref, nv_ref, src_ref, srcn_ref, hx_hbm, weg_ref, weu_ref, wed_ref, g2_ref, b2_ref,
                out_hbm, gbuf, obuf, gsem, ssem, *, alpha, tm, d):
    k = pl.program_id(0)
    nk = pl.num_programs(0)
    slot = k % 2
    nrun = tm // LANES

    def gather_copy(tok, r, s):
        return pltpu.make_async_copy(hx_hbm.at[pl.ds(tok, 1)], gbuf.at[s, pl.ds(r, 1)], gsem.at[s])

    def scatter_copy(tok, r, s):
        return pltpu.make_async_copy(obuf.at[s, pl.ds(r, 1)], out_hbm.at[pl.ds(tok, 1)], ssem.at[s])

    def start_gather(tile_src_ref, s):
        for g in range(nrun):
            for c in range(LANES):
                gather_copy(tile_src_ref[g, c], g * LANES + c, s).start()

    def start_scatter(n_rows, s):
        for g in range(nrun):
            @pl.when(n_rows >= (g + 1) * LANES)
            def _run():
                for c in range(LANES):
                    scatter_copy(src_ref[g, c], g * LANES + c, s).start()

        def row(r, carry):
            scatter_copy(src_ref[r // LANES, r % LANES], r, s).start()
            return carry

        lax.fori_loop(n_rows // LANES * LANES, n_rows, row, 0)

    def by_slot(fn, s_of_slot):
        for par in (0, 1):
            pl.when(slot == par)(functools.partial(fn, s_of_slot(par)))

    def wait_scatter(n_rows, s):
        def waiter(rows):
            def body(i, carry):
                pltpu.make_async_copy(obuf.at[s, pl.ds(0, rows)], out_hbm.at[pl.ds(0, rows)], ssem.at[s]).wait()
                return carry
            return body

        lax.fori_loop(0, n_rows // LANES, waiter(LANES), 0)
        lax.fori_loop(0, n_rows % LANES // _SUBLANES, waiter(_SUBLANES), 0)
        lax.fori_loop(0, n_rows % _SUBLANES, waiter(1), 0)

    @pl.when(k == 0)
    def _first():
        start_gather(src_ref, 0)

    pltpu.make_async_copy(hx_hbm.at[pl.ds(0, tm)], gbuf.at[slot], gsem.at[slot]).wait()

    @pl.when(k + 1 < nk)
    def _prefetch():
        by_slot(lambda s: start_gather(srcn_ref, s), lambda par: 1 - par)

    @pl.when(k >= 2)
    def _free_obuf():
        wait_scatter(nv_ref[k - 2], slot)

    x = gbuf[slot]
    h = x[:, :d]
    comb = x[:, d:]
    hb = h.astype(BF16)
    lane = lax.broadcasted_iota(I32, comb.shape, 1)
    e0 = _R_EXP + tg_ref[k] * EXPERTS_PER_GROUP
    acc = jnp.zeros((tm, d), F32)
    for e in range(EXPERTS_PER_GROUP):
        a = _dot(hb, weg_ref[e])
        u = _dot(hb, weu_ref[e])
        cw = jnp.sum(jnp.where(lane == e0 + e, comb, 0.0), axis=1, keepdims=True)
        hid = (a / (1.0 + jnp.exp(-a))) * u * cw
        acc = acc + _dot(hid.astype(BF16), wed_ref[e])
    obuf[slot] = _ln(alpha * h + acc, g2_ref[...], b2_ref[...])

    nv = nv_ref[k]
    by_slot(lambda s: start_scatter(nv, s), lambda par: par)

    @pl.when(k == nk - 1)
    def _drain():
        wait_scatter(nv, slot)

        @pl.when(k >= 1)
        def _():
            wait_scatter(nv_ref[k - 1], 1 - slot)


def _moe(hx, tile_group, n_valid, src, weg, weu, wed, g2, b2, alpha, tm, n):
    d = hx.shape[1] - LANES
    nk = src.shape[0]
    ne, _, f = weg.shape
    gsz = EXPERTS_PER_GROUP
    grid_spec = pltpu.PrefetchScalarGridSpec(
        num_scalar_prefetch=2,
        grid=(nk,),
        in_specs=[pl.BlockSpec((None, tm // LANES, LANES), lambda k, tg, nv: (k, 0, 0),
                               memory_space=pltpu.SMEM),
                  pl.BlockSpec((None, tm // LANES, LANES),
                               lambda k, tg, nv: (jnp.minimum(k + 1, nk - 1), 0, 0), memory_space=pltpu.SMEM),
                  pl.BlockSpec(memory_space=pl.ANY),
                  pl.BlockSpec((gsz, d, f), lambda k, tg, nv: (tg[k], 0, 0)),
                  pl.BlockSpec((gsz, d, f), lambda k, tg, nv: (tg[k], 0, 0)),
                  pl.BlockSpec((gsz, f, d), lambda k, tg, nv: (tg[k], 0, 0)),
                  pl.BlockSpec(g2.shape, lambda k, tg, nv: (0, 0)),
                  pl.BlockSpec(b2.shape, lambda k, tg, nv: (0, 0))],
        out_specs=pl.BlockSpec(memory_space=pl.ANY),
        scratch_shapes=[pltpu.VMEM((2, tm, d + LANES), F32), pltpu.VMEM((2, tm, d), F32),
                        pltpu.SemaphoreType.DMA((2,)), pltpu.SemaphoreType.DMA((2,))])
    return pl.pallas_call(
        functools.partial(_moe_kernel, alpha=alpha, tm=tm, d=d),
        grid_spec=grid_spec,
        out_shape=jax.ShapeDtypeStruct((n, d), F32),
        compiler_params=pltpu.CompilerParams(dimension_semantics=("arbitrary",),
                                             vmem_limit_bytes=VMEM_LIMIT),
        name="moe",
    )(tile_group, n_valid, src, src, hx, weg, weu, wed, g2, b2)


def _group_tiles(gsel, tm):
    n = gsel.shape[0]
    n_tiles = n // tm + N_GROUPS
    onehot = (gsel[:, None] == jnp.arange(N_GROUPS, dtype=I32)[None, :]).astype(I32)
    csum = jnp.cumsum(onehot, axis=0)
    rank = jnp.sum(csum * onehot, axis=1) - 1
    counts = csum[-1]
    padded = (counts + tm - 1) // tm * tm
    ends = jnp.cumsum(padded)
    offs = ends - padded
    dest = offs[gsel] + rank
    src = jnp.zeros((n_tiles * tm,), I32).at[dest].set(jnp.arange(n, dtype=I32))
    tile_start = jnp.arange(n_tiles, dtype=I32) * tm
    tile_group = jnp.minimum(jnp.sum((tile_start[:, None] >= ends[None, :]).astype(I32), axis=1), N_GROUPS - 1)
    n_valid = jnp.clip(offs[tile_group] + counts[tile_group] - tile_start, 0, tm)
    return tile_group.astype(I32), n_valid.astype(I32), src.reshape(n_tiles, tm // LANES, LANES)


def _layout_weights(w_in, w_uq, w_uk, w_uv):
    d = w_in.shape[0]
    sizes = (MLA_Q_RANK, MLA_KV_RANK, MLA_ROPE, DSA_HEADS * DSA_HEAD_DIM, DSA_HEAD_DIM, DSA_HEAD_DIM,
             IDX_HEADS * IDX_DIM, IDX_DIM, IDX_HEADS)
    offs = np.cumsum((0,) + sizes)
    c_q, c_kv, k_rope, q_b, k_b, v_b, q_idx, k_idx, w_idx = (
        w_in[:, int(offs[i]):int(offs[i + 1])] for i in range(len(sizes)))
    half = MLA_ROPE // 2
    z = lambda w: jnp.zeros((d, w), w_in.dtype)
    pad = LANES - MLA_NOPE - MLA_ROPE
    grp_a = jnp.concatenate([w_idx, z(MLA_NOPE - IDX_HEADS), k_rope, z(pad)], axis=1)
    grp_b = jnp.concatenate([z(MLA_NOPE), k_rope[:, half:], k_rope[:, :half], z(pad)], axis=1)
    w1 = jnp.concatenate([c_q, c_kv, grp_a, grp_b, q_b * DSA_HEAD_DIM ** -0.5, k_b, k_b, v_b, v_b, q_idx,
                          k_idx, k_idx, k_idx, k_idx], axis=1)
    assert w1.shape[1] == _C_END

    dk = MLA_NOPE + MLA_ROPE
    rq = w_uq.shape[0]
    zq = lambda w: jnp.zeros((rq, w), w_uq.dtype)
    wq_parts, wqr_parts, wk_parts = [], [], []
    for h in range(MLA_HEADS):
        nope = w_uq[:, h * dk:h * dk + MLA_NOPE]
        rope = w_uq[:, h * dk + MLA_NOPE:(h + 1) * dk]
        wq_parts += [nope, rope, zq(pad)]
        wqr_parts += [zq(MLA_NOPE), rope[:, half:], rope[:, :half], zq(pad)]
        wk_parts += [w_uk[:, h * MLA_NOPE:(h + 1) * MLA_NOPE],
                     jnp.zeros((w_uk.shape[0], LANES - MLA_NOPE), w_uk.dtype)]
    wq = jnp.concatenate(wq_parts, axis=1)
    wqr = jnp.concatenate(wqr_parts, axis=1)
    wk = jnp.concatenate(wk_parts, axis=1)

    inv = ROPE_THETA ** (-np.arange(half, dtype=np.float32) / half)
    invf = np.zeros((1, LANES), np.float32)
    invf[0, MLA_NOPE:MLA_NOPE + half] = inv
    invf[0, MLA_NOPE + half:MLA_NOPE + MLA_ROPE] = inv
    sgn = np.zeros((1, LANES), np.float32)
    sgn[0, MLA_NOPE:MLA_NOPE + half] = -1.0
    sgn[0, MLA_NOPE + half:MLA_NOPE + MLA_ROPE] = 1.0
    ang_b = np.arange(CHUNK, dtype=np.float64)[:, None] * invf.astype(np.float64)
    cb, sb = np.cos(ang_b).astype(np.float32), np.sin(ang_b).astype(np.float32)

    kvr = w_uv.shape[0]
    wvt = jnp.concatenate(
        [jnp.concatenate([w_uv[:, h * MLA_V:(h + 1) * MLA_V].T, jnp.zeros((_VROWS - MLA_V, kvr), w_uv.dtype)], axis=0)
         for h in range(MLA_HEADS)], axis=0)
    one = np.zeros((MLA_HEADS * _VROWS, 1), np.float32)
    one[MLA_V::_VROWS, 0] = 1.0
    return (w1.astype(BF16), wq.astype(BF16), wqr.astype(BF16), wk.astype(BF16), wvt.astype(BF16),
            jnp.asarray(one), jnp.asarray(invf), jnp.asarray(sgn), jnp.asarray(cb), jnp.asarray(sb))


def _tiles(n, seq):
    blk, tm_tok, tm_moe = 256, 512, 512
    assert seq % blk == 0 and n % tm_tok == 0 and tm_tok % blk == 0 and n % tm_moe == 0 and tm_moe % LANES == 0
    return blk, tm_tok, tm_moe


def kernel(x, positions, ln0_g, ln0_b, w_in, q_norm_g, w_uq, kv_norm_g, w_uk, w_uv, rel_bias, w_up_a, w_up_b,
           w_gate, b_gate, w_o, ln1_g, ln1_b, w_grp, b_grp, w_rt, b_rt, w_exp_gate, w_exp_up, w_exp_down,
           ln2_g, ln2_b):
    bsz, seq, d = x.shape
    depth = w_in.shape[0]
    assert depth == 1
    alpha = (2.0 * depth) ** 0.25
    n = bsz * seq
    topk = min(IDX_TOPK_MAX, seq // 4)
    r2 = lambda v: v.reshape(1, -1)

    x2 = x.reshape(n, d)
    pos2 = positions.reshape(n, 1)
    w1, wq, wqr, wk, wvt, one, invf, sgn, cb, sb = _layout_weights(w_in[0], w_uq[0], w_uk[0], w_uv[0])

    blk, tm_tok, tm_moe = _tiles(n, seq)
    bt = _bias_tiles(rel_bias, blk)
    qm, km, vmt, qb, kk, vvt, qi, ki, wt = _in_proj(
        x2, pos2, r2(ln0_g), r2(ln0_b), w1, r2(q_norm_g[0]), wq, wqr, r2(kv_norm_g[0]), wk, wvt, one, invf, sgn,
        cb, sb, tm=tm_tok, blk=blk)
    b3 = lambda a: a.reshape(bsz, seq, a.shape[1])
    c4 = lambda a: a.reshape(bsz, seq // blk, a.shape[1], blk)
    o_a = _mla_attn(b3(qm), b3(km), c4(vmt), tq=blk, ck=blk)
    o_b = _dsa_attn(b3(qb), b3(qi), wt, b3(kk), c4(vvt), b3(ki), bt, tq=blk, topk=topk)

    wr = jnp.concatenate([w_grp[0], w_rt[0], jnp.zeros((d, LANES - N_GROUPS - N_EXPERTS), F32)], axis=1)
    br = jnp.concatenate([b_grp[0], b_rt[0], jnp.zeros((LANES - N_GROUPS - N_EXPERTS,), F32)])
    hx, gsel = _post_attn(
        x2, o_a.reshape(n, -1), o_b.reshape(n, -1), r2(ln0_g), r2(ln0_b), w_gate[0].astype(BF16), r2(b_gate[0]),
        w_up_a[0].astype(BF16), w_up_b[0].astype(BF16), w_o[0].astype(BF16), r2(ln1_g[0]), r2(ln1_b[0]),
        wr.astype(BF16), r2(br), alpha, tm=tm_tok)
    tile_group, n_valid, src = _group_tiles(gsel[:, 0], tm_moe)
    out = _moe(hx, tile_group, n_valid, src, w_exp_gate[0].astype(BF16), w_exp_up[0].astype(BF16),
               w_exp_down[0].astype(BF16), r2(ln2_g[0]), r2(ln2_b[0]), alpha, tm_moe, n)
    return out.reshape(bsz, seq, d)
```

```python
import functools
import math

import numpy as np
import jax
import jax.numpy as jnp
from jax import lax
from jax.experimental import pallas as pl
from jax.experimental.pallas import tpu as pltpu

F32 = jnp.float32
BF16 = jnp.bfloat16
I32 = jnp.int32
I16 = jnp.int16

CHUNK = 64
MLA_HEADS = 8
MLA_Q_RANK = 256
MLA_KV_RANK = 128
MLA_NOPE = 64
MLA_ROPE = 32
MLA_V = 64
ROPE_THETA = 10000.0
DSA_HEADS = 8
DSA_HEAD_DIM = 64
IDX_HEADS = 8
IDX_DIM = 32
IDX_TOPK_MAX = 256
REL_BUCKETS = 32
REL_MAX_DIST = 128
N_GROUPS = 4
EXPERTS_PER_GROUP = 8
N_EXPERTS = N_GROUPS * EXPERTS_PER_GROUP
EXPERT_HIDDEN = 256
LN_EPS = 1e-5
RMS_EPS = 1e-6
NEG = -1e30

LANES = 128
_SUBLANES = 8
_VROWS = 80
VMEM_LIMIT = 48 * 1024 * 1024

_KEY_NEG = int(np.array(NEG, np.float32).view(np.int32)) ^ 0x7FFFFFFF
_I16_MIN = -(2 ** 15)
_LOG2E = math.log2(math.e)
_CHUNK_SHIFT = CHUNK.bit_length() - 1
assert 1 << _CHUNK_SHIFT == CHUNK


def _ln(x, g, b):
    mu = jnp.mean(x, axis=-1, keepdims=True)
    xc = x - mu
    var = jnp.mean(xc * xc, axis=-1, keepdims=True)
    return xc * lax.rsqrt(var + LN_EPS) * g + b


def _rms(x, g):
    return x * lax.rsqrt(jnp.mean(x * x, axis=-1, keepdims=True) + RMS_EPS) * g


def _dot(a, b):
    return jnp.dot(a, b, preferred_element_type=F32)


def _dot_nt(a, b):
    return lax.dot_general(a, b, (((1,), (1,)), ((), ())), preferred_element_type=F32)


def _bias_kernel(rb_ref, o_ref):
    nb = REL_BUCKETS // 2
    max_exact = nb // 2
    blk = o_ref.shape[2]
    s = lax.broadcasted_iota(I32, (blk, blk), 0)
    t = lax.broadcasted_iota(I32, (blk, blk), 1)
    for which, off in ((0, -blk), (1, 0)):
        rel = s - t + off
        ret = jnp.where(rel > 0, nb, 0)
        n = jnp.abs(rel)
        large = max_exact + (jnp.log(jnp.maximum(n, 1).astype(F32) / max_exact)
                             / math.log(REL_MAX_DIST / max_exact) * (nb - max_exact)).astype(I32)
        large = jnp.minimum(large, nb - 1)
        bucket = ret + jnp.where(n < max_exact, n, large)
        for h in range(DSA_HEADS):
            acc = jnp.zeros((blk, blk), F32)
            for b in range(REL_BUCKETS):
                acc = jnp.where(bucket == b, rb_ref[b, h], acc)
            o_ref[which, h] = (acc - rb_ref[nb - 1, h]) * _LOG2E


def _bias_tiles(rel_bias, blk):
    assert blk >= REL_MAX_DIST
    return pl.pallas_call(
        _bias_kernel,
        out_shape=jax.ShapeDtypeStruct((2, DSA_HEADS, blk, blk), F32),
        in_specs=[pl.BlockSpec(memory_space=pltpu.SMEM)],
        out_specs=pl.BlockSpec(memory_space=pltpu.VMEM),
        name="bias_tiles",
    )(rel_bias)


_C_Q, _C_KV, _C_RA, _C_RB, _C_QB, _C_KK, _C_VV, _C_QI, _C_KI, _C_END = (
    0, 256, 384, 512, 640, 1152, 1280, 1408, 1664, 1792)


def _value_rows(vt, dv):
    t = vt.shape[1]
    return jnp.concatenate([vt, jnp.ones((1, t), F32), jnp.zeros((_VROWS - dv - 1, t), F32)], axis=0)


def _in_proj_kernel(x_ref, pos_ref, g0_ref, b0_ref, w1_ref, qg_ref, wq_ref, wqr_ref, kg_ref, wk_ref,
                    wvt_ref, one_ref, invf_ref, sgn_ref, cb_ref, sb_ref,
                    qm_ref, km_ref, vmt_ref, qb_ref, kk_ref, vvt_ref, qi_ref, ki_ref, wt_ref, *, blk):
    tm = x_ref.shape[0]
    xn = _ln(x_ref[...], g0_ref[...], b0_ref[...])
    proj = _dot(xn.astype(BF16), w1_ref[...])
    qb_ref[...] = (proj[:, _C_QB:_C_KK] * _LOG2E).astype(BF16)
    kk_ref[...] = proj[:, _C_KK:_C_VV].astype(BF16)
    qi_ref[...] = proj[:, _C_QI:_C_KI].astype(BF16)
    ki_ref[...] = proj[:, _C_KI:_C_END].astype(BF16)
    ga = proj[:, _C_RA:_C_RB]
    gb = proj[:, _C_RB:_C_QB]
    wt_ref[...] = ga.T[0:IDX_HEADS, :] * (IDX_HEADS ** -0.5 * IDX_DIM ** -0.5)
    vvt = _value_rows(proj[:, _C_VV:_C_QI].T[0:DSA_HEAD_DIM, :], DSA_HEAD_DIM).astype(BF16)
    for c in range(tm // blk):
        vvt_ref[c] = vvt[:, c * blk:(c + 1) * blk]

    groups = tm // CHUNK
    a = (pos_ref[0:1, :] >> _CHUNK_SHIFT) + lax.broadcasted_iota(I32, (groups, 1), 0)
    ang = (a * CHUNK).astype(F32) * invf_ref[...]
    ca, sa = jnp.cos(ang), jnp.sin(ang)
    cb, sb = cb_ref[...], sb_ref[...]
    cosv = jnp.concatenate([ca[g:g + 1] * cb - sa[g:g + 1] * sb for g in range(groups)], axis=0)
    sinv = jnp.concatenate([sa[g:g + 1] * cb + ca[g:g + 1] * sb for g in range(groups)], axis=0) * sgn_ref[...]
    lane = lax.broadcasted_iota(I32, cosv.shape, 1)
    rope_lane = (lane >= MLA_NOPE) & (lane < MLA_NOPE + MLA_ROPE)
    kr = jnp.where(rope_lane, ga * cosv + gb * sinv, 0.0)

    scale = (MLA_NOPE + MLA_ROPE) ** -0.5 * _LOG2E
    cos8 = jnp.concatenate([cosv * scale] * MLA_HEADS, axis=1)
    sin8 = jnp.concatenate([sinv * scale] * MLA_HEADS, axis=1)
    cqn = _rms(proj[:, _C_Q:_C_KV], qg_ref[...]).astype(BF16)
    q = _dot(cqn, wq_ref[...]) * cos8 + _dot(cqn, wqr_ref[...]) * sin8
    qm_ref[...] = q.astype(BF16)

    ckn = _rms(proj[:, _C_KV:_C_RA], kg_ref[...]).astype(BF16)
    k = _dot(ckn, wk_ref[...]) + jnp.concatenate([kr] * MLA_HEADS, axis=1)
    km_ref[...] = k.astype(BF16)
    vmt = (_dot_nt(wvt_ref[...], ckn) + one_ref[...]).astype(BF16)
    for c in range(tm // blk):
        vmt_ref[c] = vmt[:, c * blk:(c + 1) * blk]


def _in_proj(x2, pos2, g0, b0, w1, qg, wq, wqr, kg, wk, wvt, one, invf, sgn, cb, sb, tm, blk):
    n, d = x2.shape
    assert tm % blk == 0 and blk % CHUNK == 0
    hm = MLA_HEADS * LANES
    row = lambda w: pl.BlockSpec((tm, w), lambda i: (i, 0))
    full = lambda a: pl.BlockSpec(a.shape, lambda i: (0,) * a.ndim)
    slab = lambda r: pl.BlockSpec((tm // blk, r, blk), lambda i: (i, 0, 0))
    tok = lambda w, dt: (jax.ShapeDtypeStruct((n, w), dt), row(w))
    slb = lambda r: (jax.ShapeDtypeStruct((n // blk, r, blk), BF16), slab(r))
    outs = [tok(hm, BF16), tok(hm, BF16), slb(MLA_HEADS * _VROWS), tok(DSA_HEADS * DSA_HEAD_DIM, BF16),
            tok(LANES, BF16), slb(_VROWS), tok(IDX_HEADS * IDX_DIM, BF16), tok(LANES, BF16),
            (jax.ShapeDtypeStruct((IDX_HEADS, n), F32), pl.BlockSpec((IDX_HEADS, tm), lambda i: (0, i)))]
    consts = (g0, b0, w1, qg, wq, wqr, kg, wk, wvt, one, invf, sgn, cb, sb)
    return pl.pallas_call(
        functools.partial(_in_proj_kernel, blk=blk),
        grid=(n // tm,),
        in_specs=[row(d), row(1)] + [full(a) for a in consts],
        out_specs=[s for _, s in outs],
        out_shape=[s for s, _ in outs],
        compiler_params=pltpu.CompilerParams(dimension_semantics=("parallel",),
                                             vmem_limit_bytes=VMEM_LIMIT),
        name="in_proj",
    )(x2, pos2, *consts)


def _admissible(ck, tq, k0, q0):
    s = k0 + lax.broadcasted_iota(I32, (ck, tq), 0)
    t = q0 + lax.broadcasted_iota(I32, (ck, tq), 1)
    return (s >> _CHUNK_SHIFT) <= (t >> _CHUNK_SHIFT)


def _softmax_step(s, vt, m_ref, acc_ref, h):
    sb = s.astype(BF16)
    m_prev = m_ref[h]
    m_new = jnp.maximum(m_prev, jnp.max(sb, axis=0, keepdims=True).astype(F32))
    alpha = jnp.exp2(m_prev - m_new)
    p = jnp.exp2(sb - m_new.astype(BF16))
    acc_ref[h] = alpha * acc_ref[h] + _dot(vt, p)
    m_ref[h] = m_new


_STEP_WIDTHS = (4, 2, 1)


def _scores_first(s_ref, nh, produce, consume):
    for h in range(nh):
        s_ref[h] = produce(h)
    for h in range(nh):
        consume(h, s_ref[h])


def _finish(o_ref, acc_ref, nh, dv):
    ot = jnp.concatenate([acc_ref[h, :dv, :] / acc_ref[h, dv:dv + 1, :] for h in range(nh)], axis=0)
    o_ref[...] = ot.T.astype(BF16)


def _mla_kernel(q_ref, k_ref, vt_ref, o_ref, s_ref, m_ref, acc_ref, *, tq, ck):
    j = pl.program_id(1)
    q0 = j * tq
    m_ref[...] = jnp.full(m_ref.shape, NEG, F32)
    acc_ref[...] = jnp.zeros(acc_ref.shape, F32)
    nfull = (q0 + CHUNK) // ck

    def chunk(c, width, masked):
        k0 = pl.multiple_of(c * ck, ck)
        rows = width * ck
        adm = _admissible(rows, tq, k0, q0) if masked else None

        def score(h):
            return _dot_nt(k_ref[pl.ds(k0, rows), h * LANES:(h + 1) * LANES], q_ref[:, h * LANES:(h + 1) * LANES])

        def use(h, s):
            if masked:
                s = jnp.where(adm, s, NEG)
            vt = jnp.concatenate([vt_ref[c + i, h * _VROWS:(h + 1) * _VROWS, :] for i in range(width)], axis=1)
            _softmax_step(s, vt, m_ref, acc_ref, h)

        _scores_first(s_ref.at[:, 0:rows], MLA_HEADS, score, use)

    done = 0
    for width in _STEP_WIDTHS:
        steps = (nfull - done) // width

        def body(i, carry, width=width, done=done):
            chunk(done + i * width, width, False)
            return carry

        lax.fori_loop(0, steps, body, 0)
        done = done + steps * width
    chunk(nfull, 1, True)
    _finish(o_ref, acc_ref, MLA_HEADS, MLA_V)


def _mla_attn(qm, km, vmt, tq, ck):
    b, t, hm = qm.shape
    dv = MLA_HEADS * MLA_V
    assert ck % tq == 0 and tq > CHUNK and tq % CHUNK == 0 and t % ck == 0
    return pl.pallas_call(
        functools.partial(_mla_kernel, tq=tq, ck=ck),
        grid=(b, t // tq),
        in_specs=[pl.BlockSpec((None, tq, hm), lambda i, j: (i, j, 0)),
                  pl.BlockSpec((None, t, hm), lambda i, j: (i, 0, 0)),
                  pl.BlockSpec((None, t // ck, MLA_HEADS * _VROWS, ck), lambda i, j: (i, 0, 0, 0))],
        out_specs=pl.BlockSpec((None, tq, dv), lambda i, j: (i, j, 0)),
        out_shape=jax.ShapeDtypeStruct((b, t, dv), BF16),
        scratch_shapes=[pltpu.VMEM((MLA_HEADS, _STEP_WIDTHS[0] * ck, tq), F32),
                        pltpu.VMEM((MLA_HEADS, 1, tq), F32),
                        pltpu.VMEM((MLA_HEADS, _VROWS, tq), F32)],
        compiler_params=pltpu.CompilerParams(dimension_semantics=("parallel", "arbitrary"),
                                             vmem_limit_bytes=VMEM_LIMIT),
        name="mla_attn",
    )(qm, km, vmt)


def _dsa_kernel(qb_ref, qi_ref, wt_ref, kk_ref, vt_ref, ki_ref, bt_ref, o_ref,
                qs_ref, qim_ref, keys_ref, khi_ref, klo_ref, kls_ref, s_ref, thr_ref, cut_ref, m_ref, acc_ref,
                *, tq, topk):
    nh = DSA_HEADS
    ck = tq
    seq = kk_ref.shape[0]
    j = pl.program_id(1)
    q0 = j * tq
    nck = j + 1
    lane = lax.broadcasted_iota(I32, (tq, LANES), 1)

    for h in range(nh):
        pair = qb_ref[:, (h // 2) * LANES:(h // 2 + 1) * LANES].astype(F32)
        in_head = (lane >= (h % 2) * DSA_HEAD_DIM) & (lane < (h % 2 + 1) * DSA_HEAD_DIM)
        qs_ref[h * tq:(h + 1) * tq, :] = jnp.where(in_head, pair, 0.0).astype(BF16)
        grp = qi_ref[:, (h // 4) * LANES:(h // 4 + 1) * LANES].astype(F32)
        in_head = (lane >= (h % 4) * IDX_DIM) & (lane < (h % 4 + 1) * IDX_DIM)
        qim_ref[h * tq:(h + 1) * tq, :] = jnp.where(in_head, grp, 0.0).astype(BF16)

    w = wt_ref[...]

    def widest_first(n_chunks, fn):
        done = 0
        for width in _STEP_WIDTHS:
            steps = (n_chunks - done) // width

            def body(i, carry, width=width, done=done):
                fn(done + i * width, width)
                return carry

            lax.fori_loop(0, steps, body, 0)
            done = done + steps * width

    def idx_chunk(c, width, masked):
        k0 = pl.multiple_of(c * ck, ck)
        rows = width * ck
        kc = ki_ref[pl.ds(k0, rows), :]
        total = [jnp.zeros((rows, tq), F32)]

        def add(h, r):
            total[0] = total[0] + w[h:h + 1, :] * jnp.maximum(r, 0.0)

        _scores_first(s_ref.at[:, 0:rows], nh, lambda h: _dot_nt(kc, qim_ref[h * tq:(h + 1) * tq, :]), add)
        score = total[0]
        if masked:
            score = jnp.where(_admissible(rows, tq, k0, q0), score, NEG)
        score = jnp.where(score == 0.0, 0.0, score)
        bits = lax.bitcast_convert_type(score, I32)
        key = bits ^ ((bits >> 31) & 0x7FFFFFFF)
        for i in range(width):
            ki = key[i * ck:(i + 1) * ck]
            keys_ref[c + i] = ki
            khi_ref[c + i] = (ki >> 16).astype(I16)
            klo_ref[c + i] = ((ki & 0xFFFF) - 0x8000).astype(I16)

    widest_first(j, lambda c, width: idx_chunk(c, width, False))
    idx_chunk(j, 1, True)

    thr_ref[...] = jnp.full(thr_ref.shape, _KEY_NEG + 1, I32)
    cut_ref[...] = jnp.full(cut_ref.shape, seq, I32)

    def count16(ref, cand):
        rows = 2 * _SUBLANES

        def hits(c):
            part = jnp.where(ref[c] >= cand, jnp.int16(1), jnp.int16(0)).reshape(ck // rows, rows, tq)
            while part.shape[0] > 1:
                half = part.shape[0] // 2
                part = part[:half] + part[half:]
            return part[0]

        unroll = 4
        acc = lax.fori_loop(0, nck // unroll,
                            lambda i, a: a + sum(hits(unroll * i + u) for u in range(1, unroll)) + hits(unroll * i),
                            jnp.zeros((rows, tq), I16))
        acc = lax.fori_loop(nck // unroll * unroll, nck, lambda c, a: a + hits(c), acc)
        return jnp.sum(acc.astype(F32), axis=0, keepdims=True)

    def greedy16(ref, need):
        def step(i, carry):
            v, cge, cgt = carry
            cand = v + lax.shift_left(jnp.int32(1), 15 - i)
            cnt = count16(ref, cand.astype(I16))
            ok = cnt >= need
            return jnp.where(ok, cand, v), jnp.where(ok, cnt, cge), jnp.where(ok, cgt, cnt)

        v0 = jnp.full((1, tq), _I16_MIN, I32)
        cge0 = jnp.full((1, tq), 1.0, F32) * (nck * ck).astype(F32)
        return lax.fori_loop(0, 16, step, (v0, cge0, jnp.zeros((1, tq), F32)))

    @pl.when(q0 + tq > topk)
    def _search():
        hi, cge_hi, cgt_hi = greedy16(khi_ref, float(topk))
        hi16 = hi.astype(I16)

        def mask_body(c, carry):
            kls_ref[c] = jnp.where(khi_ref[c] == hi16, klo_ref[c], jnp.int16(_I16_MIN))
            return carry

        lax.fori_loop(0, nck, mask_body, 0)
        lo, cge_lo, _ = greedy16(kls_ref, float(topk) - cgt_hi)
        thr = hi * 65536 + (lo + 0x8000)
        thr_ref[...] = thr
        cge = cgt_hi + jnp.where(lo == _I16_MIN, cge_hi - cgt_hi, cge_lo)
        excess = cge - float(topk)

        @pl.when(jnp.max(excess) > 0.0)
        def _ties():
            nbits = max(1, int(seq - 1).bit_length())
            lo16 = lo.astype(I16)

            def tie_body(c, carry):
                idx = (c * ck + lax.broadcasted_iota(I32, (ck, tq), 0)).astype(I16)
                tied = (khi_ref[c] == hi16) & (klo_ref[c] == lo16)
                kls_ref[c] = jnp.where(tied, idx, jnp.int16(-1))
                return carry

            lax.fori_loop(0, nck, tie_body, 0)

            def tstep(i, cut):
                cand = cut + lax.shift_left(jnp.int32(1), nbits - 1 - i)
                return jnp.where(count16(kls_ref, cand.astype(I16)) >= excess, cand, cut)

            cut = lax.fori_loop(0, nbits, tstep, jnp.zeros((1, tq), I32))
            cut_ref[...] = jnp.where(excess > 0.0, cut, seq)

    m_ref[...] = jnp.full(m_ref.shape, NEG, F32)
    acc_ref[...] = jnp.zeros(acc_ref.shape, F32)
    thr = thr_ref[...]
    cut = cut_ref[...]

    def att_chunk(c, width, bias_slab, own):
        k0 = pl.multiple_of(c * ck, ck)
        rows = width * ck
        kc = kk_ref[pl.ds(k0, rows), :]
        key = jnp.concatenate([keys_ref[c + i] for i in range(width)], axis=0)
        idx = k0 + lax.broadcasted_iota(I32, (rows, tq), 0)
        sel = (key > thr) | ((key == thr) & (idx < cut))
        if own:
            sel = sel & _admissible(rows, tq, k0, q0)
        vt = jnp.concatenate([vt_ref[c + i] for i in range(width)], axis=1)

        def use(h, s):
            if bias_slab is not None:
                s = s + bt_ref[bias_slab, h]
            _softmax_step(jnp.where(sel, s, NEG), vt, m_ref, acc_ref, h)

        _scores_first(s_ref.at[:, 0:rows], nh, lambda h: _dot_nt(kc, qs_ref[h * tq:(h + 1) * tq, :]), use)

    widest_first(jnp.maximum(j - 1, 0), lambda c, width: att_chunk(c, width, None, False))

    @pl.when(j > 0)
    def _prev():
        att_chunk(j - 1, 1, 0, False)

    att_chunk(j, 1, 1, True)
    _finish(o_ref, acc_ref, nh, DSA_HEAD_DIM)


def _dsa_attn(qb, qi, wt, kk, vt, ki, bt, tq, topk):
    b, t, _ = qb.shape
    ck = tq
    assert tq == bt.shape[2] and tq % LANES == 0 and t % tq == 0 and t < -_I16_MIN
    nh = DSA_HEADS
    blk = lambda a: pl.BlockSpec((None, tq, a.shape[2]), lambda i, j: (i, j, 0))
    seqb = lambda a: pl.BlockSpec((None, t, a.shape[2]), lambda i, j: (i, 0, 0))
    return pl.pallas_call(
        functools.partial(_dsa_kernel, tq=tq, topk=topk),
        grid=(b, t // tq),
        in_specs=[blk(qb), blk(qi), pl.BlockSpec((nh, tq), lambda i, j: (0, i * (t // tq) + j)), seqb(kk),
                  pl.BlockSpec((None, t // ck, _VROWS, ck), lambda i, j: (i, 0, 0, 0)), seqb(ki),
                  pl.BlockSpec(bt.shape, lambda i, j: (0, 0, 0, 0))],
        out_specs=blk(qb),
        out_shape=jax.ShapeDtypeStruct(qb.shape, BF16),
        scratch_shapes=[pltpu.VMEM((nh * tq, LANES), BF16), pltpu.VMEM((nh * tq, LANES), BF16),
                        pltpu.VMEM((t // ck, ck, tq), I32), pltpu.VMEM((t // ck, ck, tq), I16),
                        pltpu.VMEM((t // ck, ck, tq), I16), pltpu.VMEM((t // ck, ck, tq), I16),
                        pltpu.VMEM((nh, _STEP_WIDTHS[0] * ck, tq), F32),
                        pltpu.VMEM((1, tq), I32),
                        pltpu.VMEM((1, tq), I32), pltpu.VMEM((nh, 1, tq), F32),
                        pltpu.VMEM((nh, _VROWS, tq), F32)],
        compiler_params=pltpu.CompilerParams(dimension_semantics=("parallel", "arbitrary"),
                                             vmem_limit_bytes=VMEM_LIMIT),
        name="dsa_attn",
    )(qb, qi, wt, kk, vt, ki, bt)


_R_EXP = N_GROUPS


def _post_kernel(x_ref, oa_ref, ob_ref, g0_ref, b0_ref, wg_ref, bg_ref, wua_ref, wub_ref, wo_ref,
                 g1_ref, b1_ref, wr_ref, br_ref, hx_ref, gsel_ref, *, alpha):
    d = x_ref.shape[1]
    xn = _ln(x_ref[...], g0_ref[...], b0_ref[...])
    z = _dot(xn.astype(BF16), wg_ref[...]) + bg_ref[...]
    gates = 1.0 / (1.0 + jnp.exp(-z))
    mix = gates[:, :d] * _dot(oa_ref[...], wua_ref[...]) + gates[:, d:] * _dot(ob_ref[...], wub_ref[...])
    mixed = _dot(mix.astype(BF16), wo_ref[...])
    h = _ln(alpha * xn + mixed, g1_ref[...], b1_ref[...])
    hx_ref[:, :d] = h

    logits = _dot(h.astype(BF16), wr_ref[...]) + br_ref[...]
    lane = lax.broadcasted_iota(I32, logits.shape, 1).astype(F32)
    gmask = lane < N_GROUPS
    gl = jnp.where(gmask, logits, -jnp.inf)
    gmax = jnp.max(gl, axis=1, keepdims=True)
    g_sel = jnp.min(jnp.where(gl == gmax, lane, float(LANES)), axis=1, keepdims=True)
    g_w = 1.0 / jnp.sum(jnp.where(gmask, jnp.exp(gl - gmax), 0.0), axis=1, keepdims=True)
    e_lo = _R_EXP + g_sel * EXPERTS_PER_GROUP
    el = jnp.where((lane >= e_lo) & (lane < e_lo + EXPERTS_PER_GROUP), logits, -jnp.inf)
    m1 = jnp.max(el, axis=1, keepdims=True)
    i1 = jnp.min(jnp.where(el == m1, lane, float(LANES)), axis=1, keepdims=True)
    el2 = jnp.where(lane == i1, -jnp.inf, el)
    m2 = jnp.max(el2, axis=1, keepdims=True)
    i2 = jnp.min(jnp.where(el2 == m2, lane, float(LANES)), axis=1, keepdims=True)
    e21 = jnp.exp(m2 - m1)
    w1 = g_w / (1.0 + e21)
    w2 = g_w * e21 / (1.0 + e21)
    hx_ref[:, d:] = jnp.where(lane == i1, w1, 0.0) + jnp.where(lane == i2, w2, 0.0)
    gsel_ref[...] = g_sel.astype(I32)


def _post_attn(x2, oa, ob, g0, b0, wg, bg, wua, wub, wo, g1, b1, wr, br, alpha, tm):
    n, d = x2.shape
    row = lambda w: pl.BlockSpec((tm, w), lambda i: (i, 0))
    full = lambda a: pl.BlockSpec(a.shape, lambda i: (0,) * a.ndim)
    return pl.pallas_call(
        functools.partial(_post_kernel, alpha=alpha),
        grid=(n // tm,),
        in_specs=[row(d), row(oa.shape[1]), row(ob.shape[1])]
                 + [full(a) for a in (g0, b0, wg, bg, wua, wub, wo, g1, b1, wr, br)],
        out_specs=[row(d + LANES), row(1)],
        out_shape=[jax.ShapeDtypeStruct((n, d + LANES), F32), jax.ShapeDtypeStruct((n, 1), I32)],
        compiler_params=pltpu.CompilerParams(dimension_semantics=("parallel",),
                                             vmem_limit_bytes=VMEM_LIMIT),
        name="post_attn",
    )(x2, oa, ob, g0, b0, wg, bg, wua, wub, wo, g1, b1, wr, br)


def _moe_kernel(tg_ref, nv_ref, src_ref, srcn_ref, hx_hbm, weg_ref, weu_ref, wed_ref, g2_ref, b2_ref,
                out_hbm, gbuf, obuf, gsem, ssem, *, alpha, tm, d):
    k = pl.program_id(0)
    nk = pl.num_programs(0)
    slot = k % 2
    nrun = tm // LANES

    def gather_copy(tok, r, s):
        return pltpu.make_async_copy(hx_hbm.at[pl.ds(tok, 1)], gbuf.at[s, pl.ds(r, 1)], gsem.at[s])

    def scatter_copy(tok, r, s):
        return pltpu.make_async_copy(obuf.at[s, pl.ds(r, 1)], out_hbm.at[pl.ds(tok, 1)], ssem.at[s])

    def start_gather(tile_src_ref, s):
        for g in range(nrun):
            for c in range(LANES):
                gather_copy(tile_src_ref[g, c], g * LANES + c, s).start()

    def start_scatter(n_rows, s):
        for g in range(nrun):
            @pl.when(n_rows >= (g + 1) * LANES)
            def _run():
                for c in range(LANES):
                    scatter_copy(src_ref[g, c], g * LANES + c, s).start()

        def row(r, carry):
            scatter_copy(src_ref[r // LANES, r % LANES], r, s).start()
            return carry

        lax.fori_loop(n_rows // LANES * LANES, n_rows, row, 0)

    def by_slot(fn, s_of_slot):
        for par in (0, 1):
            pl.when(slot == par)(functools.partial(fn, s_of_slot(par)))

    def wait_scatter(n_rows, s):
        def waiter(rows):
            def body(i, carry):
                pltpu.make_async_copy(obuf.at[s, pl.ds(0, rows)], out_hbm.at[pl.ds(0, rows)], ssem.at[s]).wait()
                return carry
            return body

        lax.fori_loop(0, n_rows // LANES, waiter(LANES), 0)
        lax.fori_loop(0, n_rows % LANES // _SUBLANES, waiter(_SUBLANES), 0)
        lax.fori_loop(0, n_rows % _SUBLANES, waiter(1), 0)

    @pl.when(k == 0)
    def _first():
        start_gather(src_ref, 0)

    pltpu.make_async_copy(hx_hbm.at[pl.ds(0, tm)], gbuf.at[slot], gsem.at[slot]).wait()

    @pl.when(k + 1 < nk)
    def _prefetch():
        by_slot(lambda s: start_gather(srcn_ref, s), lambda par: 1 - par)

    @pl.when(k >= 2)
    def _free_obuf():
        wait_scatter(nv_ref[k - 2], slot)

    x = gbuf[slot]
    h = x[:, :d]
    comb = x[:, d:]
    hb = h.astype(BF16)
    lane = lax.broadcasted_iota(I32, comb.shape, 1)
    e0 = _R_EXP + tg_ref[k] * EXPERTS_PER_GROUP
    acc = jnp.zeros((tm, d), F32)
    for e in range(EXPERTS_PER_GROUP):
        a = _dot(hb, weg_ref[e])
        u = _dot(hb, weu_ref[e])
        cw = jnp.sum(jnp.where(lane == e0 + e, comb, 0.0), axis=1, keepdims=True)
        hid = (a / (1.0 + jnp.exp(-a))) * u * cw
        acc = acc + _dot(hid.astype(BF16), wed_ref[e])
    obuf[slot] = _ln(alpha * h + acc, g2_ref[...], b2_ref[...])

    nv = nv_ref[k]
    by_slot(lambda s: start_scatter(nv, s), lambda par: par)

    @pl.when(k == nk - 1)
    def _drain():
        wait_scatter(nv, slot)

        @pl.when(k >= 1)
        def _():
            wait_scatter(nv_ref[k - 1], 1 - slot)


def _moe(hx, tile_group, n_valid, src, weg, weu, wed, g2, b2, alpha, tm, n):
    d = hx.shape[1] - LANES
    nk = src.shape[0]
    ne, _, f = weg.shape
    gsz = EXPERTS_PER_GROUP
    grid_spec = pltpu.PrefetchScalarGridSpec(
        num_scalar_prefetch=2,
        grid=(nk,),
        in_specs=[pl.BlockSpec((None, tm // LANES, LANES), lambda k, tg, nv: (k, 0, 0),
                               memory_space=pltpu.SMEM),
                  pl.BlockSpec((None, tm // LANES, LANES),
                               lambda k, tg, nv: (jnp.minimum(k + 1, nk - 1), 0, 0), memory_space=pltpu.SMEM),
                  pl.BlockSpec(memory_space=pl.ANY),
                  pl.BlockSpec((gsz, d, f), lambda k, tg, nv: (tg[k], 0, 0)),
                  pl.BlockSpec((gsz, d, f), lambda k, tg, nv: (tg[k], 0, 0)),
                  pl.BlockSpec((gsz, f, d), lambda k, tg, nv: (tg[k], 0, 0)),
                  pl.BlockSpec(g2.shape, lambda k, tg, nv: (0, 0)),
                  pl.BlockSpec(b2.shape, lambda k, tg, nv: (0, 0))],
        out_specs=pl.BlockSpec(memory_space=pl.ANY),
        scratch_shapes=[pltpu.VMEM((2, tm, d + LANES), F32), pltpu.VMEM((2, tm, d), F32),
                        pltpu.SemaphoreType.DMA((2,)), pltpu.SemaphoreType.DMA((2,))])
    return pl.pallas_call(
        functools.partial(_moe_kernel, alpha=alpha, tm=tm, d=d),
        grid_spec=grid_spec,
        out_shape=jax.ShapeDtypeStruct((n, d), F32),
        compiler_params=pltpu.CompilerParams(dimension_semantics=("arbitrary",),
                                             vmem_limit_bytes=VMEM_LIMIT),
        name="moe",
    )(tile_group, n_valid, src, src, hx, weg, weu, wed, g2, b2)


def _slot_tokens_kernel(dest_ref, lo_ref, hi_ref, src_ref):
    for g in range(N_GROUPS):
        def pad(p, carry):
            src_ref[p] = 0
            return carry
        lax.fori_loop(lo_ref[g], hi_ref[g], pad, 0)

    def put(i, carry):
        src_ref[dest_ref[i]] = i
        return carry

    lax.fori_loop(0, dest_ref.shape[0], put, 0, unroll=8)


def _slot_tokens(dest, pad_lo, pad_hi, n_slots):
    smem = pl.BlockSpec(memory_space=pltpu.SMEM)
    return pl.pallas_call(
        _slot_tokens_kernel,
        out_shape=jax.ShapeDtypeStruct((n_slots,), I32),
        in_specs=[smem, smem, smem],
        out_specs=smem,
        name="slot_tokens",
    )(dest, pad_lo, pad_hi)


def _group_tiles(gsel, tm):
    n = gsel.shape[0]
    n_tiles = n // tm + N_GROUPS
    onehot = (gsel[:, None] == jnp.arange(N_GROUPS, dtype=I32)[None, :]).astype(I32)
    csum = jnp.cumsum(onehot, axis=0)
    rank = jnp.sum(csum * onehot, axis=1) - 1
    counts = csum[-1]
    padded = (counts + tm - 1) // tm * tm
    ends = jnp.cumsum(padded)
    offs = ends - padded
    dest = offs[gsel] + rank
    pad_lo = offs + counts
    pad_hi = ends.at[N_GROUPS - 1].set(n_tiles * tm)
    src = _slot_tokens(dest.astype(I32), pad_lo.astype(I32), pad_hi.astype(I32), n_tiles * tm)
    tile_start = jnp.arange(n_tiles, dtype=I32) * tm
    tile_group = jnp.minimum(jnp.sum((tile_start[:, None] >= ends[None, :]).astype(I32), axis=1), N_GROUPS - 1)
    n_valid = jnp.clip(offs[tile_group] + counts[tile_group] - tile_start, 0, tm)
    return tile_group.astype(I32), n_valid.astype(I32), src.reshape(n_tiles, tm // LANES, LANES)


def _layout_weights(w_in, w_uq, w_uk, w_uv):
    d = w_in.shape[0]
    sizes = (MLA_Q_RANK, MLA_KV_RANK, MLA_ROPE, DSA_HEADS * DSA_HEAD_DIM, DSA_HEAD_DIM, DSA_HEAD_DIM,
             IDX_HEADS * IDX_DIM, IDX_DIM, IDX_HEADS)
    offs = np.cumsum((0,) + sizes)
    c_q, c_kv, k_rope, q_b, k_b, v_b, q_idx, k_idx, w_idx = (
        w_in[:, int(offs[i]):int(offs[i + 1])] for i in range(len(sizes)))
    half = MLA_ROPE // 2
    z = lambda w: jnp.zeros((d, w), w_in.dtype)
    pad = LANES - MLA_NOPE - MLA_ROPE
    grp_a = jnp.concatenate([w_idx, z(MLA_NOPE - IDX_HEADS), k_rope, z(pad)], axis=1)
    grp_b = jnp.concatenate([z(MLA_NOPE), k_rope[:, half:], k_rope[:, :half], z(pad)], axis=1)
    w1 = jnp.concatenate([c_q, c_kv, grp_a, grp_b, q_b * DSA_HEAD_DIM ** -0.5, k_b, k_b, v_b, v_b, q_idx,
                          k_idx, k_idx, k_idx, k_idx], axis=1)
    assert w1.shape[1] == _C_END

    dk = MLA_NOPE + MLA_ROPE
    rq = w_uq.shape[0]
    zq = lambda w: jnp.zeros((rq, w), w_uq.dtype)
    wq_parts, wqr_parts, wk_parts = [], [], []
    for h in range(MLA_HEADS):
        nope = w_uq[:, h * dk:h * dk + MLA_NOPE]
        rope = w_uq[:, h * dk + MLA_NOPE:(h + 1) * dk]
        wq_parts += [nope, rope, zq(pad)]
        wqr_parts += [zq(MLA_NOPE), rope[:, half:], rope[:, :half], zq(pad)]
        wk_parts += [w_uk[:, h * MLA_NOPE:(h + 1) * MLA_NOPE],
                     jnp.zeros((w_uk.shape[0], LANES - MLA_NOPE), w_uk.dtype)]
    wq = jnp.concatenate(wq_parts, axis=1)
    wqr = jnp.concatenate(wqr_parts, axis=1)
    wk = jnp.concatenate(wk_parts, axis=1)

    inv = ROPE_THETA ** (-np.arange(half, dtype=np.float32) / half)
    invf = np.zeros((1, LANES), np.float32)
    invf[0, MLA_NOPE:MLA_NOPE + half] = inv
    invf[0, MLA_NOPE + half:MLA_NOPE + MLA_ROPE] = inv
    sgn = np.zeros((1, LANES), np.float32)
    sgn[0, MLA_NOPE:MLA_NOPE + half] = -1.0
    sgn[0, MLA_NOPE + half:MLA_NOPE + MLA_ROPE] = 1.0
    ang_b = np.arange(CHUNK, dtype=np.float64)[:, None] * invf.astype(np.float64)
    cb, sb = np.cos(ang_b).astype(np.float32), np.sin(ang_b).astype(np.float32)

    kvr = w_uv.shape[0]
    wvt = jnp.concatenate(
        [jnp.concatenate([w_uv[:, h * MLA_V:(h + 1) * MLA_V].T, jnp.zeros((_VROWS - MLA_V, kvr), w_uv.dtype)], axis=0)
         for h in range(MLA_HEADS)], axis=0)
    one = np.zeros((MLA_HEADS * _VROWS, 1), np.float32)
    one[MLA_V::_VROWS, 0] = 1.0
    return (w1.astype(BF16), wq.astype(BF16), wqr.astype(BF16), wk.astype(BF16), wvt.astype(BF16),
            jnp.asarray(one), jnp.asarray(invf), jnp.asarray(sgn), jnp.asarray(cb), jnp.asarray(sb))


def _tiles(n, seq):
    blk, tm_tok, tm_moe = 256, 512, 512
    assert seq % blk == 0 and n % tm_tok == 0 and tm_tok % blk == 0 and n % tm_moe == 0 and tm_moe % LANES == 0
    return blk, tm_tok, tm_moe


def kernel(x, positions, ln0_g, ln0_b, w_in, q_norm_g, w_uq, kv_norm_g, w_uk, w_uv, rel_bias, w_up_a, w_up_b,
           w_gate, b_gate, w_o, ln1_g, ln1_b, w_grp, b_grp, w_rt, b_rt, w_exp_gate, w_exp_up, w_exp_down,
           ln2_g, ln2_b):
    bsz, seq, d = x.shape
    depth = w_in.shape[0]
    assert depth == 1
    alpha = (2.0 * depth) ** 0.25
    n = bsz * seq
    topk = min(IDX_TOPK_MAX, seq // 4)
    r2 = lambda v: v.reshape(1, -1)

    x2 = x.reshape(n, d)
    pos2 = positions.reshape(n, 1)
    w1, wq, wqr, wk, wvt, one, invf, sgn, cb, sb = _layout_weights(w_in[0], w_uq[0], w_uk[0], w_uv[0])

    blk, tm_tok, tm_moe = _tiles(n, seq)
    bt = _bias_tiles(rel_bias, blk)
    qm, km, vmt, qb, kk, vvt, qi, ki, wt = _in_proj(
        x2, pos2, r2(ln0_g), r2(ln0_b), w1, r2(q_norm_g[0]), wq, wqr, r2(kv_norm_g[0]), wk, wvt, one, invf, sgn,
        cb, sb, tm=tm_tok, blk=blk)
    b3 = lambda a: a.reshape(bsz, seq, a.shape[1])
    c4 = lambda a: a.reshape(bsz, seq // blk, a.shape[1], blk)
    o_a = _mla_attn(b3(qm), b3(km), c4(vmt), tq=blk, ck=blk)
    o_b = _dsa_attn(b3(qb), b3(qi), wt, b3(kk), c4(vvt), b3(ki), bt, tq=blk, topk=topk)

    wr = jnp.concatenate([w_grp[0], w_rt[0], jnp.zeros((d, LANES - N_GROUPS - N_EXPERTS), F32)], axis=1)
    br = jnp.concatenate([b_grp[0], b_rt[0], jnp.zeros((LANES - N_GROUPS - N_EXPERTS,), F32)])
    hx, gsel = _post_attn(
        x2, o_a.reshape(n, -1), o_b.reshape(n, -1), r2(ln0_g), r2(ln0_b), w_gate[0].astype(BF16), r2(b_gate[0]),
        w_up_a[0].astype(BF16), w_up_b[0].astype(BF16), w_o[0].astype(BF16), r2(ln1_g[0]), r2(ln1_b[0]),
        wr.astype(BF16), r2(br), alpha, tm=tm_tok)
    tile_group, n_valid, src = _group_tiles(gsel[:, 0], tm_moe)
    out = _moe(hx, tile_group, n_valid, src, w_exp_gate[0].astype(BF16), w_exp_up[0].astype(BF16),
               w_exp_down[0].astype(BF16), r2(ln2_g[0]), r2(ln2_b[0]), alpha, tm_moe, n)
    return out.reshape(bsz, seq, d)
```

```python
import functools
import math

import numpy as np
import jax
import jax.numpy as jnp
from jax import lax
from jax.experimental import pallas as pl
from jax.experimental.pallas import tpu as pltpu

F32 = jnp.float32
BF16 = jnp.bfloat16
I32 = jnp.int32
I16 = jnp.int16

CHUNK = 64
MLA_HEADS = 8
MLA_Q_RANK = 256
MLA_KV_RANK = 128
MLA_NOPE = 64
MLA_ROPE = 32
MLA_V = 64
ROPE_THETA = 10000.0
DSA_HEADS = 8
DSA_HEAD_DIM = 64
IDX_HEADS = 8
IDX_DIM = 32
IDX_TOPK_MAX = 256
REL_BUCKETS = 32
REL_MAX_DIST = 128
N_GROUPS = 4
EXPERTS_PER_GROUP = 8
N_EXPERTS = N_GROUPS * EXPERTS_PER_GROUP
EXPERT_HIDDEN = 256
LN_EPS = 1e-5
RMS_EPS = 1e-6
NEG = -1e30

LANES = 128
_SUBLANES = 8
_VROWS = 80
VMEM_LIMIT = 48 * 1024 * 1024

_KEY_NEG = int(np.array(NEG, np.float32).view(np.int32)) ^ 0x7FFFFFFF
_I16_MIN = -(2 ** 15)
_LOG2E = math.log2(math.e)
_CHUNK_SHIFT = CHUNK.bit_length() - 1
assert 1 << _CHUNK_SHIFT == CHUNK


def _ln(x, g, b):
    mu = jnp.mean(x, axis=-1, keepdims=True)
    xc = x - mu
    var = jnp.mean(xc * xc, axis=-1, keepdims=True)
    return xc * lax.rsqrt(var + LN_EPS) * g + b


def _rms(x, g):
    return x * lax.rsqrt(jnp.mean(x * x, axis=-1, keepdims=True) + RMS_EPS) * g


def _dot(a, b):
    return jnp.dot(a, b, preferred_element_type=F32)


def _dot_nt(a, b):
    return lax.dot_general(a, b, (((1,), (1,)), ((), ())), preferred_element_type=F32)


def _bias_kernel(rb_ref, o_ref):
    nb = REL_BUCKETS // 2
    max_exact = nb // 2
    blk = o_ref.shape[2]
    s = lax.broadcasted_iota(I32, (blk, blk), 0)
    t = lax.broadcasted_iota(I32, (blk, blk), 1)
    for which, off in ((0, -blk), (1, 0)):
        rel = s - t + off
        ret = jnp.where(rel > 0, nb, 0)
        n = jnp.abs(rel)
        large = max_exact + (jnp.log(jnp.maximum(n, 1).astype(F32) / max_exact)
                             / math.log(REL_MAX_DIST / max_exact) * (nb - max_exact)).astype(I32)
        large = jnp.minimum(large, nb - 1)
        bucket = ret + jnp.where(n < max_exact, n, large)
        for h in range(DSA_HEADS):
            acc = jnp.zeros((blk, blk), F32)
            for b in range(REL_BUCKETS):
                acc = jnp.where(bucket == b, rb_ref[b, h], acc)
            o_ref[which, h] = (acc - rb_ref[nb - 1, h]) * _LOG2E


def _bias_tiles(rel_bias, blk):
    assert blk >= REL_MAX_DIST
    return pl.pallas_call(
        _bias_kernel,
        out_shape=jax.ShapeDtypeStruct((2, DSA_HEADS, blk, blk), F32),
        in_specs=[pl.BlockSpec(memory_space=pltpu.SMEM)],
        out_specs=pl.BlockSpec(memory_space=pltpu.VMEM),
        name="bias_tiles",
    )(rel_bias)


_C_Q, _C_KV, _C_RA, _C_RB, _C_QB, _C_KK, _C_VV, _C_QI, _C_KI, _C_END = (
    0, 256, 384, 512, 640, 1152, 1280, 1408, 1664, 1792)


def _value_rows(vt, dv):
    t = vt.shape[1]
    return jnp.concatenate([vt, jnp.ones((1, t), F32), jnp.zeros((_VROWS - dv - 1, t), F32)], axis=0)


def _in_proj_kernel(x_ref, pos_ref, g0_ref, b0_ref, w1_ref, qg_ref, wq_ref, wqr_ref, kg_ref, wk_ref,
                    wvt_ref, one_ref, invf_ref, sgn_ref, cb_ref, sb_ref,
                    qm_ref, km_ref, vmt_ref, qb_ref, kk_ref, vvt_ref, qi_ref, ki_ref, wt_ref, *, blk):
    tm = x_ref.shape[0]
    xn = _ln(x_ref[...], g0_ref[...], b0_ref[...])
    proj = _dot(xn.astype(BF16), w1_ref[...])
    qb_ref[...] = (proj[:, _C_QB:_C_KK] * _LOG2E).astype(BF16)
    kk_ref[...] = proj[:, _C_KK:_C_VV].astype(BF16)
    qi_ref[...] = proj[:, _C_QI:_C_KI].astype(BF16)
    ki_ref[...] = proj[:, _C_KI:_C_END].astype(BF16)
    ga = proj[:, _C_RA:_C_RB]
    gb = proj[:, _C_RB:_C_QB]
    wt_ref[...] = ga.T[0:IDX_HEADS, :] * (IDX_HEADS ** -0.5 * IDX_DIM ** -0.5)
    vvt = _value_rows(proj[:, _C_VV:_C_QI].T[0:DSA_HEAD_DIM, :], DSA_HEAD_DIM).astype(BF16)
    for c in range(tm // blk):
        vvt_ref[c] = vvt[:, c * blk:(c + 1) * blk]

    groups = tm // CHUNK
    a = (pos_ref[0:1, :] >> _CHUNK_SHIFT) + lax.broadcasted_iota(I32, (groups, 1), 0)
    ang = (a * CHUNK).astype(F32) * invf_ref[...]
    ca, sa = jnp.cos(ang), jnp.sin(ang)
    cb, sb = cb_ref[...], sb_ref[...]
    cosv = jnp.concatenate([ca[g:g + 1] * cb - sa[g:g + 1] * sb for g in range(groups)], axis=0)
    sinv = jnp.concatenate([sa[g:g + 1] * cb + ca[g:g + 1] * sb for g in range(groups)], axis=0) * sgn_ref[...]
    lane = lax.broadcasted_iota(I32, cosv.shape, 1)
    rope_lane = (lane >= MLA_NOPE) & (lane < MLA_NOPE + MLA_ROPE)
    kr = jnp.where(rope_lane, ga * cosv + gb * sinv, 0.0)

    scale = (MLA_NOPE + MLA_ROPE) ** -0.5 * _LOG2E
    cos8 = jnp.concatenate([cosv * scale] * MLA_HEADS, axis=1)
    sin8 = jnp.concatenate([sinv * scale] * MLA_HEADS, axis=1)
    cqn = _rms(proj[:, _C_Q:_C_KV], qg_ref[...]).astype(BF16)
    q = _dot(cqn, wq_ref[...]) * cos8 + _dot(cqn, wqr_ref[...]) * sin8
    qm_ref[...] = q.astype(BF16)

    ckn = _rms(proj[:, _C_KV:_C_RA], kg_ref[...]).astype(BF16)
    k = _dot(ckn, wk_ref[...]) + jnp.concatenate([kr] * MLA_HEADS, axis=1)
    km_ref[...] = k.astype(BF16)
    vmt = (_dot_nt(wvt_ref[...], ckn) + one_ref[...]).astype(BF16)
    for c in range(tm // blk):
        vmt_ref[c] = vmt[:, c * blk:(c + 1) * blk]


def _in_proj(x2, pos2, g0, b0, w1, qg, wq, wqr, kg, wk, wvt, one, invf, sgn, cb, sb, tm, blk):
    n, d = x2.shape
    assert tm % blk == 0 and blk % CHUNK == 0
    hm = MLA_HEADS * LANES
    row = lambda w: pl.BlockSpec((tm, w), lambda i: (i, 0))
    full = lambda a: pl.BlockSpec(a.shape, lambda i: (0,) * a.ndim)
    slab = lambda r: pl.BlockSpec((tm // blk, r, blk), lambda i: (i, 0, 0))
    tok = lambda w, dt: (jax.ShapeDtypeStruct((n, w), dt), row(w))
    slb = lambda r: (jax.ShapeDtypeStruct((n // blk, r, blk), BF16), slab(r))
    outs = [tok(hm, BF16), tok(hm, BF16), slb(MLA_HEADS * _VROWS), tok(DSA_HEADS * DSA_HEAD_DIM, BF16),
            tok(LANES, BF16), slb(_VROWS), tok(IDX_HEADS * IDX_DIM, BF16), tok(LANES, BF16),
            (jax.ShapeDtypeStruct((IDX_HEADS, n), F32), pl.BlockSpec((IDX_HEADS, tm), lambda i: (0, i)))]
    consts = (g0, b0, w1, qg, wq, wqr, kg, wk, wvt, one, invf, sgn, cb, sb)
    return pl.pallas_call(
        functools.partial(_in_proj_kernel, blk=blk),
        grid=(n // tm,),
        in_specs=[row(d), row(1)] + [full(a) for a in consts],
        out_specs=[s for _, s in outs],
        out_shape=[s for s, _ in outs],
        compiler_params=pltpu.CompilerParams(dimension_semantics=("parallel",),
                                             vmem_limit_bytes=VMEM_LIMIT),
        name="in_proj",
    )(x2, pos2, *consts)


def _admissible(ck, tq, k0, q0):
    s = k0 + lax.broadcasted_iota(I32, (ck, tq), 0)
    t = q0 + lax.broadcasted_iota(I32, (ck, tq), 1)
    return (s >> _CHUNK_SHIFT) <= (t >> _CHUNK_SHIFT)


def _softmax_step(s, vt, m_ref, acc_ref, h):
    sb = s.astype(BF16)
    m_prev = m_ref[h]
    m_new = jnp.maximum(m_prev, jnp.max(sb, axis=0, keepdims=True).astype(F32))
    alpha = jnp.exp2(m_prev - m_new)
    p = jnp.exp2(sb - m_new.astype(BF16))
    acc_ref[h] = alpha * acc_ref[h] + _dot(vt, p)
    m_ref[h] = m_new


_STEP_WIDTHS = (4, 2, 1)


def _scores_first(s_ref, nh, produce, consume):
    for h in range(nh):
        s_ref[h] = produce(h)
    for h in range(nh):
        consume(h, s_ref[h])


def _finish(o_ref, acc_ref, nh, dv):
    ot = jnp.concatenate([acc_ref[h, :dv, :] / acc_ref[h, dv:dv + 1, :] for h in range(nh)], axis=0)
    o_ref[...] = ot.T.astype(BF16)


def _mla_kernel(q_ref, k_ref, vt_ref, o_ref, s_ref, m_ref, acc_ref, *, tq, ck):
    j = pl.program_id(1)
    q0 = j * tq
    m_ref[...] = jnp.full(m_ref.shape, NEG, F32)
    acc_ref[...] = jnp.zeros(acc_ref.shape, F32)
    nfull = (q0 + CHUNK) // ck

    def chunk(c, width, masked):
        k0 = pl.multiple_of(c * ck, ck)
        rows = width * ck
        adm = _admissible(rows, tq, k0, q0) if masked else None

        def score(h):
            return _dot_nt(k_ref[pl.ds(k0, rows), h * LANES:(h + 1) * LANES], q_ref[:, h * LANES:(h + 1) * LANES])

        def use(h, s):
            if masked:
                s = jnp.where(adm, s, NEG)
            vt = jnp.concatenate([vt_ref[c + i, h * _VROWS:(h + 1) * _VROWS, :] for i in range(width)], axis=1)
            _softmax_step(s, vt, m_ref, acc_ref, h)

        _scores_first(s_ref.at[:, 0:rows], MLA_HEADS, score, use)

    odd = nfull % 2
    done = 0
    for width in _STEP_WIDTHS:
        steps = (nfull - odd - done) // width

        def body(i, carry, width=width, done=done):
            chunk(done + i * width, width, False)
            return carry

        lax.fori_loop(0, steps, body, 0)
        done = done + steps * width

    @pl.when(odd == 1)
    def _last_two():
        chunk(nfull - 1, 2, True)

    @pl.when(odd == 0)
    def _last_one():
        chunk(nfull, 1, True)
    _finish(o_ref, acc_ref, MLA_HEADS, MLA_V)


def _mla_attn(qm, km, vmt, tq, ck):
    b, t, hm = qm.shape
    dv = MLA_HEADS * MLA_V
    assert ck % tq == 0 and tq > CHUNK and tq % CHUNK == 0 and t % ck == 0
    return pl.pallas_call(
        functools.partial(_mla_kernel, tq=tq, ck=ck),
        grid=(b, t // tq),
        in_specs=[pl.BlockSpec((None, tq, hm), lambda i, j: (i, j, 0)),
                  pl.BlockSpec((None, t, hm), lambda i, j: (i, 0, 0)),
                  pl.BlockSpec((None, t // ck, MLA_HEADS * _VROWS, ck), lambda i, j: (i, 0, 0, 0))],
        out_specs=pl.BlockSpec((None, tq, dv), lambda i, j: (i, j, 0)),
        out_shape=jax.ShapeDtypeStruct((b, t, dv), BF16),
        scratch_shapes=[pltpu.VMEM((MLA_HEADS, _STEP_WIDTHS[0] * ck, tq), F32),
                        pltpu.VMEM((MLA_HEADS, 1, tq), F32),
                        pltpu.VMEM((MLA_HEADS, _VROWS, tq), F32)],
        compiler_params=pltpu.CompilerParams(dimension_semantics=("parallel", "arbitrary"),
                                             vmem_limit_bytes=VMEM_LIMIT),
        name="mla_attn",
    )(qm, km, vmt)


def _dsa_kernel(qb_ref, qi_ref, wt_ref, kk_ref, vt_ref, ki_ref, bt_ref, o_ref,
                qs_ref, qim_ref, keys_ref, khi_ref, klo_ref, kls_ref, s_ref, thr_ref, cut_ref, m_ref, acc_ref,
                *, tq, topk):
    nh = DSA_HEADS
    ck = tq
    seq = kk_ref.shape[0]
    j = pl.program_id(1)
    q0 = j * tq
    nck = j + 1
    lane = lax.broadcasted_iota(I32, (tq, LANES), 1)

    for h in range(nh):
        pair = qb_ref[:, (h // 2) * LANES:(h // 2 + 1) * LANES].astype(F32)
        in_head = (lane >= (h % 2) * DSA_HEAD_DIM) & (lane < (h % 2 + 1) * DSA_HEAD_DIM)
        qs_ref[h * tq:(h + 1) * tq, :] = jnp.where(in_head, pair, 0.0).astype(BF16)
        grp = qi_ref[:, (h // 4) * LANES:(h // 4 + 1) * LANES].astype(F32)
        in_head = (lane >= (h % 4) * IDX_DIM) & (lane < (h % 4 + 1) * IDX_DIM)
        qim_ref[h * tq:(h + 1) * tq, :] = jnp.where(in_head, grp, 0.0).astype(BF16)

    w = wt_ref[...]

    def widest_first(n_chunks, fn):
        done = 0
        for width in _STEP_WIDTHS:
            steps = (n_chunks - done) // width

            def body(i, carry, width=width, done=done):
                fn(done + i * width, width)
                return carry

            lax.fori_loop(0, steps, body, 0)
            done = done + steps * width

    def idx_chunk(c, width, masked):
        k0 = pl.multiple_of(c * ck, ck)
        rows = width * ck
        kc = ki_ref[pl.ds(k0, rows), :]
        total = [jnp.zeros((rows, tq), F32)]

        def add(h, r):
            total[0] = total[0] + w[h:h + 1, :] * jnp.maximum(r, 0.0)

        _scores_first(s_ref.at[:, 0:rows], nh, lambda h: _dot_nt(kc, qim_ref[h * tq:(h + 1) * tq, :]), add)
        score = total[0]
        if masked:
            score = jnp.where(_admissible(rows, tq, k0, q0), score, NEG)
        score = jnp.where(score == 0.0, 0.0, score)
        bits = lax.bitcast_convert_type(score, I32)
        key = bits ^ ((bits >> 31) & 0x7FFFFFFF)
        for i in range(width):
            ki = key[i * ck:(i + 1) * ck]
            keys_ref[c + i] = ki
            khi_ref[c + i] = (ki >> 16).astype(I16)
            klo_ref[c + i] = ((ki & 0xFFFF) - 0x8000).astype(I16)

    widest_first(j, lambda c, width: idx_chunk(c, width, False))
    idx_chunk(j, 1, True)

    thr_ref[...] = jnp.full(thr_ref.shape, _KEY_NEG + 1, I32)
    cut_ref[...] = jnp.full(cut_ref.shape, seq, I32)

    def count16(ref, cand):
        rows = 2 * _SUBLANES

        def hits(c):
            part = jnp.where(ref[c] >= cand, jnp.int16(1), jnp.int16(0)).reshape(ck // rows, rows, tq)
            while part.shape[0] > 1:
                half = part.shape[0] // 2
                part = part[:half] + part[half:]
            return part[0]

        unroll = 4
        acc = lax.fori_loop(0, nck // unroll,
                            lambda i, a: a + sum(hits(unroll * i + u) for u in range(1, unroll)) + hits(unroll * i),
                            jnp.zeros((rows, tq), I16))
        acc = lax.fori_loop(nck // unroll * unroll, nck, lambda c, a: a + hits(c), acc)
        return jnp.sum(acc.astype(F32), axis=0, keepdims=True)

    def greedy16(ref, need):
        def step(i, carry):
            v, cge, cgt = carry
            cand = v + lax.shift_left(jnp.int32(1), 15 - i)
            cnt = count16(ref, cand.astype(I16))
            ok = cnt >= need
            return jnp.where(ok, cand, v), jnp.where(ok, cnt, cge), jnp.where(ok, cgt, cnt)

        v0 = jnp.full((1, tq), _I16_MIN, I32)
        cge0 = jnp.full((1, tq), 1.0, F32) * (nck * ck).astype(F32)
        return lax.fori_loop(0, 16, step, (v0, cge0, jnp.zeros((1, tq), F32)))

    @pl.when(q0 + tq > topk)
    def _search():
        hi, cge_hi, cgt_hi = greedy16(khi_ref, float(topk))
        hi16 = hi.astype(I16)

        def mask_body(c, carry):
            kls_ref[c] = jnp.where(khi_ref[c] == hi16, klo_ref[c], jnp.int16(_I16_MIN))
            return carry

        lax.fori_loop(0, nck, mask_body, 0)
        lo, cge_lo, _ = greedy16(kls_ref, float(topk) - cgt_hi)
        thr = hi * 65536 + (lo + 0x8000)
        thr_ref[...] = thr
        cge = cgt_hi + jnp.where(lo == _I16_MIN, cge_hi - cgt_hi, cge_lo)
        excess = cge - float(topk)

        @pl.when(jnp.max(excess) > 0.0)
        def _ties():
            nbits = max(1, int(seq - 1).bit_length())
            lo16 = lo.astype(I16)

            def tie_body(c, carry):
                idx = (c * ck + lax.broadcasted_iota(I32, (ck, tq), 0)).astype(I16)
                tied = (khi_ref[c] == hi16) & (klo_ref[c] == lo16)
                kls_ref[c] = jnp.where(tied, idx, jnp.int16(-1))
                return carry

            lax.fori_loop(0, nck, tie_body, 0)

            def tstep(i, cut):
                cand = cut + lax.shift_left(jnp.int32(1), nbits - 1 - i)
                return jnp.where(count16(kls_ref, cand.astype(I16)) >= excess, cand, cut)

            cut = lax.fori_loop(0, nbits, tstep, jnp.zeros((1, tq), I32))
            cut_ref[...] = jnp.where(excess > 0.0, cut, seq)

    m_ref[...] = jnp.full(m_ref.shape, NEG, F32)
    acc_ref[...] = jnp.zeros(acc_ref.shape, F32)
    thr = thr_ref[...]
    cut = cut_ref[...]

    def att_chunk(c, width, bias_slab, own):
        k0 = pl.multiple_of(c * ck, ck)
        rows = width * ck
        kc = kk_ref[pl.ds(k0, rows), :]
        key = jnp.concatenate([keys_ref[c + i] for i in range(width)], axis=0)
        idx = k0 + lax.broadcasted_iota(I32, (rows, tq), 0)
        sel = (key > thr) | ((key == thr) & (idx < cut))
        if own:
            sel = sel & _admissible(rows, tq, k0, q0)
        vt = jnp.concatenate([vt_ref[c + i] for i in range(width)], axis=1)

        def use(h, s):
            if bias_slab is not None:
                s = s + jnp.concatenate([bt_ref[b, h] for b in bias_slab], axis=0)
            _softmax_step(jnp.where(sel, s, NEG), vt, m_ref, acc_ref, h)

        _scores_first(s_ref.at[:, 0:rows], nh, lambda h: _dot_nt(kc, qs_ref[h * tq:(h + 1) * tq, :]), use)

    widest_first(jnp.maximum(j - 1, 0), lambda c, width: att_chunk(c, width, None, False))

    @pl.when(j > 0)
    def _near():
        att_chunk(j - 1, 2, (0, 1), True)

    @pl.when(j == 0)
    def _first_block():
        att_chunk(0, 1, (1,), True)

    _finish(o_ref, acc_ref, nh, DSA_HEAD_DIM)


def _dsa_attn(qb, qi, wt, kk, vt, ki, bt, tq, topk):
    b, t, _ = qb.shape
    ck = tq
    assert tq == bt.shape[2] and tq % LANES == 0 and t % tq == 0 and t < -_I16_MIN
    nh = DSA_HEADS
    blk = lambda a: pl.BlockSpec((None, tq, a.shape[2]), lambda i, j: (i, j, 0))
    seqb = lambda a: pl.BlockSpec((None, t, a.shape[2]), lambda i, j: (i, 0, 0))
    return pl.pallas_call(
        functools.partial(_dsa_kernel, tq=tq, topk=topk),
        grid=(b, t // tq),
        in_specs=[blk(qb), blk(qi), pl.BlockSpec((nh, tq), lambda i, j: (0, i * (t // tq) + j)), seqb(kk),
                  pl.BlockSpec((None, t // ck, _VROWS, ck), lambda i, j: (i, 0, 0, 0)), seqb(ki),
                  pl.BlockSpec(bt.shape, lambda i, j: (0, 0, 0, 0))],
        out_specs=blk(qb),
        out_shape=jax.ShapeDtypeStruct(qb.shape, BF16),
        scratch_shapes=[pltpu.VMEM((nh * tq, LANES), BF16), pltpu.VMEM((nh * tq, LANES), BF16),
                        pltpu.VMEM((t // ck, ck, tq), I32), pltpu.VMEM((t // ck, ck, tq), I16),
                        pltpu.VMEM((t // ck, ck, tq), I16), pltpu.VMEM((t // ck, ck, tq), I16),
                        pltpu.VMEM((nh, _STEP_WIDTHS[0] * ck, tq), F32),
                        pltpu.VMEM((1, tq), I32),
                        pltpu.VMEM((1, tq), I32), pltpu.VMEM((nh, 1, tq), F32),
                        pltpu.VMEM((nh, _VROWS, tq), F32)],
        compiler_params=pltpu.CompilerParams(dimension_semantics=("parallel", "arbitrary"),
                                             vmem_limit_bytes=VMEM_LIMIT),
        name="dsa_attn",
    )(qb, qi, wt, kk, vt, ki, bt)


_R_EXP = N_GROUPS


def _post_kernel(x_ref, oa_ref, ob_ref, g0_ref, b0_ref, wg_ref, bg_ref, wua_ref, wub_ref, wo_ref,
                 g1_ref, b1_ref, wr_ref, br_ref, hx_ref, gsel_ref, *, alpha):
    d = x_ref.shape[1]
    xn = _ln(x_ref[...], g0_ref[...], b0_ref[...])
    z = _dot(xn.astype(BF16), wg_ref[...]) + bg_ref[...]
    gates = 1.0 / (1.0 + jnp.exp(-z))
    mix = gates[:, :d] * _dot(oa_ref[...], wua_ref[...]) + gates[:, d:] * _dot(ob_ref[...], wub_ref[...])
    mixed = _dot(mix.astype(BF16), wo_ref[...])
    h = _ln(alpha * xn + mixed, g1_ref[...], b1_ref[...])
    hx_ref[:, :d] = h

    logits = _dot(h.astype(BF16), wr_ref[...]) + br_ref[...]
    lane = lax.broadcasted_iota(I32, logits.shape, 1).astype(F32)
    gmask = lane < N_GROUPS
    gl = jnp.where(gmask, logits, -jnp.inf)
    gmax = jnp.max(gl, axis=1, keepdims=True)
    g_sel = jnp.min(jnp.where(gl == gmax, lane, float(LANES)), axis=1, keepdims=True)
    g_w = 1.0 / jnp.sum(jnp.where(gmask, jnp.exp(gl - gmax), 0.0), axis=1, keepdims=True)
    e_lo = _R_EXP + g_sel * EXPERTS_PER_GROUP
    el = jnp.where((lane >= e_lo) & (lane < e_lo + EXPERTS_PER_GROUP), logits, -jnp.inf)
    m1 = jnp.max(el, axis=1, keepdims=True)
    i1 = jnp.min(jnp.where(el == m1, lane, float(LANES)), axis=1, keepdims=True)
    el2 = jnp.where(lane == i1, -jnp.inf, el)
    m2 = jnp.max(el2, axis=1, keepdims=True)
    i2 = jnp.min(jnp.where(el2 == m2, lane, float(LANES)), axis=1, keepdims=True)
    e21 = jnp.exp(m2 - m1)
    w1 = g_w / (1.0 + e21)
    w2 = g_w * e21 / (1.0 + e21)
    hx_ref[:, d:] = jnp.where(lane == i1, w1, 0.0) + jnp.where(lane == i2, w2, 0.0)
    gsel_ref[...] = g_sel.astype(I32)


def _post_attn(x2, oa, ob, g0, b0, wg, bg, wua, wub, wo, g1, b1, wr, br, alpha, tm):
    n, d = x2.shape
    row = lambda w: pl.BlockSpec((tm, w), lambda i: (i, 0))
    full = lambda a: pl.BlockSpec(a.shape, lambda i: (0,) * a.ndim)
    return pl.pallas_call(
        functools.partial(_post_kernel, alpha=alpha),
        grid=(n // tm,),
        in_specs=[row(d), row(oa.shape[1]), row(ob.shape[1])]
                 + [full(a) for a in (g0, b0, wg, bg, wua, wub, wo, g1, b1, wr, br)],
        out_specs=[row(d + LANES), row(1)],
        out_shape=[jax.ShapeDtypeStruct((n, d + LANES), F32), jax.ShapeDtypeStruct((n, 1), I32)],
        compiler_params=pltpu.CompilerParams(dimension_semantics=("parallel",),
                                             vmem_limit_bytes=VMEM_LIMIT),
        name="post_attn",
    )(x2, oa, ob, g0, b0, wg, bg, wua, wub, wo, g1, b1, wr, br)


def _moe_kernel(tg_ref, nv_ref, src_ref, srcn_ref, hx_hbm, weg_ref, weu_ref, wed_ref, g2_ref, b2_ref,
                out_hbm, gbuf, obuf, gsem, ssem, *, alpha, tm, d):
    k = pl.program_id(0)
    nk = pl.num_programs(0)
    slot = k % 2
    nrun = tm // LANES

    def gather_copy(tok, r, s):
        return pltpu.make_async_copy(hx_hbm.at[pl.ds(tok, 1)], gbuf.at[s, pl.ds(r, 1)], gsem.at[s])

    def scatter_copy(tok, r, s):
        return pltpu.make_async_copy(obuf.at[s, pl.ds(r, 1)], out_hbm.at[pl.ds(tok, 1)], ssem.at[s])

    def start_gather(tile_src_ref, s):
        for g in range(nrun):
            for c in range(LANES):
                gather_copy(tile_src_ref[g, c], g * LANES + c, s).start()

    def start_scatter(n_rows, s):
        for g in range(nrun):
            @pl.when(n_rows >= (g + 1) * LANES)
            def _run():
                for c in range(LANES):
                    scatter_copy(src_ref[g, c], g * LANES + c, s).start()

        def row(r, carry):
            scatter_copy(src_ref[r // LANES, r % LANES], r, s).start()
            return carry

        lax.fori_loop(n_rows // LANES * LANES, n_rows, row, 0)

    def by_slot(fn, s_of_slot):
        for par in (0, 1):
            pl.when(slot == par)(functools.partial(fn, s_of_slot(par)))

    def wait_scatter(n_rows, s):
        def waiter(rows):
            def body(i, carry):
                pltpu.make_async_copy(obuf.at[s, pl.ds(0, rows)], out_hbm.at[pl.ds(0, rows)], ssem.at[s]).wait()
                return carry
            return body

        lax.fori_loop(0, n_rows // LANES, waiter(LANES), 0)
        lax.fori_loop(0, n_rows % LANES // _SUBLANES, waiter(_SUBLANES), 0)
        lax.fori_loop(0, n_rows % _SUBLANES, waiter(1), 0)

    @pl.when(k == 0)
    def _first():
        start_gather(src_ref, 0)

    pltpu.make_async_copy(hx_hbm.at[pl.ds(0, tm)], gbuf.at[slot], gsem.at[slot]).wait()

    @pl.when(k + 1 < nk)
    def _prefetch():
        by_slot(lambda s: start_gather(srcn_ref, s), lambda par: 1 - par)

    @pl.when(k >= 2)
    def _free_obuf():
        wait_scatter(nv_ref[k - 2], slot)

    x = gbuf[slot]
    h = x[:, :d]
    comb = x[:, d:]
    hb = h.astype(BF16)
    lane = lax.broadcasted_iota(I32, comb.shape, 1)
    e0 = _R_EXP + tg_ref[k] * EXPERTS_PER_GROUP
    acc = jnp.zeros((tm, d), F32)
    for e in range(EXPERTS_PER_GROUP):
        a = _dot(hb, weg_ref[e])
        u = _dot(hb, weu_ref[e])
        cw = jnp.sum(jnp.where(lane == e0 + e, comb, 0.0), axis=1, keepdims=True)
        hid = (a / (1.0 + jnp.exp(-a))) * u * cw
        acc = acc + _dot(hid.astype(BF16), wed_ref[e])
    obuf[slot] = _ln(alpha * h + acc, g2_ref[...], b2_ref[...])

    nv = nv_ref[k]
    by_slot(lambda s: start_scatter(nv, s), lambda par: par)

    @pl.when(k == nk - 1)
    def _drain():
        wait_scatter(nv, slot)

        @pl.when(k >= 1)
        def _():
            wait_scatter(nv_ref[k - 1], 1 - slot)


def _moe(hx, tile_group, n_valid, src, weg, weu, wed, g2, b2, alpha, tm, n):
    d = hx.shape[1] - LANES
    nk = src.shape[0]
    ne, _, f = weg.shape
    gsz = EXPERTS_PER_GROUP
    grid_spec = pltpu.PrefetchScalarGridSpec(
        num_scalar_prefetch=2,
        grid=(nk,),
        in_specs=[pl.BlockSpec((None, tm // LANES, LANES), lambda k, tg, nv: (k, 0, 0),
                               memory_space=pltpu.SMEM),
                  pl.BlockSpec((None, tm // LANES, LANES),
                               lambda k, tg, nv: (jnp.minimum(k + 1, nk - 1), 0, 0), memory_space=pltpu.SMEM),
                  pl.BlockSpec(memory_space=pl.ANY),
                  pl.BlockSpec((gsz, d, f), lambda k, tg, nv: (tg[k], 0, 0)),
                  pl.BlockSpec((gsz, d, f), lambda k, tg, nv: (tg[k], 0, 0)),
                  pl.BlockSpec((gsz, f, d), lambda k, tg, nv: (tg[k], 0, 0)),
                  pl.BlockSpec(g2.shape, lambda k, tg, nv: (0, 0)),
                  pl.BlockSpec(b2.shape, lambda k, tg, nv: (0, 0))],
        out_specs=pl.BlockSpec(memory_space=pl.ANY),
        scratch_shapes=[pltpu.VMEM((2, tm, d + LANES), F32), pltpu.VMEM((2, tm, d), F32),
                        pltpu.SemaphoreType.DMA((2,)), pltpu.SemaphoreType.DMA((2,))])
    return pl.pallas_call(
        functools.partial(_moe_kernel, alpha=alpha, tm=tm, d=d),
        grid_spec=grid_spec,
        out_shape=jax.ShapeDtypeStruct((n, d), F32),
        compiler_params=pltpu.CompilerParams(dimension_semantics=("arbitrary",),
                                             vmem_limit_bytes=VMEM_LIMIT),
        name="moe",
    )(tile_group, n_valid, src, src, hx, weg, weu, wed, g2, b2)


def _slot_tokens_kernel(dest_ref, lo_ref, hi_ref, src_ref):
    for g in range(N_GROUPS):
        def pad(p, carry):
            src_ref[p] = 0
            return carry
        lax.fori_loop(lo_ref[g], hi_ref[g], pad, 0)

    def put(i, carry):
        src_ref[dest_ref[i]] = i
        return carry

    lax.fori_loop(0, dest_ref.shape[0], put, 0, unroll=8)


def _slot_tokens(dest, pad_lo, pad_hi, n_slots):
    smem = pl.BlockSpec(memory_space=pltpu.SMEM)
    return pl.pallas_call(
        _slot_tokens_kernel,
        out_shape=jax.ShapeDtypeStruct((n_slots,), I32),
        in_specs=[smem, smem, smem],
        out_specs=smem,
        name="slot_tokens",
    )(dest, pad_lo, pad_hi)


def _group_tiles(gsel, tm):
    n = gsel.shape[0]
    n_tiles = n // tm + N_GROUPS
    onehot = (gsel[:, None] == jnp.arange(N_GROUPS, dtype=I32)[None, :]).astype(I32)
    csum = jnp.cumsum(onehot, axis=0)
    rank = jnp.sum(csum * onehot, axis=1) - 1
    counts = csum[-1]
    padded = (counts + tm - 1) // tm * tm
    ends = jnp.cumsum(padded)
    offs = ends - padded
    dest = offs[gsel] + rank
    pad_lo = offs + counts
    pad_hi = ends.at[N_GROUPS - 1].set(n_tiles * tm)
    src = _slot_tokens(dest.astype(I32), pad_lo.astype(I32), pad_hi.astype(I32), n_tiles * tm)
    tile_start = jnp.arange(n_tiles, dtype=I32) * tm
    tile_group = jnp.minimum(jnp.sum((tile_start[:, None] >= ends[None, :]).astype(I32), axis=1), N_GROUPS - 1)
    n_valid = jnp.clip(offs[tile_group] + counts[tile_group] - tile_start, 0, tm)
    return tile_group.astype(I32), n_valid.astype(I32), src.reshape(n_tiles, tm // LANES, LANES)


def _layout_weights(w_in, w_uq, w_uk, w_uv):
    d = w_in.shape[0]
    sizes = (MLA_Q_RANK, MLA_KV_RANK, MLA_ROPE, DSA_HEADS * DSA_HEAD_DIM, DSA_HEAD_DIM, DSA_HEAD_DIM,
             IDX_HEADS * IDX_DIM, IDX_DIM, IDX_HEADS)
    offs = np.cumsum((0,) + sizes)
    c_q, c_kv, k_rope, q_b, k_b, v_b, q_idx, k_idx, w_idx = (
        w_in[:, int(offs[i]):int(offs[i + 1])] for i in range(len(sizes)))
    half = MLA_ROPE // 2
    z = lambda w: jnp.zeros((d, w), w_in.dtype)
    pad = LANES - MLA_NOPE - MLA_ROPE
    grp_a = jnp.concatenate([w_idx, z(MLA_NOPE - IDX_HEADS), k_rope, z(pad)], axis=1)
    grp_b = jnp.concatenate([z(MLA_NOPE), k_rope[:, half:], k_rope[:, :half], z(pad)], axis=1)
    w1 = jnp.concatenate([c_q, c_kv, grp_a, grp_b, q_b * DSA_HEAD_DIM ** -0.5, k_b, k_b, v_b, v_b, q_idx,
                          k_idx, k_idx, k_idx, k_idx], axis=1)
    assert w1.shape[1] == _C_END

    dk = MLA_NOPE + MLA_ROPE
    rq = w_uq.shape[0]
    zq = lambda w: jnp.zeros((rq, w), w_uq.dtype)
    wq_parts, wqr_parts, wk_parts = [], [], []
    for h in range(MLA_HEADS):
        nope = w_uq[:, h * dk:h * dk + MLA_NOPE]
        rope = w_uq[:, h * dk + MLA_NOPE:(h + 1) * dk]
        wq_parts += [nope, rope, zq(pad)]
        wqr_parts += [zq(MLA_NOPE), rope[:, half:], rope[:, :half], zq(pad)]
        wk_parts += [w_uk[:, h * MLA_NOPE:(h + 1) * MLA_NOPE],
                     jnp.zeros((w_uk.shape[0], LANES - MLA_NOPE), w_uk.dtype)]
    wq = jnp.concatenate(wq_parts, axis=1)
    wqr = jnp.concatenate(wqr_parts, axis=1)
    wk = jnp.concatenate(wk_parts, axis=1)

    inv = ROPE_THETA ** (-np.arange(half, dtype=np.float32) / half)
    invf = np.zeros((1, LANES), np.float32)
    invf[0, MLA_NOPE:MLA_NOPE + half] = inv
    invf[0, MLA_NOPE + half:MLA_NOPE + MLA_ROPE] = inv
    sgn = np.zeros((1, LANES), np.float32)
    sgn[0, MLA_NOPE:MLA_NOPE + half] = -1.0
    sgn[0, MLA_NOPE + half:MLA_NOPE + MLA_ROPE] = 1.0
    ang_b = np.arange(CHUNK, dtype=np.float64)[:, None] * invf.astype(np.float64)
    cb, sb = np.cos(ang_b).astype(np.float32), np.sin(ang_b).astype(np.float32)

    kvr = w_uv.shape[0]
    wvt = jnp.concatenate(
        [jnp.concatenate([w_uv[:, h * MLA_V:(h + 1) * MLA_V].T, jnp.zeros((_VROWS - MLA_V, kvr), w_uv.dtype)], axis=0)
         for h in range(MLA_HEADS)], axis=0)
    one = np.zeros((MLA_HEADS * _VROWS, 1), np.float32)
    one[MLA_V::_VROWS, 0] = 1.0
    return (w1.astype(BF16), wq.astype(BF16), wqr.astype(BF16), wk.astype(BF16), wvt.astype(BF16),
            jnp.asarray(one), jnp.asarray(invf), jnp.asarray(sgn), jnp.asarray(cb), jnp.asarray(sb))


def _tiles(n, seq):
    blk, tm_tok, tm_moe = 256, 512, 512
    assert seq % blk == 0 and n % tm_tok == 0 and tm_tok % blk == 0 and n % tm_moe == 0 and tm_moe % LANES == 0
    return blk, tm_tok, tm_moe


def kernel(x, positions, ln0_g, ln0_b, w_in, q_norm_g, w_uq, kv_norm_g, w_uk, w_uv, rel_bias, w_up_a, w_up_b,
           w_gate, b_gate, w_o, ln1_g, ln1_b, w_grp, b_grp, w_rt, b_rt, w_exp_gate, w_exp_up, w_exp_down,
           ln2_g, ln2_b):
    bsz, seq, d = x.shape
    depth = w_in.shape[0]
    assert depth == 1
    alpha = (2.0 * depth) ** 0.25
    n = bsz * seq
    topk = min(IDX_TOPK_MAX, seq // 4)
    r2 = lambda v: v.reshape(1, -1)

    x2 = x.reshape(n, d)
    pos2 = positions.reshape(n, 1)
    w1, wq, wqr, wk, wvt, one, invf, sgn, cb, sb = _layout_weights(w_in[0], w_uq[0], w_uk[0], w_uv[0])

    blk, tm_tok, tm_moe = _tiles(n, seq)
    bt = _bias_tiles(rel_bias, blk)
    qm, km, vmt, qb, kk, vvt, qi, ki, wt = _in_proj(
        x2, pos2, r2(ln0_g), r2(ln0_b), w1, r2(q_norm_g[0]), wq, wqr, r2(kv_norm_g[0]), wk, wvt, one, invf, sgn,
        cb, sb, tm=tm_tok, blk=blk)
    b3 = lambda a: a.reshape(bsz, seq, a.shape[1])
    c4 = lambda a: a.reshape(bsz, seq // blk, a.shape[1], blk)
    o_a = _mla_attn(b3(qm), b3(km), c4(vmt), tq=blk, ck=blk)
    o_b = _dsa_attn(b3(qb), b3(qi), wt, b3(kk), c4(vvt), b3(ki), bt, tq=blk, topk=topk)

    wr = jnp.concatenate([w_grp[0], w_rt[0], jnp.zeros((d, LANES - N_GROUPS - N_EXPERTS), F32)], axis=1)
    br = jnp.concatenate([b_grp[0], b_rt[0], jnp.zeros((LANES - N_GROUPS - N_EXPERTS,), F32)])
    hx, gsel = _post_attn(
        x2, o_a.reshape(n, -1), o_b.reshape(n, -1), r2(ln0_g), r2(ln0_b), w_gate[0].astype(BF16), r2(b_gate[0]),
        w_up_a[0].astype(BF16), w_up_b[0].astype(BF16), w_o[0].astype(BF16), r2(ln1_g[0]), r2(ln1_b[0]),
        wr.astype(BF16), r2(br), alpha, tm=tm_tok)
    tile_group, n_valid, src = _group_tiles(gsel[:, 0], tm_moe)
    out = _moe(hx, tile_group, n_valid, src, w_exp_gate[0].astype(BF16), w_exp_up[0].astype(BF16),
               w_exp_down[0].astype(BF16), r2(ln2_g[0]), r2(ln2_b[0]), alpha, tm_moe, n)
    return out.reshape(bsz, seq, d)
```

```python
import functools
import math

import numpy as np
import jax
import jax.numpy as jnp
from jax import lax
from jax.experimental import pallas as pl
from jax.experimental.pallas import tpu as pltpu

F32 = jnp.float32
BF16 = jnp.bfloat16
I32 = jnp.int32
I16 = jnp.int16

CHUNK = 64
MLA_HEADS = 8
MLA_Q_RANK = 256
MLA_KV_RANK = 128
MLA_NOPE = 64
MLA_ROPE = 32
MLA_V = 64
ROPE_THETA = 10000.0
DSA_HEADS = 8
DSA_HEAD_DIM = 64
IDX_HEADS = 8
IDX_DIM = 32
IDX_TOPK_MAX = 256
REL_BUCKETS = 32
REL_MAX_DIST = 128
N_GROUPS = 4
EXPERTS_PER_GROUP = 8
N_EXPERTS = N_GROUPS * EXPERTS_PER_GROUP
EXPERT_HIDDEN = 256
LN_EPS = 1e-5
RMS_EPS = 1e-6
NEG = -1e30

LANES = 128
_SUBLANES = 8
_VROWS = 80
VMEM_LIMIT = 48 * 1024 * 1024

_KEY_NEG = int(np.array(NEG, np.float32).view(np.int32)) ^ 0x7FFFFFFF
_I16_MIN = -(2 ** 15)
_LOG2E = math.log2(math.e)
_CHUNK_SHIFT = CHUNK.bit_length() - 1
assert 1 << _CHUNK_SHIFT == CHUNK


def _ln(x, g, b):
    mu = jnp.mean(x, axis=-1, keepdims=True)
    xc = x - mu
    var = jnp.mean(xc * xc, axis=-1, keepdims=True)
    return xc * lax.rsqrt(var + LN_EPS) * g + b


def _rms(x, g):
    return x * lax.rsqrt(jnp.mean(x * x, axis=-1, keepdims=True) + RMS_EPS) * g


def _dot(a, b):
    return jnp.dot(a, b, preferred_element_type=F32)


def _dot_nt(a, b):
    return lax.dot_general(a, b, (((1,), (1,)), ((), ())), preferred_element_type=F32)


def _bias_kernel(rb_ref, o_ref):
    nb = REL_BUCKETS // 2
    max_exact = nb // 2
    blk = o_ref.shape[2]
    s = lax.broadcasted_iota(I32, (blk, blk), 0)
    t = lax.broadcasted_iota(I32, (blk, blk), 1)
    for which, off in ((0, -blk), (1, 0)):
        rel = s - t + off
        ret = jnp.where(rel > 0, nb, 0)
        n = jnp.abs(rel)
        large = max_exact + (jnp.log(jnp.maximum(n, 1).astype(F32) / max_exact)
                             / math.log(REL_MAX_DIST / max_exact) * (nb - max_exact)).astype(I32)
        large = jnp.minimum(large, nb - 1)
        bucket = ret + jnp.where(n < max_exact, n, large)
        for h in range(DSA_HEADS):
            acc = jnp.zeros((blk, blk), F32)
            for b in range(REL_BUCKETS):
                acc = jnp.where(bucket == b, rb_ref[b, h], acc)
            o_ref[which, h] = (acc - rb_ref[nb - 1, h]) * _LOG2E


def _bias_tiles(rel_bias, blk):
    assert blk >= REL_MAX_DIST
    return pl.pallas_call(
        _bias_kernel,
        out_shape=jax.ShapeDtypeStruct((2, DSA_HEADS, blk, blk), F32),
        in_specs=[pl.BlockSpec(memory_space=pltpu.SMEM)],
        out_specs=pl.BlockSpec(memory_space=pltpu.VMEM),
        name="bias_tiles",
    )(rel_bias)


_C_Q, _C_KV, _C_RA, _C_RB, _C_QB, _C_KK, _C_VV, _C_QI, _C_KI, _C_END = (
    0, 256, 384, 512, 640, 1152, 1280, 1408, 1664, 1792)


def _value_rows(vt, dv):
    t = vt.shape[1]
    return jnp.concatenate([vt, jnp.ones((1, t), F32), jnp.zeros((_VROWS - dv - 1, t), F32)], axis=0)


def _in_proj_kernel(x_ref, pos_ref, g0_ref, b0_ref, w1_ref, qg_ref, wq_ref, wqr_ref, kg_ref, wk_ref,
                    wvt_ref, one_ref, invf_ref, sgn_ref, cb_ref, sb_ref,
                    qm_ref, km_ref, vmt_ref, qb_ref, kk_ref, vvt_ref, qi_ref, ki_ref, wt_ref, *, blk):
    tm = x_ref.shape[0]
    xn = _ln(x_ref[...], g0_ref[...], b0_ref[...])
    proj = _dot(xn.astype(BF16), w1_ref[...])
    qb_ref[...] = (proj[:, _C_QB:_C_KK] * _LOG2E).astype(BF16)
    kk_ref[...] = proj[:, _C_KK:_C_VV].astype(BF16)
    qi_ref[...] = proj[:, _C_QI:_C_KI].astype(BF16)
    ki_ref[...] = proj[:, _C_KI:_C_END].astype(BF16)
    ga = proj[:, _C_RA:_C_RB]
    gb = proj[:, _C_RB:_C_QB]
    wt_ref[...] = ga.T[0:IDX_HEADS, :] * (IDX_HEADS ** -0.5 * IDX_DIM ** -0.5)
    vvt = _value_rows(proj[:, _C_VV:_C_QI].T[0:DSA_HEAD_DIM, :], DSA_HEAD_DIM).astype(BF16)
    for c in range(tm // blk):
        vvt_ref[c] = vvt[:, c * blk:(c + 1) * blk]

    groups = tm // CHUNK
    a = (pos_ref[0:1, :] >> _CHUNK_SHIFT) + lax.broadcasted_iota(I32, (groups, 1), 0)
    ang = (a * CHUNK).astype(F32) * invf_ref[...]
    ca, sa = jnp.cos(ang), jnp.sin(ang)
    cb, sb = cb_ref[...], sb_ref[...]
    cosv = jnp.concatenate([ca[g:g + 1] * cb - sa[g:g + 1] * sb for g in range(groups)], axis=0)
    sinv = jnp.concatenate([sa[g:g + 1] * cb + ca[g:g + 1] * sb for g in range(groups)], axis=0) * sgn_ref[...]
    lane = lax.broadcasted_iota(I32, cosv.shape, 1)
    rope_lane = (lane >= MLA_NOPE) & (lane < MLA_NOPE + MLA_ROPE)
    kr = jnp.where(rope_lane, ga * cosv + gb * sinv, 0.0)

    scale = (MLA_NOPE + MLA_ROPE) ** -0.5 * _LOG2E
    cos8 = jnp.concatenate([cosv * scale] * MLA_HEADS, axis=1)
    sin8 = jnp.concatenate([sinv * scale] * MLA_HEADS, axis=1)
    cqn = _rms(proj[:, _C_Q:_C_KV], qg_ref[...]).astype(BF16)
    q = _dot(cqn, wq_ref[...]) * cos8 + _dot(cqn, wqr_ref[...]) * sin8
    qm_ref[...] = q.astype(BF16)

    ckn = _rms(proj[:, _C_KV:_C_RA], kg_ref[...]).astype(BF16)
    k = _dot(ckn, wk_ref[...]) + jnp.concatenate([kr] * MLA_HEADS, axis=1)
    km_ref[...] = k.astype(BF16)
    vmt = (_dot_nt(wvt_ref[...], ckn) + one_ref[...]).astype(BF16)
    for c in range(tm // blk):
        vmt_ref[c] = vmt[:, c * blk:(c + 1) * blk]


def _in_proj(x2, pos2, g0, b0, w1, qg, wq, wqr, kg, wk, wvt, one, invf, sgn, cb, sb, tm, blk):
    n, d = x2.shape
    assert tm % blk == 0 and blk % CHUNK == 0
    hm = MLA_HEADS * LANES
    row = lambda w: pl.BlockSpec((tm, w), lambda i: (i, 0))
    full = lambda a: pl.BlockSpec(a.shape, lambda i: (0,) * a.ndim)
    slab = lambda r: pl.BlockSpec((tm // blk, r, blk), lambda i: (i, 0, 0))
    tok = lambda w, dt: (jax.ShapeDtypeStruct((n, w), dt), row(w))
    slb = lambda r: (jax.ShapeDtypeStruct((n // blk, r, blk), BF16), slab(r))
    outs = [tok(hm, BF16), tok(hm, BF16), slb(MLA_HEADS * _VROWS), tok(DSA_HEADS * DSA_HEAD_DIM, BF16),
            tok(LANES, BF16), slb(_VROWS), tok(IDX_HEADS * IDX_DIM, BF16), tok(LANES, BF16),
            (jax.ShapeDtypeStruct((IDX_HEADS, n), F32), pl.BlockSpec((IDX_HEADS, tm), lambda i: (0, i)))]
    consts = (g0, b0, w1, qg, wq, wqr, kg, wk, wvt, one, invf, sgn, cb, sb)
    return pl.pallas_call(
        functools.partial(_in_proj_kernel, blk=blk),
        grid=(n // tm,),
        in_specs=[row(d), row(1)] + [full(a) for a in consts],
        out_specs=[s for _, s in outs],
        out_shape=[s for s, _ in outs],
        compiler_params=pltpu.CompilerParams(dimension_semantics=("parallel",),
                                             vmem_limit_bytes=VMEM_LIMIT),
        name="in_proj",
    )(x2, pos2, *consts)


def _admissible(ck, tq, k0, q0):
    s = k0 + lax.broadcasted_iota(I32, (ck, tq), 0)
    t = q0 + lax.broadcasted_iota(I32, (ck, tq), 1)
    return (s >> _CHUNK_SHIFT) <= (t >> _CHUNK_SHIFT)


def _softmax_step(s, vt, m_ref, acc_ref, h):
    sb = s.astype(BF16)
    m_prev = m_ref[h]
    m_new = jnp.maximum(m_prev, jnp.max(sb, axis=0, keepdims=True).astype(F32))
    alpha = jnp.exp2(m_prev - m_new)
    p = jnp.exp2(sb - m_new.astype(BF16))
    acc_ref[h] = alpha * acc_ref[h] + _dot(vt, p)
    m_ref[h] = m_new


_STEP_WIDTHS = (4, 2, 1)


def _scores_first(s_ref, nh, produce, consume):
    for h in range(nh):
        s_ref[h] = produce(h)
    for h in range(nh):
        consume(h, s_ref[h])


def _finish(o_ref, acc_ref, nh, dv):
    ot = jnp.concatenate([acc_ref[h, :dv, :] / acc_ref[h, dv:dv + 1, :] for h in range(nh)], axis=0)
    o_ref[...] = ot.T.astype(BF16)


def _mla_kernel(q_ref, k_ref, vt_ref, o_ref, s_ref, m_ref, acc_ref, *, tq, ck):
    j = pl.program_id(1)
    q0 = j * tq
    m_ref[...] = jnp.full(m_ref.shape, NEG, F32)
    acc_ref[...] = jnp.zeros(acc_ref.shape, F32)
    nfull = (q0 + CHUNK) // ck

    def chunk(c, width, masked):
        k0 = pl.multiple_of(c * ck, ck)
        rows = width * ck
        adm = _admissible(rows, tq, k0, q0) if masked else None

        def score(h):
            return _dot_nt(k_ref[pl.ds(k0, rows), h * LANES:(h + 1) * LANES], q_ref[:, h * LANES:(h + 1) * LANES])

        def use(h, s):
            if masked:
                s = jnp.where(adm, s, NEG)
            vt = jnp.concatenate([vt_ref[c + i, h * _VROWS:(h + 1) * _VROWS, :] for i in range(width)], axis=1)
            _softmax_step(s, vt, m_ref, acc_ref, h)

        _scores_first(s_ref.at[:, 0:rows], MLA_HEADS, score, use)

    odd = nfull % 2
    done = 0
    for width in _STEP_WIDTHS:
        steps = (nfull - odd - done) // width

        def body(i, carry, width=width, done=done):
            chunk(done + i * width, width, False)
            return carry

        lax.fori_loop(0, steps, body, 0)
        done = done + steps * width

    @pl.when(odd == 1)
    def _last_two():
        chunk(nfull - 1, 2, True)

    @pl.when(odd == 0)
    def _last_one():
        chunk(nfull, 1, True)
    _finish(o_ref, acc_ref, MLA_HEADS, MLA_V)


def _mla_attn(qm, km, vmt, tq, ck):
    b, t, hm = qm.shape
    dv = MLA_HEADS * MLA_V
    assert ck % tq == 0 and tq > CHUNK and tq % CHUNK == 0 and t % ck == 0
    return pl.pallas_call(
        functools.partial(_mla_kernel, tq=tq, ck=ck),
        grid=(b, t // tq),
        in_specs=[pl.BlockSpec((None, tq, hm), lambda i, j: (i, j, 0)),
                  pl.BlockSpec((None, t, hm), lambda i, j: (i, 0, 0)),
                  pl.BlockSpec((None, t // ck, MLA_HEADS * _VROWS, ck), lambda i, j: (i, 0, 0, 0))],
        out_specs=pl.BlockSpec((None, tq, dv), lambda i, j: (i, j, 0)),
        out_shape=jax.ShapeDtypeStruct((b, t, dv), BF16),
        scratch_shapes=[pltpu.VMEM((MLA_HEADS, _STEP_WIDTHS[0] * ck, tq), F32),
                        pltpu.VMEM((MLA_HEADS, 1, tq), F32),
                        pltpu.VMEM((MLA_HEADS, _VROWS, tq), F32)],
        compiler_params=pltpu.CompilerParams(dimension_semantics=("parallel", "arbitrary"),
                                             vmem_limit_bytes=VMEM_LIMIT),
        name="mla_attn",
    )(qm, km, vmt)


def _dsa_kernel(qb_ref, qi_ref, wt_ref, kk_ref, vt_ref, ki_ref, bt_ref, o_ref,
                qs_ref, qim_ref, keys_ref, khi_ref, klo_ref, kls_ref, s_ref, thr_ref, cut_ref, m_ref, acc_ref,
                *, tq, topk):
    nh = DSA_HEADS
    ck = tq
    seq = kk_ref.shape[0]
    j = pl.program_id(1)
    q0 = j * tq
    nck = j + 1
    lane = lax.broadcasted_iota(I32, (tq, LANES), 1)

    for h in range(nh):
        pair = qb_ref[:, (h // 2) * LANES:(h // 2 + 1) * LANES].astype(F32)
        in_head = (lane >= (h % 2) * DSA_HEAD_DIM) & (lane < (h % 2 + 1) * DSA_HEAD_DIM)
        qs_ref[h * tq:(h + 1) * tq, :] = jnp.where(in_head, pair, 0.0).astype(BF16)
        grp = qi_ref[:, (h // 4) * LANES:(h // 4 + 1) * LANES].astype(F32)
        in_head = (lane >= (h % 4) * IDX_DIM) & (lane < (h % 4 + 1) * IDX_DIM)
        qim_ref[h * tq:(h + 1) * tq, :] = jnp.where(in_head, grp, 0.0).astype(BF16)

    w = wt_ref[...]

    def widest_first(n_chunks, fn):
        done = 0
        for width in _STEP_WIDTHS:
            steps = (n_chunks - done) // width

            def body(i, carry, width=width, done=done):
                fn(done + i * width, width)
                return carry

            lax.fori_loop(0, steps, body, 0)
            done = done + steps * width

    def idx_chunk(c, width, masked):
        k0 = pl.multiple_of(c * ck, ck)
        rows = width * ck
        kc = ki_ref[pl.ds(k0, rows), :]
        total = [jnp.zeros((rows, tq), F32)]

        def add(h, r):
            total[0] = total[0] + w[h:h + 1, :] * jnp.maximum(r, 0.0)

        _scores_first(s_ref.at[:, 0:rows], nh, lambda h: _dot_nt(kc, qim_ref[h * tq:(h + 1) * tq, :]), add)
        score = total[0]
        if masked:
            score = jnp.where(_admissible(rows, tq, k0, q0), score, NEG)
        score = jnp.where(score == 0.0, 0.0, score)
        bits = lax.bitcast_convert_type(score, I32)
        key = bits ^ ((bits >> 31) & 0x7FFFFFFF)
        for i in range(width):
            ki = key[i * ck:(i + 1) * ck]
            keys_ref[c + i] = ki
            khi_ref[c + i] = (ki >> 16).astype(I16)
            klo_ref[c + i] = ((ki & 0xFFFF) - 0x8000).astype(I16)

    widest_first(j, lambda c, width: idx_chunk(c, width, False))
    idx_chunk(j, 1, True)

    thr_ref[...] = jnp.full(thr_ref.shape, _KEY_NEG + 1, I32)
    cut_ref[...] = jnp.full(cut_ref.shape, seq, I32)

    def count16(ref, cand):
        rows = 2 * _SUBLANES

        def hits(c):
            part = jnp.where(ref[c] >= cand, jnp.int16(1), jnp.int16(0)).reshape(ck // rows, rows, tq)
            while part.shape[0] > 1:
                half = part.shape[0] // 2
                part = part[:half] + part[half:]
            return part[0]

        unroll = 4
        acc = lax.fori_loop(0, nck // unroll,
                            lambda i, a: a + sum(hits(unroll * i + u) for u in range(1, unroll)) + hits(unroll * i),
                            jnp.zeros((rows, tq), I16))
        acc = lax.fori_loop(nck // unroll * unroll, nck, lambda c, a: a + hits(c), acc)
        return jnp.sum(acc.astype(F32), axis=0, keepdims=True)

    def greedy16(ref, need):
        def step(i, carry):
            v, cge, cgt = carry
            cand = v + lax.shift_left(jnp.int32(1), 15 - i)
            cnt = count16(ref, cand.astype(I16))
            ok = cnt >= need
            return jnp.where(ok, cand, v), jnp.where(ok, cnt, cge), jnp.where(ok, cgt, cnt)

        v0 = jnp.full((1, tq), _I16_MIN, I32)
        cge0 = jnp.full((1, tq), 1.0, F32) * (nck * ck).astype(F32)
        return lax.fori_loop(0, 16, step, (v0, cge0, jnp.zeros((1, tq), F32)))

    @pl.when(q0 + tq > topk)
    def _search():
        hi, cge_hi, cgt_hi = greedy16(khi_ref, float(topk))
        hi16 = hi.astype(I16)

        def mask_body(c, carry):
            kls_ref[c] = jnp.where(khi_ref[c] == hi16, klo_ref[c], jnp.int16(_I16_MIN))
            return carry

        lax.fori_loop(0, nck, mask_body, 0)
        lo, cge_lo, _ = greedy16(kls_ref, float(topk) - cgt_hi)
        thr = hi * 65536 + (lo + 0x8000)
        thr_ref[...] = thr
        cge = cgt_hi + jnp.where(lo == _I16_MIN, cge_hi - cgt_hi, cge_lo)
        excess = cge - float(topk)

        @pl.when(jnp.max(excess) > 0.0)
        def _ties():
            nbits = max(1, int(seq - 1).bit_length())
            lo16 = lo.astype(I16)

            def tie_body(c, carry):
                idx = (c * ck + lax.broadcasted_iota(I32, (ck, tq), 0)).astype(I16)
                tied = (khi_ref[c] == hi16) & (klo_ref[c] == lo16)
                kls_ref[c] = jnp.where(tied, idx, jnp.int16(-1))
                return carry

            lax.fori_loop(0, nck, tie_body, 0)

            def tstep(i, cut):
                cand = cut + lax.shift_left(jnp.int32(1), nbits - 1 - i)
                return jnp.where(count16(kls_ref, cand.astype(I16)) >= excess, cand, cut)

            cut = lax.fori_loop(0, nbits, tstep, jnp.zeros((1, tq), I32))
            cut_ref[...] = jnp.where(excess > 0.0, cut, seq)

    m_ref[...] = jnp.full(m_ref.shape, NEG, F32)
    acc_ref[...] = jnp.zeros(acc_ref.shape, F32)
    thr = thr_ref[...]
    cut = cut_ref[...]

    def att_chunk(c, width, bias_slab, own):
        k0 = pl.multiple_of(c * ck, ck)
        rows = width * ck
        kc = kk_ref[pl.ds(k0, rows), :]
        key = jnp.concatenate([keys_ref[c + i] for i in range(width)], axis=0)
        idx = k0 + lax.broadcasted_iota(I32, (rows, tq), 0)
        sel = (key > thr) | ((key == thr) & (idx < cut))
        if own:
            sel = sel & _admissible(rows, tq, k0, q0)
        vt = jnp.concatenate([vt_ref[c + i] for i in range(width)], axis=1)

        def use(h, s):
            if bias_slab is not None:
                s = s + jnp.concatenate([bt_ref[b, h] for b in bias_slab], axis=0)
            _softmax_step(jnp.where(sel, s, NEG), vt, m_ref, acc_ref, h)

        _scores_first(s_ref.at[:, 0:rows], nh, lambda h: _dot_nt(kc, qs_ref[h * tq:(h + 1) * tq, :]), use)

    widest_first(jnp.maximum(j - 1, 0), lambda c, width: att_chunk(c, width, None, False))

    @pl.when(j > 0)
    def _near():
        att_chunk(j - 1, 2, (0, 1), True)

    @pl.when(j == 0)
    def _first_block():
        att_chunk(0, 1, (1,), True)

    _finish(o_ref, acc_ref, nh, DSA_HEAD_DIM)


def _dsa_attn(qb, qi, wt, kk, vt, ki, bt, tq, topk):
    b, t, _ = qb.shape
    ck = tq
    assert tq == bt.shape[2] and tq % LANES == 0 and t % tq == 0 and t < -_I16_MIN
    nh = DSA_HEADS
    blk = lambda a: pl.BlockSpec((None, tq, a.shape[2]), lambda i, j: (i, j, 0))
    seqb = lambda a: pl.BlockSpec((None, t, a.shape[2]), lambda i, j: (i, 0, 0))
    return pl.pallas_call(
        functools.partial(_dsa_kernel, tq=tq, topk=topk),
        grid=(b, t // tq),
        in_specs=[blk(qb), blk(qi), pl.BlockSpec((nh, tq), lambda i, j: (0, i * (t // tq) + j)), seqb(kk),
                  pl.BlockSpec((None, t // ck, _VROWS, ck), lambda i, j: (i, 0, 0, 0)), seqb(ki),
                  pl.BlockSpec(bt.shape, lambda i, j: (0, 0, 0, 0))],
        out_specs=blk(qb),
        out_shape=jax.ShapeDtypeStruct(qb.shape, BF16),
        scratch_shapes=[pltpu.VMEM((nh * tq, LANES), BF16), pltpu.VMEM((nh * tq, LANES), BF16),
                        pltpu.VMEM((t // ck, ck, tq), I32), pltpu.VMEM((t // ck, ck, tq), I16),
                        pltpu.VMEM((t // ck, ck, tq), I16), pltpu.VMEM((t // ck, ck, tq), I16),
                        pltpu.VMEM((nh, _STEP_WIDTHS[0] * ck, tq), F32),
                        pltpu.VMEM((1, tq), I32),
                        pltpu.VMEM((1, tq), I32), pltpu.VMEM((nh, 1, tq), F32),
                        pltpu.VMEM((nh, _VROWS, tq), F32)],
        compiler_params=pltpu.CompilerParams(dimension_semantics=("parallel", "arbitrary"),
                                             vmem_limit_bytes=VMEM_LIMIT),
        name="dsa_attn",
    )(qb, qi, wt, kk, vt, ki, bt)


_R_EXP = N_GROUPS


def _post_kernel(x_ref, oa_ref, ob_ref, g0_ref, b0_ref, wg_ref, bg_ref, wua_ref, wub_ref, wo_ref,
                 g1_ref, b1_ref, wr_ref, br_ref, hx_ref, gsel_ref, *, alpha):
    d = x_ref.shape[1]
    xn = _ln(x_ref[...], g0_ref[...], b0_ref[...])
    z = _dot(xn.astype(BF16), wg_ref[...]) + bg_ref[...]
    gates = 1.0 / (1.0 + jnp.exp(-z))
    mix = gates[:, :d] * _dot(oa_ref[...], wua_ref[...]) + gates[:, d:] * _dot(ob_ref[...], wub_ref[...])
    mixed = _dot(mix.astype(BF16), wo_ref[...])
    h = _ln(alpha * xn + mixed, g1_ref[...], b1_ref[...])
    hx_ref[:, :d] = h

    logits = _dot(h.astype(BF16), wr_ref[...]) + br_ref[...]
    lane = lax.broadcasted_iota(I32, logits.shape, 1).astype(F32)
    gmask = lane < N_GROUPS
    gl = jnp.where(gmask, logits, -jnp.inf)
    gmax = jnp.max(gl, axis=1, keepdims=True)
    g_sel = jnp.min(jnp.where(gl == gmax, lane, float(LANES)), axis=1, keepdims=True)
    g_w = 1.0 / jnp.sum(jnp.where(gmask, jnp.exp(gl - gmax), 0.0), axis=1, keepdims=True)
    e_lo = _R_EXP + g_sel * EXPERTS_PER_GROUP
    el = jnp.where((lane >= e_lo) & (lane < e_lo + EXPERTS_PER_GROUP), logits, -jnp.inf)
    m1 = jnp.max(el, axis=1, keepdims=True)
    i1 = jnp.min(jnp.where(el == m1, lane, float(LANES)), axis=1, keepdims=True)
    el2 = jnp.where(lane == i1, -jnp.inf, el)
    m2 = jnp.max(el2, axis=1, keepdims=True)
    i2 = jnp.min(jnp.where(el2 == m2, lane, float(LANES)), axis=1, keepdims=True)
    e21 = jnp.exp(m2 - m1)
    w1 = g_w / (1.0 + e21)
    w2 = g_w * e21 / (1.0 + e21)
    hx_ref[:, d:] = jnp.where(lane == i1, w1, 0.0) + jnp.where(lane == i2, w2, 0.0)
    gsel_ref[...] = g_sel.astype(I32)


def _post_attn(x2, oa, ob, g0, b0, wg, bg, wua, wub, wo, g1, b1, wr, br, alpha, tm):
    n, d = x2.shape
    row = lambda w: pl.BlockSpec((tm, w), lambda i: (i, 0))
    full = lambda a: pl.BlockSpec(a.shape, lambda i: (0,) * a.ndim)
    return pl.pallas_call(
        functools.partial(_post_kernel, alpha=alpha),
        grid=(n // tm,),
        in_specs=[row(d), row(oa.shape[1]), row(ob.shape[1])]
                 + [full(a) for a in (g0, b0, wg, bg, wua, wub, wo, g1, b1, wr, br)],
        out_specs=[row(d + LANES), row(1)],
        out_shape=[jax.ShapeDtypeStruct((n, d + LANES), F32), jax.ShapeDtypeStruct((n, 1), I32)],
        compiler_params=pltpu.CompilerParams(dimension_semantics=("parallel",),
                                             vmem_limit_bytes=VMEM_LIMIT),
        name="post_attn",
    )(x2, oa, ob, g0, b0, wg, bg, wua, wub, wo, g1, b1, wr, br)


def _moe_kernel(tg_ref, nv_ref, src_ref, srcn_ref, hx_hbm, weg_ref, weu_ref, wed_ref, g2_ref, b2_ref,
                out_hbm, gbuf, obuf, gsem, ssem, *, alpha, tm, d):
    k = pl.program_id(0)
    nk = pl.num_programs(0)
    slot = k % 2
    nrun = tm // LANES

    def gather_copy(tok, r, s):
        return pltpu.make_async_copy(hx_hbm.at[pl.ds(tok, 1)], gbuf.at[s, pl.ds(r, 1)], gsem.at[s])

    def scatter_copy(tok, r, s):
        return pltpu.make_async_copy(obuf.at[s, pl.ds(r, 1)], out_hbm.at[pl.ds(tok, 1)], ssem.at[s])

    def start_gather(tile_src_ref, s):
        for g in range(nrun):
            for c in range(LANES):
                gather_copy(tile_src_ref[g, c], g * LANES + c, s).start(priority=c % 2)

    def start_scatter(n_rows, s):
        for g in range(nrun):
            @pl.when(n_rows >= (g + 1) * LANES)
            def _run():
                for c in range(LANES):
                    scatter_copy(src_ref[g, c], g * LANES + c, s).start(priority=c % 2)

        def row(r, carry):
            scatter_copy(src_ref[r // LANES, r % LANES], r, s).start()
            return carry

        lax.fori_loop(n_rows // LANES * LANES, n_rows, row, 0)

    def by_slot(fn, s_of_slot):
        for par in (0, 1):
            pl.when(slot == par)(functools.partial(fn, s_of_slot(par)))

    def wait_scatter(n_rows, s):
        def waiter(rows):
            def body(i, carry):
                pltpu.make_async_copy(obuf.at[s, pl.ds(0, rows)], out_hbm.at[pl.ds(0, rows)], ssem.at[s]).wait()
                return carry
            return body

        lax.fori_loop(0, n_rows // LANES, waiter(LANES), 0)
        lax.fori_loop(0, n_rows % LANES // _SUBLANES, waiter(_SUBLANES), 0)
        lax.fori_loop(0, n_rows % _SUBLANES, waiter(1), 0)

    @pl.when(k == 0)
    def _first():
        start_gather(src_ref, 0)

    pltpu.make_async_copy(hx_hbm.at[pl.ds(0, tm)], gbuf.at[slot], gsem.at[slot]).wait()

    @pl.when(k + 1 < nk)
    def _prefetch():
        by_slot(lambda s: start_gather(srcn_ref, s), lambda par: 1 - par)

    @pl.when(k >= 2)
    def _free_obuf():
        wait_scatter(nv_ref[k - 2], slot)

    x = gbuf[slot]
    h = x[:, :d]
    comb = x[:, d:]
    hb = h.astype(BF16)
    lane = lax.broadcasted_iota(I32, comb.shape, 1)
    e0 = _R_EXP + tg_ref[k] * EXPERTS_PER_GROUP
    acc = jnp.zeros((tm, d), F32)
    for e in range(EXPERTS_PER_GROUP):
        a = _dot(hb, weg_ref[e])
        u = _dot(hb, weu_ref[e])
        cw = jnp.sum(jnp.where(lane == e0 + e, comb, 0.0), axis=1, keepdims=True)
        hid = (a / (1.0 + jnp.exp(-a))) * u * cw
        acc = acc + _dot(hid.astype(BF16), wed_ref[e])
    obuf[slot] = _ln(alpha * h + acc, g2_ref[...], b2_ref[...])

    nv = nv_ref[k]
    by_slot(lambda s: start_scatter(nv, s), lambda par: par)

    @pl.when(k == nk - 1)
    def _drain():
        wait_scatter(nv, slot)

        @pl.when(k >= 1)
        def _():
            wait_scatter(nv_ref[k - 1], 1 - slot)


def _moe(hx, tile_group, n_valid, src, weg, weu, wed, g2, b2, alpha, tm, n):
    d = hx.shape[1] - LANES
    nk = src.shape[0]
    ne, _, f = weg.shape
    gsz = EXPERTS_PER_GROUP
    grid_spec = pltpu.PrefetchScalarGridSpec(
        num_scalar_prefetch=2,
        grid=(nk,),
        in_specs=[pl.BlockSpec((None, tm // LANES, LANES), lambda k, tg, nv: (k, 0, 0),
                               memory_space=pltpu.SMEM),
                  pl.BlockSpec((None, tm // LANES, LANES),
                               lambda k, tg, nv: (jnp.minimum(k + 1, nk - 1), 0, 0), memory_space=pltpu.SMEM),
                  pl.BlockSpec(memory_space=pl.ANY),
                  pl.BlockSpec((gsz, d, f), lambda k, tg, nv: (tg[k], 0, 0)),
                  pl.BlockSpec((gsz, d, f), lambda k, tg, nv: (tg[k], 0, 0)),
                  pl.BlockSpec((gsz, f, d), lambda k, tg, nv: (tg[k], 0, 0)),
                  pl.BlockSpec(g2.shape, lambda k, tg, nv: (0, 0)),
                  pl.BlockSpec(b2.shape, lambda k, tg, nv: (0, 0))],
        out_specs=pl.BlockSpec(memory_space=pl.ANY),
        scratch_shapes=[pltpu.VMEM((2, tm, d + LANES), F32), pltpu.VMEM((2, tm, d), F32),
                        pltpu.SemaphoreType.DMA((2,)), pltpu.SemaphoreType.DMA((2,))])
    return pl.pallas_call(
        functools.partial(_moe_kernel, alpha=alpha, tm=tm, d=d),
        grid_spec=grid_spec,
        out_shape=jax.ShapeDtypeStruct((n, d), F32),
        compiler_params=pltpu.CompilerParams(dimension_semantics=("arbitrary",),
                                             vmem_limit_bytes=VMEM_LIMIT),
        name="moe",
    )(tile_group, n_valid, src, src, hx, weg, weu, wed, g2, b2)


def _slot_tokens_kernel(dest_ref, lo_ref, hi_ref, src_ref):
    for g in range(N_GROUPS):
        def pad(p, carry):
            src_ref[p] = 0
            return carry
        lax.fori_loop(lo_ref[g], hi_ref[g], pad, 0)

    def put(i, carry):
        src_ref[dest_ref[i]] = i
        return carry

    lax.fori_loop(0, dest_ref.shape[0], put, 0, unroll=8)


def _slot_tokens(dest, pad_lo, pad_hi, n_slots):
    smem = pl.BlockSpec(memory_space=pltpu.SMEM)
    return pl.pallas_call(
        _slot_tokens_kernel,
        out_shape=jax.ShapeDtypeStruct((n_slots,), I32),
        in_specs=[smem, smem, smem],
        out_specs=smem,
        name="slot_tokens",
    )(dest, pad_lo, pad_hi)


def _group_tiles(gsel, tm):
    n = gsel.shape[0]
    n_tiles = n // tm + N_GROUPS
    onehot = (gsel[:, None] == jnp.arange(N_GROUPS, dtype=I32)[None, :]).astype(I32)
    csum = jnp.cumsum(onehot, axis=0)
    rank = jnp.sum(csum * onehot, axis=1) - 1
    counts = csum[-1]
    padded = (counts + tm - 1) // tm * tm
    ends = jnp.cumsum(padded)
    offs = ends - padded
    dest = offs[gsel] + rank
    pad_lo = offs + counts
    pad_hi = ends.at[N_GROUPS - 1].set(n_tiles * tm)
    src = _slot_tokens(dest.astype(I32), pad_lo.astype(I32), pad_hi.astype(I32), n_tiles * tm)
    tile_start = jnp.arange(n_tiles, dtype=I32) * tm
    tile_group = jnp.minimum(jnp.sum((tile_start[:, None] >= ends[None, :]).astype(I32), axis=1), N_GROUPS - 1)
    n_valid = jnp.clip(offs[tile_group] + counts[tile_group] - tile_start, 0, tm)
    return tile_group.astype(I32), n_valid.astype(I32), src.reshape(n_tiles, tm // LANES, LANES)


def _layout_weights(w_in, w_uq, w_uk, w_uv):
    d = w_in.shape[0]
    sizes = (MLA_Q_RANK, MLA_KV_RANK, MLA_ROPE, DSA_HEADS * DSA_HEAD_DIM, DSA_HEAD_DIM, DSA_HEAD_DIM,
             IDX_HEADS * IDX_DIM, IDX_DIM, IDX_HEADS)
    offs = np.cumsum((0,) + sizes)
    c_q, c_kv, k_rope, q_b, k_b, v_b, q_idx, k_idx, w_idx = (
        w_in[:, int(offs[i]):int(offs[i + 1])] for i in range(len(sizes)))
    half = MLA_ROPE // 2
    z = lambda w: jnp.zeros((d, w), w_in.dtype)
    pad = LANES - MLA_NOPE - MLA_ROPE
    grp_a = jnp.concatenate([w_idx, z(MLA_NOPE - IDX_HEADS), k_rope, z(pad)], axis=1)
    grp_b = jnp.concatenate([z(MLA_NOPE), k_rope[:, half:], k_rope[:, :half], z(pad)], axis=1)
    w1 = jnp.concatenate([c_q, c_kv, grp_a, grp_b, q_b * DSA_HEAD_DIM ** -0.5, k_b, k_b, v_b, v_b, q_idx,
                          k_idx, k_idx, k_idx, k_idx], axis=1)
    assert w1.shape[1] == _C_END

    dk = MLA_NOPE + MLA_ROPE
    rq = w_uq.shape[0]
    zq = lambda w: jnp.zeros((rq, w), w_uq.dtype)
    wq_parts, wqr_parts, wk_parts = [], [], []
    for h in range(MLA_HEADS):
        nope = w_uq[:, h * dk:h * dk + MLA_NOPE]
        rope = w_uq[:, h * dk + MLA_NOPE:(h + 1) * dk]
        wq_parts += [nope, rope, zq(pad)]
        wqr_parts += [zq(MLA_NOPE), rope[:, half:], rope[:, :half], zq(pad)]
        wk_parts += [w_uk[:, h * MLA_NOPE:(h + 1) * MLA_NOPE],
                     jnp.zeros((w_uk.shape[0], LANES - MLA_NOPE), w_uk.dtype)]
    wq = jnp.concatenate(wq_parts, axis=1)
    wqr = jnp.concatenate(wqr_parts, axis=1)
    wk = jnp.concatenate(wk_parts, axis=1)

    inv = ROPE_THETA ** (-np.arange(half, dtype=np.float32) / half)
    invf = np.zeros((1, LANES), np.float32)
    invf[0, MLA_NOPE:MLA_NOPE + half] = inv
    invf[0, MLA_NOPE + half:MLA_NOPE + MLA_ROPE] = inv
    sgn = np.zeros((1, LANES), np.float32)
    sgn[0, MLA_NOPE:MLA_NOPE + half] = -1.0
    sgn[0, MLA_NOPE + half:MLA_NOPE + MLA_ROPE] = 1.0
    ang_b = np.arange(CHUNK, dtype=np.float64)[:, None] * invf.astype(np.float64)
    cb, sb = np.cos(ang_b).astype(np.float32), np.sin(ang_b).astype(np.float32)

    kvr = w_uv.shape[0]
    wvt = jnp.concatenate(
        [jnp.concatenate([w_uv[:, h * MLA_V:(h + 1) * MLA_V].T, jnp.zeros((_VROWS - MLA_V, kvr), w_uv.dtype)], axis=0)
         for h in range(MLA_HEADS)], axis=0)
    one = np.zeros((MLA_HEADS * _VROWS, 1), np.float32)
    one[MLA_V::_VROWS, 0] = 1.0
    return (w1.astype(BF16), wq.astype(BF16), wqr.astype(BF16), wk.astype(BF16), wvt.astype(BF16),
            jnp.asarray(one), jnp.asarray(invf), jnp.asarray(sgn), jnp.asarray(cb), jnp.asarray(sb))


def _tiles(n, seq):
    blk, tm_tok, tm_moe = 256, 512, 512
    assert seq % blk == 0 and n % tm_tok == 0 and tm_tok % blk == 0 and n % tm_moe == 0 and tm_moe % LANES == 0
    return blk, tm_tok, tm_moe


def kernel(x, positions, ln0_g, ln0_b, w_in, q_norm_g, w_uq, kv_norm_g, w_uk, w_uv, rel_bias, w_up_a, w_up_b,
           w_gate, b_gate, w_o, ln1_g, ln1_b, w_grp, b_grp, w_rt, b_rt, w_exp_gate, w_exp_up, w_exp_down,
           ln2_g, ln2_b):
    bsz, seq, d = x.shape
    depth = w_in.shape[0]
    assert depth == 1
    alpha = (2.0 * depth) ** 0.25
    n = bsz * seq
    topk = min(IDX_TOPK_MAX, seq // 4)
    r2 = lambda v: v.reshape(1, -1)

    x2 = x.reshape(n, d)
    pos2 = positions.reshape(n, 1)
    w1, wq, wqr, wk, wvt, one, invf, sgn, cb, sb = _layout_weights(w_in[0], w_uq[0], w_uk[0], w_uv[0])

    blk, tm_tok, tm_moe = _tiles(n, seq)
    bt = _bias_tiles(rel_bias, blk)
    qm, km, vmt, qb, kk, vvt, qi, ki, wt = _in_proj(
        x2, pos2, r2(ln0_g), r2(ln0_b), w1, r2(q_norm_g[0]), wq, wqr, r2(kv_norm_g[0]), wk, wvt, one, invf, sgn,
        cb, sb, tm=tm_tok, blk=blk)
    b3 = lambda a: a.reshape(bsz, seq, a.shape[1])
    c4 = lambda a: a.reshape(bsz, seq // blk, a.shape[1], blk)
    o_a = _mla_attn(b3(qm), b3(km), c4(vmt), tq=blk, ck=blk)
    o_b = _dsa_attn(b3(qb), b3(qi), wt, b3(kk), c4(vvt), b3(ki), bt, tq=blk, topk=topk)

    wr = jnp.concatenate([w_grp[0], w_rt[0], jnp.zeros((d, LANES - N_GROUPS - N_EXPERTS), F32)], axis=1)
    br = jnp.concatenate([b_grp[0], b_rt[0], jnp.zeros((LANES - N_GROUPS - N_EXPERTS,), F32)])
    hx, gsel = _post_attn(
        x2, o_a.reshape(n, -1), o_b.reshape(n, -1), r2(ln0_g), r2(ln0_b), w_gate[0].astype(BF16), r2(b_gate[0]),
        w_up_a[0].astype(BF16), w_up_b[0].astype(BF16), w_o[0].astype(BF16), r2(ln1_g[0]), r2(ln1_b[0]),
        wr.astype(BF16), r2(br), alpha, tm=tm_tok)
    tile_group, n_valid, src = _group_tiles(gsel[:, 0], tm_moe)
    out = _moe(hx, tile_group, n_valid, src, w_exp_gate[0].astype(BF16), w_exp_up[0].astype(BF16),
               w_exp_down[0].astype(BF16), r2(ln2_g[0]), r2(ln2_b[0]), alpha, tm_moe, n)
    return out.reshape(bsz, seq, d)
```
